```python
import math
import jax, jax.numpy as jnp
from jax import lax
import numpy as np

D_MODEL = 2048
BATCH = 4
SEQ = 8192
DEPTH = 2
DEC_BATCH = 16
DEC_SEQ = 16
PAST_LEN = 4096

CHUNK = 64
Q_BLOCK = 128
HEAD_DIM = 128
A_HEADS = 8
A_KV_HEADS = 2
A_GROUP = A_HEADS // A_KV_HEADS
IDX_HEADS = 8
IDX_DIM = 64
TOPK_MAX = 256
B_HEADS = 4
D_FF = 5504
N_EXPERTS = 8
TOP_K_EXPERTS = 2
MOE_BLOCK = 128
N_DENSE = (DEPTH + 1) // 2
N_MOE = DEPTH // 2
N_ADA = 6
NORM_EPS = 1e-6
SUBLN_EPS = 1e-5

A_Q = A_HEADS * HEAD_DIM
A_KV = A_KV_HEADS * HEAD_DIM
I_Q = IDX_HEADS * IDX_DIM
B_QK = B_HEADS * 2 * HEAD_DIM
B_V = B_HEADS * 2 * HEAD_DIM
IN_SIZES = (A_Q, A_KV, A_KV, I_Q, IDX_DIM, IDX_HEADS, B_QK, B_QK, B_V, D_MODEL, D_MODEL)
D_IN = A_Q + 2 * A_KV + I_Q + IDX_DIM + IDX_HEADS + 2 * B_QK + B_V + 2 * D_MODEL

kernel_name = 'hybrid_dsa_diffattn_stream_step'


def rms_norm(x, g, eps=NORM_EPS):
    xf = x.astype(jnp.float32)
    y = xf * lax.rsqrt(jnp.mean(xf * xf, axis=-1, keepdims=True) + eps)
    return (y * g.astype(jnp.float32)).astype(x.dtype)


def alibi_slopes(n):
    return 2.0 ** (-8.0 * jnp.arange(1, n + 1, dtype=jnp.float32) / n)


def ada_mod(c, w, b):
    m = jnp.einsum('bd,de->be', jax.nn.silu(c), w) + b
    return jnp.split(m[:, None, :], N_ADA, axis=-1)


def to_blocks(a):
    b, s = a.shape[:2]
    return jnp.swapaxes(a.reshape((b, s // Q_BLOCK, Q_BLOCK) + a.shape[2:]), 0, 1)


def from_blocks(a):
    nb, b, q = a.shape[:3]
    return jnp.swapaxes(a, 0, 1).reshape((b, nb * q) + a.shape[3:])


def chunk_admissible(q_pos, k_pos):
    return (k_pos[None, :] // CHUNK) <= (q_pos[:, None] // CHUNK)


def dsa_attend(q, iq, iw, q_pos, k, v, ik, k_pos, topk, slopes):
    adm = chunk_admissible(q_pos, k_pos)
    rel = jnp.einsum('bqhd,bld->bqhl', iq.astype(jnp.float32), ik.astype(jnp.float32))
    score = jnp.einsum('bqh,bqhl->bql', iw.astype(jnp.float32), jax.nn.relu(rel))
    score = jnp.where(adm[None], score, -jnp.inf)
    _, idx = lax.top_k(score, topk)
    sel_pos = k_pos[idx]
    valid = (sel_pos // CHUNK) <= (q_pos[None, :, None] // CHUNK)
    gather = jax.vmap(lambda a, i: a[i])
    k_sel = gather(k, idx)
    v_sel = gather(v, idx)
    logits = jnp.einsum('bqjgd,bqnjd->bqjgn', q, k_sel).astype(jnp.float32) * (HEAD_DIM ** -0.5)
    dist = jnp.abs(q_pos[None, :, None] - sel_pos).astype(jnp.float32)
    logits = logits - slopes[None, None, :, :, None] * dist[:, :, None, None, :]
    logits = jnp.where(valid[:, :, None, None, :], logits, -jnp.inf)
    p = jax.nn.softmax(logits, axis=-1).astype(v.dtype)
    return jnp.einsum('bqjgn,bqnjd->bqjgd', p, v_sel)


def diff_attend(q, q_pos, k, v, k_pos, lam, slopes):
    adm = chunk_admissible(q_pos, k_pos)
    logits = jnp.einsum('bqhcd,blhcd->bhcql', q, k).astype(jnp.float32) * (HEAD_DIM ** -0.5)
    dist = jnp.abs(q_pos[:, None] - k_pos[None, :]).astype(jnp.float32)
    logits = logits - slopes[None, :, None, None, None] * dist
    logits = jnp.where(adm, logits, -jnp.inf)
    p = jax.nn.softmax(logits, axis=-1)
    w = p[:, :, 0] - lam * p[:, :, 1]
    return jnp.einsum('bhql,blhe->bqhe', w.astype(v.dtype), v)


def token_mixers(h, past, l, p):
    B, S, _ = h.shape
    z = jnp.einsum('bsd,de->bse', h, p['w_in'][l])
    parts = []
    o = 0
    for n in IN_SIZES:
        parts.append(z[..., o:o + n])
        o += n
    aq, ak, av, iq, ik, iw, bq, bk, bv, gate_a, gate_b = parts
    aq = aq.reshape(B, S, A_KV_HEADS, A_GROUP, HEAD_DIM)
    ak = ak.reshape(B, S, A_KV_HEADS, HEAD_DIM)
    av = av.reshape(B, S, A_KV_HEADS, HEAD_DIM)
    iq = iq.reshape(B, S, IDX_HEADS, IDX_DIM)
    iw = iw * ((IDX_HEADS * IDX_DIM) ** -0.5)
    bq = bq.reshape(B, S, B_HEADS, 2, HEAD_DIM)
    bk = bk.reshape(B, S, B_HEADS, 2, HEAD_DIM)
    bv = bv.reshape(B, S, B_HEADS, 2 * HEAD_DIM)
    new_rows = (ak, av, ik, bk, bv)
    if past is None:
        q_pos = jnp.arange(S, dtype=jnp.int32)
        k_pos = q_pos
        keys = new_rows
    else:
        n_past = past[0].shape[1]
        q_pos = n_past + jnp.arange(S, dtype=jnp.int32)
        k_pos = jnp.arange(n_past + S, dtype=jnp.int32)
        keys = tuple(jnp.concatenate([pr, nr.astype(pr.dtype)], axis=1) for pr, nr in zip(past, new_rows))
    kk, kv, kik, kbk, kbv = keys
    topk = min(TOPK_MAX, k_pos.shape[0] // 4)
    slopes_a = alibi_slopes(A_HEADS).reshape(A_KV_HEADS, A_GROUP)
    slopes_b = alibi_slopes(B_HEADS)
    lam_init = 0.8 - 0.6 * math.exp(-0.3 * l)
    lq = p['lam_qk'][l].astype(jnp.float32)
    lam = jnp.exp(jnp.sum(lq[0] * lq[1])) - jnp.exp(jnp.sum(lq[2] * lq[3])) + lam_init

    def attend(q_a, iq_b, iw_b, q_b, qp):
        oa_ = dsa_attend(q_a, iq_b, iw_b, qp, kk, kv, kik, k_pos, topk, slopes_a)
        ob_ = diff_attend(q_b, qp, kbk, kbv, k_pos, lam, slopes_b)
        return oa_, ob_

    if past is None:
        oa, ob = lax.map(lambda t: attend(*t),
                         (to_blocks(aq), to_blocks(iq), to_blocks(iw), to_blocks(bq), q_pos.reshape(-1, Q_BLOCK)))
        oa, ob = from_blocks(oa), from_blocks(ob)
    else:
        oa, ob = attend(aq, iq, iw, bq, q_pos)
    ob = rms_norm(ob, p['g_subln'][l], SUBLN_EPS) * (1.0 - lam_init)
    ya = jnp.einsum('bse,ed->bsd', oa.reshape(B, S, A_Q), p['w_out_a'][l])
    yb = jnp.einsum('bse,ed->bsd', ob.reshape(B, S, B_V), p['w_out_b'][l])
    merged = jax.nn.sigmoid(gate_a) * ya + jax.nn.sigmoid(gate_b) * yb
    return jnp.einsum('bsd,de->bse', merged, p['w_out'][l]), new_rows


def swiglu(h, w_gate, w_up, w_down):
    u = jnp.einsum('bsd,df->bsf', h, w_gate)
    v = jnp.einsum('bsd,df->bsf', h, w_up)
    return jnp.einsum('bsf,fd->bsd', jax.nn.silu(u) * v, w_down)


def moe_ffn(h, w_router, b_router, w_gate, w_up, w_down):
    B, S, D = h.shape
    xt = h.reshape(-1, D)
    T = xt.shape[0]
    logits = jnp.einsum('td,de->te', xt, w_router).astype(jnp.float32) + b_router.astype(jnp.float32)
    top_val, top_idx = lax.top_k(logits, TOP_K_EXPERTS)
    gates = jax.nn.softmax(top_val, axis=-1)
    A = T * TOP_K_EXPERTS
    e_flat = top_idx.reshape(-1)
    tok_flat = jnp.arange(A, dtype=jnp.int32) // TOP_K_EXPERTS
    g_flat = gates.reshape(-1)
    order = jnp.argsort(e_flat)
    e_s, tok_s, g_s = e_flat[order], tok_flat[order], g_flat[order]
    counts = jnp.bincount(e_flat, length=N_EXPERTS)
    starts = jnp.cumsum(counts) - counts
    padded = ((counts + MOE_BLOCK - 1) // MOE_BLOCK) * MOE_BLOCK
    pends = jnp.cumsum(padded)
    pstarts = pends - padded
    dest = pstarts[e_s] + (jnp.arange(A, dtype=jnp.int32) - starts[e_s])
    n_blocks = -(-(A + N_EXPERTS * (MOE_BLOCK - 1)) // MOE_BLOCK)
    P = n_blocks * MOE_BLOCK
    tok_buf = jnp.zeros((P,), jnp.int32).at[dest].set(tok_s)
    g_buf = jnp.zeros((P,), jnp.float32).at[dest].set(g_s)
    block_expert = jnp.clip(jnp.searchsorted(pends, jnp.arange(n_blocks) * MOE_BLOCK, side='right'), 0, N_EXPERTS - 1)
    xb = xt[tok_buf].reshape(n_blocks, MOE_BLOCK, D)

    def expert_block(args):
        xblk, e = args
        u = xblk @ w_gate[e]
        v = xblk @ w_up[e]
        return (jax.nn.silu(u) * v) @ w_down[e]

    yb = lax.map(expert_block, (xb, block_expert)).reshape(P, D)
    y = jax.ops.segment_sum(yb * g_buf[:, None].astype(yb.dtype), tok_buf, num_segments=T)
    return y.reshape(B, S, D)


def trunk(x, c, past_all, p):
    rows = []
    for l in range(DEPTH):
        shift_a, scale_a, gate_a, shift_f, scale_f, gate_f = ada_mod(c, p['w_ada'][l], p['b_ada'][l])
        h = rms_norm(x, p['g_attn'][l]) * (1 + scale_a) + shift_a
        past = None if past_all is None else tuple(a[l] for a in past_all)
        mix, new_rows = token_mixers(h, past, l, p)
        x = x + gate_a * mix
        h = rms_norm(x, p['g_ffn'][l]) * (1 + scale_f) + shift_f
        if l % 2 == 0:
            i = l // 2
            f = swiglu(h, p['w_ff_gate'][i], p['w_ff_up'][i], p['w_ff_down'][i])
        else:
            i = l // 2
            f = moe_ffn(h, p['w_router'][i], p['b_router'][i], p['w_moe_gate'][i], p['w_moe_up'][i], p['w_moe_down'][i])
        x = x + gate_f * f
        rows.append(new_rows)
    y = rms_norm(x, p['g_final'])
    state = tuple(jnp.stack([r[j] for r in rows]) for j in range(5))
    return y, state


def setup_inputs(seed: int = 0) -> dict:
    key = jax.random.key(seed)
    ks = iter(jax.random.split(key, 32))
    D = D_MODEL

    def nrm(shape, scale):
        return scale * jax.random.normal(next(ks), shape, jnp.float32)

    def gain(shape):
        return 1.0 + nrm(shape, 0.05)

    return {
        'x_prompt': nrm((BATCH, SEQ, D), 1.0),
        'x_sample': nrm((DEC_BATCH, DEC_SEQ, D), 1.0),
        'c_prompt': nrm((BATCH, D), 1.0),
        'c_sample': nrm((DEC_BATCH, D), 1.0),
        'cache_dsa_k': nrm((DEPTH, DEC_BATCH, PAST_LEN, A_KV_HEADS, HEAD_DIM), 1.0),
        'cache_dsa_v': nrm((DEPTH, DEC_BATCH, PAST_LEN, A_KV_HEADS, HEAD_DIM), 1.0),
        'cache_idx_k': nrm((DEPTH, DEC_BATCH, PAST_LEN, IDX_DIM), 1.0),
        'cache_diff_k': nrm((DEPTH, DEC_BATCH, PAST_LEN, B_HEADS, 2, HEAD_DIM), 1.0),
        'cache_diff_v': nrm((DEPTH, DEC_BATCH, PAST_LEN, B_HEADS, 2 * HEAD_DIM), 1.0),
        'w_ada': nrm((DEPTH, D, N_ADA * D), 0.5 * D ** -0.5),
        'b_ada': nrm((DEPTH, N_ADA * D), 0.02),
        'g_attn': gain((DEPTH, D)),
        'w_in': nrm((DEPTH, D, D_IN), D ** -0.5),
        'w_out_a': nrm((DEPTH, A_Q, D), A_Q ** -0.5),
        'w_out_b': nrm((DEPTH, B_V, D), B_V ** -0.5),
        'w_out': nrm((DEPTH, D, D), D ** -0.5),
        'lam_qk': nrm((DEPTH, 4, HEAD_DIM), 0.1),
        'g_subln': gain((DEPTH, 2 * HEAD_DIM)),
        'g_ffn': gain((DEPTH, D)),
        'w_ff_gate': nrm((N_DENSE, D, D_FF), D ** -0.5),
        'w_ff_up': nrm((N_DENSE, D, D_FF), D ** -0.5),
        'w_ff_down': nrm((N_DENSE, D_FF, D), D_FF ** -0.5),
        'w_router': nrm((N_MOE, D, N_EXPERTS), D ** -0.5),
        'b_router': nrm((N_MOE, N_EXPERTS), 0.01),
        'w_moe_gate': nrm((N_MOE, N_EXPERTS, D, D_FF), D ** -0.5),
        'w_moe_up': nrm((N_MOE, N_EXPERTS, D, D_FF), D ** -0.5),
        'w_moe_down': nrm((N_MOE, N_EXPERTS, D_FF, D), D_FF ** -0.5),
        'g_final': gain((D,)),
    }


def reference(x_prompt, x_sample, c_prompt, c_sample, cache_dsa_k, cache_dsa_v, cache_idx_k, cache_diff_k,
              cache_diff_v, w_ada, b_ada, g_attn, w_in, w_out_a, w_out_b, w_out, lam_qk, g_subln, g_ffn,
              w_ff_gate, w_ff_up, w_ff_down, w_router, b_router, w_moe_gate, w_moe_up, w_moe_down, g_final):
    p = dict(w_ada=w_ada, b_ada=b_ada, g_attn=g_attn, w_in=w_in, w_out_a=w_out_a, w_out_b=w_out_b,
             w_out=w_out, lam_qk=lam_qk, g_subln=g_subln, g_ffn=g_ffn, w_ff_gate=w_ff_gate,
             w_ff_up=w_ff_up, w_ff_down=w_ff_down, w_router=w_router, b_router=b_router,
             w_moe_gate=w_moe_gate, w_moe_up=w_moe_up, w_moe_down=w_moe_down, g_final=g_final)
    y_prompt, (p_dsa_k, p_dsa_v, p_idx_k, p_diff_k, p_diff_v) = trunk(x_prompt, c_prompt, None, p)
    past_all = (cache_dsa_k, cache_dsa_v, cache_idx_k, cache_diff_k, cache_diff_v)
    y_sample, (s_dsa_k, s_dsa_v, s_idx_k, s_diff_k, s_diff_v) = trunk(x_sample, c_sample, past_all, p)
    return (y_prompt, y_sample, p_dsa_k, p_dsa_v, p_idx_k, p_diff_k, p_diff_v,
            s_dsa_k, s_dsa_v, s_idx_k, s_diff_k, s_diff_v)
```

```python
import functools
import math

import jax
import jax.numpy as jnp
from jax import lax
from jax.experimental import pallas as pl
from jax.experimental.pallas import tpu as pltpu

F32, BF16, I32 = jnp.float32, jnp.bfloat16, jnp.int32

CHUNK = 64
CHUNK_SHIFT = CHUNK.bit_length() - 1
HEAD_DIM = 128
A_HEADS = 8
A_KV_HEADS = 2
A_GROUP = A_HEADS // A_KV_HEADS
IDX_HEADS = 8
IDX_DIM = 64
TOPK_MAX = 256
B_HEADS = 4
N_EXPERTS = 8
TOP_K_EXPERTS = 2
N_ADA = 6
NORM_EPS = 1e-6
SUBLN_EPS = 1e-5
A_Q = A_HEADS * HEAD_DIM
A_KV = A_KV_HEADS * HEAD_DIM
I_Q = IDX_HEADS * IDX_DIM
B_QK = B_HEADS * 2 * HEAD_DIM
B_V = B_HEADS * 2 * HEAD_DIM

LANES = 128
VMEM_BYTES_V7X = 64 * 2 ** 20
INT_MIN = -2 ** 31
NEG = -1e30
NO_LIMIT = 2 ** 30


def _cparams(dims, block_bytes):
    limit = min(max(2 * int(block_bytes) + (8 << 20), 32 << 20), VMEM_BYTES_V7X - (6 << 20))
    return pltpu.CompilerParams(dimension_semantics=dims, vmem_limit_bytes=limit)


def _nbytes(shape, dtype):
    return math.prod(shape) * jnp.dtype(dtype).itemsize


def _tile(n, pref):
    if n <= pref:
        return n
    t = pref
    while n % t:
        t //= 2
    return t


def _round_up(n, m):
    return -(-n // m) * m


def _z_layout(d_model):
    off, o = {}, 0
    for name, n in (('aq', A_Q), ('ak', A_KV), ('av', A_KV), ('iq', I_Q), ('bq', B_QK), ('bk', B_QK),
                    ('bv', B_V), ('ga', d_model), ('gb', d_model), ('ik', LANES), ('iw', LANES)):
        off[name] = o
        o += n
    return off, o


def _prep_w_in(w, d_model):
    sizes = (A_Q, A_KV, A_KV, I_Q, IDX_DIM, IDX_HEADS, B_QK, B_QK, B_V, d_model, d_model)
    names = ('aq', 'ak', 'av', 'iq', 'ik', 'iw', 'bq', 'bk', 'bv', 'ga', 'gb')
    parts, o = {}, 0
    for n, s in zip(names, sizes):
        parts[n] = w[:, o:o + s]
        o += s
    zeros = jnp.zeros((w.shape[0], LANES - IDX_HEADS), w.dtype)
    cols = [parts[n] for n in ('aq', 'ak', 'av', 'iq', 'bq', 'bk', 'bv', 'ga', 'gb')]
    cols += [parts['ik'], parts['ik'], parts['iw'], zeros]
    return jnp.concatenate(cols, axis=1).astype(BF16)


def _ada_kernel(c_ref, w_ref, b_ref, o_ref):
    c = c_ref[...]
    s = c * jax.nn.sigmoid(c)
    o_ref[0] = jnp.dot(s.astype(BF16), w_ref[0].astype(BF16), preferred_element_type=F32) + b_ref[0]


def _ada_mod(c_all, w_ada, b_ada):
    depth, d, n = w_ada.shape
    nb = c_all.shape[0]
    tn = _tile(n, 1024)
    blocks = _nbytes((nb, d), F32) + _nbytes((d, tn), F32) * 2 + _nbytes((nb, tn), F32)
    return pl.pallas_call(
        _ada_kernel,
        grid=(depth, n // tn),
        in_specs=[pl.BlockSpec((nb, d), lambda l, j: (0, 0)),
                  pl.BlockSpec((1, d, tn), lambda l, j: (l, 0, j)),
                  pl.BlockSpec((1, 1, tn), lambda l, j: (l, 0, j))],
        out_specs=pl.BlockSpec((1, nb, tn), lambda l, j: (l, 0, j)),
        out_shape=jax.ShapeDtypeStruct((depth, nb, n), F32),
        compiler_params=_cparams(("arbitrary", "arbitrary"), blocks),
        name="ada_mod",
    )(c_all, w_ada, b_ada.reshape(depth, 1, n))


def _norm_kernel(*refs, modulated, router, eps):
    refs = list(refs)
    x_ref, g_ref = refs[:2]
    pos = 2
    x = x_ref[...]
    y = x * lax.rsqrt(jnp.mean(x * x, axis=-1, keepdims=True) + eps)
    y = y * g_ref[...]
    if modulated:
        scale_ref, shift_ref = refs[pos:pos + 2]
        pos += 2
        y = y * (1.0 + scale_ref[0]) + shift_ref[0]
    if router:
        wr_ref, br_ref = refs[pos:pos + 2]
        pos += 2
    o_ref = refs[pos]
    o_ref[...] = y.astype(o_ref.dtype)
    if router:
        idx_ref, gate_ref = refs[pos + 1:pos + 3]
        logits = jnp.dot(y, wr_ref[...], preferred_element_type=F32,
                         precision=lax.Precision.HIGHEST) + br_ref[...]
        lane = lax.broadcasted_iota(I32, logits.shape, 1)
        logits = jnp.where(lane < N_EXPERTS, logits, -jnp.inf)
        m1 = jnp.max(logits, axis=-1, keepdims=True)
        i1 = jnp.min(jnp.where(logits == m1, lane, LANES), axis=-1, keepdims=True)
        rest = jnp.where(lane == i1, -jnp.inf, logits)
        m2 = jnp.max(rest, axis=-1, keepdims=True)
        i2 = jnp.min(jnp.where(rest == m2, lane, LANES), axis=-1, keepdims=True)
        e = jnp.exp(m2 - m1)
        g1 = 1.0 / (1.0 + e)
        g2 = e / (1.0 + e)
        idx_ref[...] = jnp.where(lane == 0, i1, jnp.where(lane == 1, i2, 0))
        gate_ref[...] = jnp.where(lane == 0, g1, jnp.where(lane == 1, g2, 0.0))


def _norm(x2, g, seq, *, scale=None, shift=None, router=None, out_dtype, eps=NORM_EPS, name):
    t, d = x2.shape
    tm = _tile(seq, 256)
    per_b = seq // tm
    in_specs = [pl.BlockSpec((tm, d), lambda i: (i, 0)), pl.BlockSpec((1, d), lambda i: (0, 0))]
    args = [x2, g.reshape(1, d)]
    if scale is not None:
        in_specs += [pl.BlockSpec((1, 1, d), lambda i: (i // per_b, 0, 0))] * 2
        args += [scale, shift]
    out_specs = [pl.BlockSpec((tm, d), lambda i: (i, 0))]
    out_shape = [jax.ShapeDtypeStruct((t, d), out_dtype)]
    if router is not None:
        w_r, b_r = router
        in_specs += [pl.BlockSpec((d, LANES), lambda i: (0, 0)), pl.BlockSpec((1, LANES), lambda i: (0, 0))]
        args += [w_r, b_r]
        out_specs += [pl.BlockSpec((tm, LANES), lambda i: (i, 0))] * 2
        out_shape += [jax.ShapeDtypeStruct((t, LANES), I32), jax.ShapeDtypeStruct((t, LANES), F32)]
    blocks = 3 * _nbytes((tm, d), F32) + _nbytes((d, LANES), F32)
    outs = pl.pallas_call(
        functools.partial(_norm_kernel, modulated=scale is not None, router=router is not None, eps=eps),
        grid=(t // tm,), in_specs=in_specs, out_specs=out_specs, out_shape=out_shape,
        compiler_params=_cparams(("arbitrary",), blocks), name=name,
    )(*args)
    return outs if router is not None else outs[0]


def _mm_kernel(*refs, n_a, a_of_w, n_extra, epi, has_eids):
    refs = list(refs)
    if has_eids:
        refs = refs[1:]
    n_w = len(a_of_w)
    a_refs = refs[:n_a]
    w_refs = refs[n_a:n_a + n_w]
    e_refs = refs[n_a + n_w:n_a + n_w + n_extra]
    o_refs = refs[n_a + n_w + n_extra:]
    a_vals = [a[...].astype(BF16) for a in a_refs]
    accs = []
    for ai, w_ref in zip(a_of_w, w_refs):
        w = w_ref[0] if len(w_ref.shape) == 3 else w_ref[...]
        accs.append(jnp.dot(a_vals[ai], w, preferred_element_type=F32))
    outs = epi(accs, [e[...] for e in e_refs])
    for o_ref, o in zip(o_refs, outs):
        o_ref[...] = o.astype(o_ref.dtype)


def _matmul(a_list, w_list, a_of_w, extras, epi, out_dtypes, n_cols, *, tm, tn, eids=None, name):
    m = a_list[0].shape[0]
    in_specs, blocks = [], 0
    for a in a_list:
        in_specs.append(pl.BlockSpec((tm, a.shape[1]), lambda i, j, *_: (i, 0)))
        blocks += _nbytes((tm, a.shape[1]), a.dtype)
    for w in w_list:
        if w.ndim == 2:
            in_specs.append(pl.BlockSpec((w.shape[0], tn), lambda i, j, *_: (0, j)))
        else:
            in_specs.append(pl.BlockSpec((1, w.shape[1], tn), lambda i, j, e: (e[i], 0, j)))
        blocks += _nbytes((w.shape[-2], tn), w.dtype)
    for arr, bs, im in extras:
        in_specs.append(pl.BlockSpec(bs, im))
        blocks += _nbytes(bs, arr.dtype)
    out_specs = [pl.BlockSpec((tm, tn), lambda i, j, *_: (i, j)) for _ in out_dtypes]
    out_shape = [jax.ShapeDtypeStruct((m, n_cols), dt) for dt in out_dtypes]
    blocks += sum(_nbytes((tm, tn), dt) for dt in out_dtypes) + len(w_list) * _nbytes((tm, tn), F32)
    kern = functools.partial(_mm_kernel, n_a=len(a_list), a_of_w=tuple(a_of_w), n_extra=len(extras),
                             epi=epi, has_eids=eids is not None)
    grid_spec = pltpu.PrefetchScalarGridSpec(
        num_scalar_prefetch=0 if eids is None else 1, grid=(m // tm, n_cols // tn),
        in_specs=in_specs, out_specs=out_specs)
    args = ([] if eids is None else [eids]) + list(a_list) + list(w_list) + [e[0] for e in extras]
    outs = pl.pallas_call(kern, grid_spec=grid_spec, out_shape=out_shape,
                          compiler_params=_cparams(("arbitrary", "arbitrary"), blocks), name=name)(*args)
    return outs


def _nt_dot(a, b):
    return lax.dot_general(a, b, (((1,), (1,)), ((), ())), preferred_element_type=F32)


def _softmax_step(logits, vt, m, l, acc):
    m_new = jnp.maximum(m, jnp.max(logits, axis=-1, keepdims=True))
    alpha = jnp.exp(m - m_new)
    p = jnp.exp(logits - m_new)
    l = alpha * l + jnp.sum(p, axis=-1, keepdims=True)
    acc = alpha * acc + jnp.dot(p.astype(BF16), vt, preferred_element_type=F32)
    return m_new, l, acc


def _num_kv_tiles(q0, tq, tk, l_valid):
    kmax = jnp.minimum(l_valid, (((q0 + tq - 1) >> CHUNK_SHIFT) + 1) * CHUNK)
    return (kmax + tk - 1) // tk


def _dsa_kernel(qa_ref, iq_ref, iw_ref, k_ref, v_ref, ik_ref, o_ref, keys_ref, bias_ref, jlim_ref, *,
                tq, tk, n_past, l_valid, topk, index_bits):
    q0 = n_past + pl.program_id(1) * tq
    n_kv = _num_kv_tiles(q0, tq, tk, l_valid)
    row = lax.broadcasted_iota(I32, (tq, tk), 0)
    col = lax.broadcasted_iota(I32, (tq, tk), 1)
    qpos = q0 + row
    lane = lax.broadcasted_iota(I32, (tq, LANES), 1)

    iw = iw_ref[0] * (I_Q ** -0.5)
    iq = iq_ref[0].astype(F32)
    iq_heads = []
    for h in range(IDX_HEADS):
        pair = iq[:, (h // 2) * LANES:(h // 2 + 1) * LANES]
        keep = (lane < IDX_DIM) if h % 2 == 0 else (lane >= IDX_DIM)
        iq_heads.append(jnp.where(keep, pair, 0.0).astype(BF16))

    def score_tile(t, carry):
        start = pl.multiple_of(t * tk, tk)
        ikt = ik_ref[0, pl.ds(start, tk), :]
        score = jnp.zeros((tq, tk), F32)
        for h in range(IDX_HEADS):
            score = score + iw[:, h:h + 1] * jnp.maximum(_nt_dot(iq_heads[h], ikt), 0.0)
        kpos = start + col
        adm = ((kpos >> CHUNK_SHIFT) <= (qpos >> CHUNK_SHIFT)) & (kpos < l_valid)
        bits = lax.bitcast_convert_type(score, I32)
        key = bits ^ ((bits >> 31) & 0x7FFFFFFF)
        keys_ref[t] = jnp.where(adm, key, INT_MIN)
        return carry

    lax.fori_loop(0, n_kv, score_tile, 0)

    def count(pred):
        def body(t, acc):
            x = jnp.where(pred(keys_ref[t], t), 1.0, 0.0)
            part = x[:, 0:LANES]
            for c in range(1, tk // LANES):
                part = part + x[:, c * LANES:(c + 1) * LANES]
            return acc + part
        acc = lax.fori_loop(0, n_kv, body, jnp.zeros((tq, LANES), F32))
        return jnp.sum(acc, axis=-1, keepdims=True)

    def value_bit(it, res):
        cand = res | lax.shift_left(jnp.int32(1), 31 - it)
        cand_key = cand ^ INT_MIN
        cnt = count(lambda k, t: k >= cand_key)
        return jnp.where(cnt >= topk, cand, res)

    thr = lax.fori_loop(0, 32, value_bit, jnp.zeros((tq, 1), I32)) ^ INT_MIN
    n_gt = count(lambda k, t: k > thr)
    n_ge = count(lambda k, t: k >= thr)
    want_ties = topk - n_gt
    need = ((n_ge - n_gt) > want_ties) & (thr != INT_MIN)
    jlim_ref[...] = jnp.full((tq, LANES), NO_LIMIT, I32)

    @pl.when(jnp.max(jnp.where(need, 1.0, 0.0)) > 0.0)
    def _():
        def index_bit(it, j_lim):
            cand = j_lim | lax.shift_left(jnp.int32(1), index_bits - 1 - it)
            below = count(lambda k, t: (k == thr) & ((t * tk + col) < cand))
            return jnp.where(below <= want_ties - 1.0, cand, j_lim)
        j_lim = lax.fori_loop(0, index_bits, index_bit, jnp.zeros((tq, 1), I32))
        jlim_ref[...] = jnp.broadcast_to(jnp.where(need, j_lim, NO_LIMIT), (tq, LANES))

    j_lim = jlim_ref[:, 0:1]

    def bias_tile(t, carry):
        k = keys_ref[t]
        kpos = t * tk + col
        sel = (k > thr) | ((k == thr) & (kpos <= j_lim))
        sel = sel & (k != INT_MIN)
        bias_ref[t] = jnp.where(sel, 0.0, NEG)
        return carry

    lax.fori_loop(0, n_kv, bias_tile, 0)

    scale = HEAD_DIM ** -0.5
    for j in range(A_KV_HEADS):
        q4 = jnp.concatenate([qa_ref[0, :, (j * A_GROUP + g) * HEAD_DIM:(j * A_GROUP + g + 1) * HEAD_DIM]
                              for g in range(A_GROUP)], axis=0)

        def kv_tile(t, carry, j=j, q4=q4):
            m, l, acc = carry
            start = pl.multiple_of(t * tk, tk)
            kt = k_ref[0, pl.ds(start, tk), j * HEAD_DIM:(j + 1) * HEAD_DIM]
            vt = v_ref[0, pl.ds(start, tk), j * HEAD_DIM:(j + 1) * HEAD_DIM]
            s = _nt_dot(q4, kt) * scale
            dist = jnp.abs(qpos - (start + col)).astype(F32)
            bt = bias_ref[t]
            logits = jnp.concatenate(
                [s[g * tq:(g + 1) * tq] - 2.0 ** -(j * A_GROUP + g + 1) * dist + bt for g in range(A_GROUP)],
                axis=0)
            return _softmax_step(logits, vt, m, l, acc)

        rows = A_GROUP * tq
        m, l, acc = lax.fori_loop(
            0, n_kv, kv_tile,
            (jnp.full((rows, 1), NEG, F32), jnp.zeros((rows, 1), F32), jnp.zeros((rows, HEAD_DIM), F32)))
        out = acc / l
        for g in range(A_GROUP):
            h = j * A_GROUP + g
            o_ref[0, :, h * HEAD_DIM:(h + 1) * HEAD_DIM] = out[g * tq:(g + 1) * tq].astype(o_ref.dtype)


def _dsa_attention(zb, zf, keys, off, *, n_past, l_valid, tq, tk, topk, name):
    b, s, _ = zb.shape
    karr, kcol, varr, vcol, ikarr, ikcol = keys
    lp = karr.shape[1]
    n_tiles = lp // tk
    kern = functools.partial(_dsa_kernel, tq=tq, tk=tk, n_past=n_past, l_valid=l_valid, topk=topk,
                             index_bits=lp.bit_length())
    blocks = (_nbytes((tq, A_Q + I_Q), BF16) + _nbytes((tq, LANES), F32) + 2 * _nbytes((lp, A_KV), BF16)
              + _nbytes((lp, LANES), BF16) + _nbytes((tq, A_Q), BF16) + _nbytes((tq, lp), F32)
              + 8 * _nbytes((A_GROUP * tq, tk), F32))
    return pl.pallas_call(
        kern,
        grid=(b, s // tq),
        in_specs=[pl.BlockSpec((1, tq, A_Q), lambda bi, qi: (bi, qi, off['aq'] // A_Q)),
                  pl.BlockSpec((1, tq, I_Q), lambda bi, qi: (bi, qi, off['iq'] // I_Q)),
                  pl.BlockSpec((1, tq, LANES), lambda bi, qi: (bi, qi, off['iw'] // LANES)),
                  pl.BlockSpec((1, lp, A_KV), lambda bi, qi: (bi, 0, kcol)),
                  pl.BlockSpec((1, lp, A_KV), lambda bi, qi: (bi, 0, vcol)),
                  pl.BlockSpec((1, lp, LANES), lambda bi, qi: (bi, 0, ikcol))],
        out_specs=pl.BlockSpec((1, tq, A_Q), lambda bi, qi: (bi, qi, 0)),
        out_shape=jax.ShapeDtypeStruct((b, s, A_Q), BF16),
        scratch_shapes=[pltpu.VMEM((n_tiles, tq, tk), I32), pltpu.VMEM((n_tiles, tq, tk), F32),
                        pltpu.VMEM((tq, LANES), I32)],
        compiler_params=_cparams(("arbitrary", "arbitrary"), blocks),
        name=name,
    )(zb, zb, zf, karr, varr, ikarr)


def _diff_kernel(q_ref, k_ref, v_ref, lq_ref, g_ref, o_ref, *, tq, tk, n_past, l_valid, lam_init):
    h = pl.program_id(1)
    q0 = n_past + pl.program_id(2) * tq
    n_kv = _num_kv_tiles(q0, tq, tk, l_valid)
    row = lax.broadcasted_iota(I32, (tq, tk), 0)
    col = lax.broadcasted_iota(I32, (tq, tk), 1)
    qpos = q0 + row
    slope = jnp.float32(1.0)
    for hh in range(B_HEADS):
        slope = jnp.where(h == hh, jnp.float32(2.0 ** (-8.0 * (hh + 1) / B_HEADS)), slope)
    scale = HEAD_DIM ** -0.5
    qc = [q_ref[0, :, c * HEAD_DIM:(c + 1) * HEAD_DIM] for c in range(2)]

    def kv_tile(t, carry):
        m, l, acc = carry
        start = pl.multiple_of(t * tk, tk)
        kpos = start + col
        adm = ((kpos >> CHUNK_SHIFT) <= (qpos >> CHUNK_SHIFT)) & (kpos < l_valid)
        dist = jnp.abs(qpos - kpos).astype(F32)
        parts = []
        for c in range(2):
            kt = k_ref[0, pl.ds(start, tk), c * HEAD_DIM:(c + 1) * HEAD_DIM]
            s = _nt_dot(qc[c], kt) * scale - slope * dist
            parts.append(jnp.where(adm, s, NEG))
        vt = v_ref[0, pl.ds(start, tk), :]
        return _softmax_step(jnp.concatenate(parts, axis=0), vt, m, l, acc)

    m, l, acc = lax.fori_loop(
        0, n_kv, kv_tile,
        (jnp.full((2 * tq, 1), NEG, F32), jnp.zeros((2 * tq, 1), F32), jnp.zeros((2 * tq, 2 * HEAD_DIM), F32)))
    out = acc / l
    lq = lq_ref[...]
    lam = (jnp.exp(jnp.sum(lq[0:1] * lq[1:2], axis=-1, keepdims=True))
           - jnp.exp(jnp.sum(lq[2:3] * lq[3:4], axis=-1, keepdims=True)) + lam_init)
    o = out[:tq] - lam * out[tq:]
    y = o * lax.rsqrt(jnp.mean(o * o, axis=-1, keepdims=True) + SUBLN_EPS)
    y = (y * g_ref[...]) * (1.0 - lam_init)
    o_ref[0] = y.astype(o_ref.dtype)


def _diff_attention(zb, keys, lam_qk_l, g_subln_l, off, *, n_past, l_valid, tq, tk, lam_init, name):
    b, s, _ = zb.shape
    karr, kcol0, varr, vcol0 = keys
    lp = karr.shape[1]
    w = 2 * HEAD_DIM
    kern = functools.partial(_diff_kernel, tq=tq, tk=tk, n_past=n_past, l_valid=l_valid, lam_init=lam_init)
    blocks = (2 * _nbytes((tq, w), BF16) + 2 * _nbytes((lp, w), BF16) + 8 * _nbytes((2 * tq, tk), F32))
    return pl.pallas_call(
        kern,
        grid=(b, B_HEADS, s // tq),
        in_specs=[pl.BlockSpec((1, tq, w), lambda bi, h, qi: (bi, qi, off['bq'] // w + h)),
                  pl.BlockSpec((1, lp, w), lambda bi, h, qi: (bi, 0, kcol0 + h)),
                  pl.BlockSpec((1, lp, w), lambda bi, h, qi: (bi, 0, vcol0 + h)),
                  pl.BlockSpec((4, HEAD_DIM), lambda bi, h, qi: (0, 0)),
                  pl.BlockSpec((1, w), lambda bi, h, qi: (0, 0))],
        out_specs=pl.BlockSpec((1, tq, w), lambda bi, h, qi: (bi, qi, h)),
        out_shape=jax.ShapeDtypeStruct((b, s, B_V), BF16),
        compiler_params=_cparams(("arbitrary", "arbitrary", "arbitrary"), blocks),
        name=name,
    )(zb, karr, varr, lam_qk_l, g_subln_l.reshape(1, w))


def _gather_kernel(idx_ref, src_ref, o_ref, sem, *, rows):
    base = pl.program_id(0) * rows

    def row_copy(r):
        return pltpu.make_async_copy(src_ref.at[pl.ds(idx_ref[base + r], 1)], o_ref.at[pl.ds(r, 1)], sem)

    def start(r, carry):
        row_copy(r).start()
        return carry

    def wait(r, carry):
        row_copy(r).wait()
        return carry

    lax.fori_loop(0, rows, start, 0)
    lax.fori_loop(0, rows, wait, 0)


def _gather_rows(src, idx, *, name):
    n = idx.shape[0]
    d = src.shape[1]
    rows = _tile(n, 128)
    return pl.pallas_call(
        functools.partial(_gather_kernel, rows=rows),
        grid_spec=pltpu.PrefetchScalarGridSpec(
            num_scalar_prefetch=1, grid=(n // rows,),
            in_specs=[pl.BlockSpec(memory_space=pl.ANY)],
            out_specs=pl.BlockSpec((rows, d), lambda i, idx_ref: (i, 0)),
            scratch_shapes=[pltpu.SemaphoreType.DMA(())]),
        out_shape=jax.ShapeDtypeStruct((n, d), src.dtype),
        compiler_params=_cparams(("arbitrary",), _nbytes((rows, d), src.dtype)),
        name=name,
    )(idx, src)


def _combine_kernel(x_ref, y0_ref, y1_ref, rg_ref, gate_ref, o_ref):
    rg = rg_ref[...]
    f = rg[:, 0:1] * y0_ref[...] + rg[:, 1:2] * y1_ref[...]
    o_ref[...] = x_ref[...] + gate_ref[0] * f


def _moe_combine(x2, yg2, rgate, gate_f, seq, *, name):
    t, d = x2.shape
    tm = _tile(seq, 256)
    per_b = seq // tm
    return pl.pallas_call(
        _combine_kernel,
        grid=(t // tm,),
        in_specs=[pl.BlockSpec((tm, d), lambda i: (i, 0)),
                  pl.BlockSpec((tm, d), lambda i: (i, 0)),
                  pl.BlockSpec((tm, d), lambda i: (i, 1)),
                  pl.BlockSpec((tm, LANES), lambda i: (i, 0)),
                  pl.BlockSpec((1, 1, d), lambda i: (i // per_b, 0, 0))],
        out_specs=pl.BlockSpec((tm, d), lambda i: (i, 0)),
        out_shape=jax.ShapeDtypeStruct((t, d), F32),
        compiler_params=_cparams(("arbitrary",), 4 * _nbytes((tm, d), F32)),
        name=name,
    )(x2, yg2, yg2, rgate, gate_f)


def _silu_mul(accs, extras):
    u, v = accs
    return [(u * jax.nn.sigmoid(u)) * v]


def _moe_ffn(x2, h2, ridx, rgate, gate_f, seq, w_gate, w_up, w_down, *, tag):
    t, d = x2.shape
    fp = w_gate.shape[-1]
    a = t * TOP_K_EXPERTS
    blk = _tile(a, 512)
    n_blocks = -(-(a + N_EXPERTS * (blk - 1)) // blk)
    p = n_blocks * blk
    e_flat = ridx[:, :TOP_K_EXPERTS].reshape(-1)
    onehot = (e_flat[:, None] == jnp.arange(N_EXPERTS, dtype=I32)[None, :]).astype(I32)
    rank = jnp.sum((jnp.cumsum(onehot, axis=0) - onehot) * onehot, axis=1)
    counts = jnp.sum(onehot, axis=0)
    padded = ((counts + blk - 1) // blk) * blk
    pends = jnp.cumsum(padded)
    pstarts = pends - padded
    dest = (pstarts[e_flat] + rank).astype(I32)
    tok_buf = jnp.zeros((p,), I32).at[dest].set(jnp.arange(a, dtype=I32) // TOP_K_EXPERTS)
    block_expert = jnp.clip(jnp.searchsorted(pends, jnp.arange(n_blocks, dtype=I32) * blk, side='right'),
                            0, N_EXPERTS - 1).astype(I32)

    xb = _gather_rows(h2, tok_buf, name=f"moe_gather_{tag}")
    tn = _tile(fp, 512)
    (act,) = _matmul([xb], [w_gate, w_up], [0, 0], [], _silu_mul, [BF16], fp, tm=blk, tn=tn,
                     eids=block_expert, name=f"moe_up_{tag}")
    (yb,) = _matmul([act], [w_down], [0], [], lambda accs, ex: accs, [F32], d, tm=blk, tn=_tile(d, 512),
                    eids=block_expert, name=f"moe_down_{tag}")
    yg = _gather_rows(yb, dest, name=f"moe_ungather_{tag}")
    return _moe_combine(x2, yg.reshape(t, TOP_K_EXPERTS * d), rgate, gate_f, seq, name=f"moe_combine_{tag}")


def _trunk(x, mods, past_all, wts, tag):
    b, s, d = x.shape
    t = b * s
    depth = len(wts['w_in'])
    off, nz = _z_layout(d)
    x2 = x.reshape(t, d)
    tn_d = _tile(d, 512)
    tm_b = _tile(s, 512)
    tm_f = _tile(t, 1024)
    per_b = s // tm_b
    rows = []
    for l in range(depth):
        shift_a, scale_a, gate_a, shift_f, scale_f, gate_f = [
            mods[l][:, i * d:(i + 1) * d].reshape(b, 1, d) for i in range(N_ADA)]
        batch_tile = lambda i, j, *_: (i // per_b, 0, j)

        h = _norm(x2, wts['g_attn'][l], s, scale=scale_a, shift=shift_a, out_dtype=BF16,
                  name=f"norm_attn_{tag}{l}")
        zf, zb = _matmul([h], [wts['w_in'][l]], [0], [], lambda accs, ex: [accs[0], accs[0]], [F32, BF16], nz,
                         tm=tm_f, tn=_tile(nz, 512) if nz % 512 == 0 else 256, name=f"in_proj_{tag}{l}")
        zf3, zb3 = zf.reshape(b, s, nz), zb.reshape(b, s, nz)
        ak = zf3[:, :, off['ak']:off['ak'] + A_KV].reshape(b, s, A_KV_HEADS, HEAD_DIM)
        av = zf3[:, :, off['av']:off['av'] + A_KV].reshape(b, s, A_KV_HEADS, HEAD_DIM)
        ik = zf3[:, :, off['ik']:off['ik'] + IDX_DIM]
        bk = zf3[:, :, off['bk']:off['bk'] + B_QK].reshape(b, s, B_HEADS, 2, HEAD_DIM)
        bv = zf3[:, :, off['bv']:off['bv'] + B_V].reshape(b, s, B_HEADS, 2 * HEAD_DIM)
        rows.append((ak, av, ik, bk, bv))

        if past_all is None:
            n_past, l_valid = 0, s
            tk = _tile(s, 512)
            tq_a, tq_b = _tile(s, 128), _tile(s, 256)
            dsa_keys = (zb3, off['ak'] // A_KV, zb3, off['av'] // A_KV, zb3, off['ik'] // LANES)
            diff_keys = (zb3, off['bk'] // (2 * HEAD_DIM), zb3, off['bv'] // (2 * HEAD_DIM))
        else:
            n_past = past_all[0].shape[2]
            l_valid = n_past + s
            tk = 512
            lp = _round_up(l_valid, tk)
            tq_a = tq_b = s

            def with_past(cache, new, width):
                full = jnp.concatenate([cache[l].reshape(b, n_past, width), new.reshape(b, s, width)], axis=1)
                return jnp.pad(full.astype(BF16), ((0, 0), (0, lp - l_valid), (0, 0)))

            kik = with_past(past_all[2], ik, IDX_DIM)
            dsa_keys = (with_past(past_all[0], ak, A_KV), 0, with_past(past_all[1], av, A_KV), 0,
                        jnp.concatenate([kik, kik], axis=-1), 0)
            diff_keys = (with_past(past_all[3], bk, B_QK), 0, with_past(past_all[4], bv, B_V), 0)
        topk = min(TOPK_MAX, l_valid // 4)
        oa = _dsa_attention(zb3, zf3, dsa_keys, off, n_past=n_past, l_valid=l_valid, tq=tq_a, tk=tk,
                            topk=topk, name=f"dsa_{tag}{l}")
        lam_init = 0.8 - 0.6 * math.exp(-0.3 * l)
        ob = _diff_attention(zb3, diff_keys, wts['lam_qk'][l], wts['g_subln'][l], off, n_past=n_past,
                             l_valid=l_valid, tq=tq_b, tk=tk, lam_init=lam_init, name=f"diff_{tag}{l}")

        def merge_epi(accs, ex):
            return [jax.nn.sigmoid(ex[0]) * accs[0] + jax.nn.sigmoid(ex[1]) * accs[1]]

        (merged,) = _matmul(
            [oa.reshape(t, A_Q), ob.reshape(t, B_V)], [wts['w_out_a'][l], wts['w_out_b'][l]], [0, 1],
            [(zf, (tm_f, tn_d), lambda i, j, *_: (i, off['ga'] // tn_d + j)),
             (zf, (tm_f, tn_d), lambda i, j, *_: (i, off['gb'] // tn_d + j))],
            merge_epi, [BF16], d, tm=tm_f, tn=tn_d, name=f"merge_{tag}{l}")

        def resid_epi(accs, ex):
            return [ex[0] + ex[1][0] * accs[0]]

        (x2,) = _matmul([merged], [wts['w_out'][l]], [0],
                        [(x2, (tm_b, tn_d), lambda i, j, *_: (i, j)), (gate_a, (1, 1, tn_d), batch_tile)],
                        resid_epi, [F32], d, tm=tm_b, tn=tn_d, name=f"out_proj_{tag}{l}")

        if l % 2 == 0:
            i = l // 2
            h = _norm(x2, wts['g_ffn'][l], s, scale=scale_f, shift=shift_f, out_dtype=BF16,
                      name=f"norm_ffn_{tag}{l}")
            fp = wts['w_ff_gate'][i].shape[-1]
            (act,) = _matmul([h], [wts['w_ff_gate'][i], wts['w_ff_up'][i]], [0, 0], [], _silu_mul, [BF16], fp,
                             tm=tm_f, tn=_tile(fp, 512), name=f"ffn_up_{tag}{l}")
            (x2,) = _matmul([act], [wts['w_ff_down'][i]], [0],
                            [(x2, (tm_b, tn_d), lambda i, j, *_: (i, j)), (gate_f, (1, 1, tn_d), batch_tile)],
                            resid_epi, [F32], d, tm=tm_b, tn=tn_d, name=f"ffn_down_{tag}{l}")
        else:
            i = l // 2
            h2, ridx, rgate = _norm(x2, wts['g_ffn'][l], s, scale=scale_f, shift=shift_f,
                                    router=(wts['w_router'][i], wts['b_router'][i]), out_dtype=F32,
                                    name=f"norm_router_{tag}{l}")
            x2 = _moe_ffn(x2, h2, ridx, rgate, gate_f, s, wts['w_moe_gate'][i], wts['w_moe_up'][i],
                          wts['w_moe_down'][i], tag=f"{tag}{l}")
    y = _norm(x2, wts['g_final'], s, out_dtype=F32, name=f"norm_final_{tag}")
    state = tuple(jnp.stack([r[j] for r in rows]) for j in range(5))
    return y.reshape(b, s, d), state


def _prep_weights(w_in, w_out_a, w_out_b, w_out, w_ff_gate, w_ff_up, w_ff_down, w_router, b_router,
                  w_moe_gate, w_moe_up, w_moe_down, d_model):
    f = w_ff_gate.shape[-1]
    fp = _round_up(f, 512)
    pad_cols = lambda w: jnp.pad(w.astype(BF16), [(0, 0)] * (w.ndim - 1) + [(0, fp - f)])
    pad_rows = lambda w: jnp.pad(w.astype(BF16), [(0, 0)] * (w.ndim - 2) + [(0, fp - f), (0, 0)])
    n_moe = w_router.shape[0]
    return dict(
        w_in=[_prep_w_in(w_in[l], d_model) for l in range(w_in.shape[0])],
        w_out_a=w_out_a.astype(BF16), w_out_b=w_out_b.astype(BF16), w_out=w_out.astype(BF16),
        w_ff_gate=pad_cols(w_ff_gate), w_ff_up=pad_cols(w_ff_up), w_ff_down=pad_rows(w_ff_down),
        w_router=[jnp.pad(w_router[i], ((0, 0), (0, LANES - N_EXPERTS))) for i in range(n_moe)],
        b_router=[jnp.pad(b_router[i], (0, LANES - N_EXPERTS)).reshape(1, LANES) for i in range(n_moe)],
        w_moe_gate=pad_cols(w_moe_gate), w_moe_up=pad_cols(w_moe_up), w_moe_down=pad_rows(w_moe_down),
    )


def kernel(x_prompt, x_sample, c_prompt, c_sample, cache_dsa_k, cache_dsa_v, cache_idx_k, cache_diff_k, cache_diff_v, w_ada, b_ada, g_attn, w_in, w_out_a, w_out_b, w_out, lam_qk, g_subln, g_ffn, w_ff_gate, w_ff_up, w_ff_down, w_router, b_router, w_moe_gate, w_moe_up, w_moe_down, g_final):
    d = x_prompt.shape[-1]
    wts = _prep_weights(w_in, w_out_a, w_out_b, w_out, w_ff_gate, w_ff_up, w_ff_down, w_router, b_router,
                        w_moe_gate, w_moe_up, w_moe_down, d)
    wts.update(g_attn=g_attn, g_ffn=g_ffn, g_final=g_final, lam_qk=lam_qk, g_subln=g_subln)
    nb_p = c_prompt.shape[0]
    mods = _ada_mod(jnp.concatenate([c_prompt, c_sample], axis=0), w_ada, b_ada)
    y_p, st_p = _trunk(x_prompt, mods[:, :nb_p], None, wts, "p")
    past_all = (cache_dsa_k, cache_dsa_v, cache_idx_k, cache_diff_k, cache_diff_v)
    y_s, st_s = _trunk(x_sample, mods[:, nb_p:], past_all, wts, "s")
    return (y_p, y_s) + st_p + st_s
```

```python
import functools
import math

import jax
import jax.numpy as jnp
from jax import lax
from jax.experimental import pallas as pl
from jax.experimental.pallas import tpu as pltpu

F32, BF16, I32 = jnp.float32, jnp.bfloat16, jnp.int32

CHUNK = 64
CHUNK_SHIFT = CHUNK.bit_length() - 1
HEAD_DIM = 128
A_HEADS = 8
A_KV_HEADS = 2
A_GROUP = A_HEADS // A_KV_HEADS
IDX_HEADS = 8
IDX_DIM = 64
TOPK_MAX = 256
B_HEADS = 4
N_EXPERTS = 8
TOP_K_EXPERTS = 2
N_ADA = 6
NORM_EPS = 1e-6
SUBLN_EPS = 1e-5
A_Q = A_HEADS * HEAD_DIM
A_KV = A_KV_HEADS * HEAD_DIM
I_Q = IDX_HEADS * IDX_DIM
B_QK = B_HEADS * 2 * HEAD_DIM
B_V = B_HEADS * 2 * HEAD_DIM

LANES = 128
VMEM_BYTES_V7X = 64 * 2 ** 20
INT_MIN = -2 ** 31
NEG = -1e30
NO_LIMIT = 2 ** 30
LOG2E = math.log2(math.e)
IN_PROJ_TN = 512
MASKED_KEY = 0x807FFFFF - 2 ** 32
WIDE_TILES = 4


def _cparams(dims, block_bytes):
    limit = min(max(2 * int(block_bytes) + (8 << 20), 32 << 20), VMEM_BYTES_V7X - (6 << 20))
    return pltpu.CompilerParams(dimension_semantics=dims, vmem_limit_bytes=limit)


def _nbytes(shape, dtype):
    return math.prod(shape) * jnp.dtype(dtype).itemsize


def _tile(n, pref):
    if n <= pref:
        return n
    t = pref
    while n % t:
        t //= 2
    return t


def _round_up(n, m):
    return -(-n // m) * m


def _z_layout(d_model):
    off, o = {}, 0
    for name, n in (('aq', A_Q), ('ak', A_KV), ('av', A_KV), ('iq', I_Q), ('bq', B_QK), ('bk', B_QK),
                    ('bv', B_V), ('ga', d_model), ('gb', d_model), ('ik', LANES), ('iw', LANES)):
        off[name] = o
        o += n
    return off, _round_up(o, IN_PROJ_TN)


def _prep_w_in(w, d_model):
    sizes = (A_Q, A_KV, A_KV, I_Q, IDX_DIM, IDX_HEADS, B_QK, B_QK, B_V, d_model, d_model)
    names = ('aq', 'ak', 'av', 'iq', 'ik', 'iw', 'bq', 'bk', 'bv', 'ga', 'gb')
    parts, o = {}, 0
    for n, s in zip(names, sizes):
        parts[n] = w[:, o:o + s]
        o += s
    cols = [parts[n] for n in ('aq', 'ak', 'av', 'iq', 'bq', 'bk', 'bv', 'ga', 'gb')]
    cols += [parts['ik'], parts['ik'], parts['iw']]
    used = sum(c.shape[1] for c in cols)
    cols.append(jnp.zeros((w.shape[0], _z_layout(d_model)[1] - used), w.dtype))
    return jnp.concatenate(cols, axis=1).astype(BF16)


def _ada_kernel(c_ref, w_ref, b_ref, o_ref):
    c = c_ref[...]
    s = c * jax.nn.sigmoid(c)
    o_ref[0] = jnp.dot(s.astype(BF16), w_ref[0].astype(BF16), preferred_element_type=F32) + b_ref[0]


def _ada_mod(c_all, w_ada, b_ada):
    depth, d, n = w_ada.shape
    nb = c_all.shape[0]
    tn = _tile(n, 1024)
    blocks = _nbytes((nb, d), F32) + _nbytes((d, tn), F32) * 2 + _nbytes((nb, tn), F32)
    return pl.pallas_call(
        _ada_kernel,
        grid=(depth, n // tn),
        in_specs=[pl.BlockSpec((nb, d), lambda l, j: (0, 0)),
                  pl.BlockSpec((1, d, tn), lambda l, j: (l, 0, j)),
                  pl.BlockSpec((1, 1, tn), lambda l, j: (l, 0, j))],
        out_specs=pl.BlockSpec((1, nb, tn), lambda l, j: (l, 0, j)),
        out_shape=jax.ShapeDtypeStruct((depth, nb, n), F32),
        compiler_params=_cparams(("arbitrary", "arbitrary"), blocks),
        name="ada_mod",
    )(c_all, w_ada, b_ada.reshape(depth, 1, n))


def _norm_kernel(*refs, modulated, router, eps):
    refs = list(refs)
    x_ref, g_ref = refs[:2]
    pos = 2
    x = x_ref[...]
    y = x * lax.rsqrt(jnp.mean(x * x, axis=-1, keepdims=True) + eps)
    y = y * g_ref[...]
    if modulated:
        scale_ref, shift_ref = refs[pos:pos + 2]
        pos += 2
        y = y * (1.0 + scale_ref[0]) + shift_ref[0]
    if router:
        wr_ref, br_ref = refs[pos:pos + 2]
        pos += 2
    o_ref = refs[pos]
    o_ref[...] = y.astype(o_ref.dtype)
    if router:
        idx_ref, gate_ref = refs[pos + 1:pos + 3]
        logits = jnp.dot(y, wr_ref[...], preferred_element_type=F32,
                         precision=lax.Precision.HIGHEST) + br_ref[...]
        lane = lax.broadcasted_iota(I32, logits.shape, 1)
        logits = jnp.where(lane < N_EXPERTS, logits, -jnp.inf)
        m1 = jnp.max(logits, axis=-1, keepdims=True)
        i1 = jnp.min(jnp.where(logits == m1, lane, LANES), axis=-1, keepdims=True)
        rest = jnp.where(lane == i1, -jnp.inf, logits)
        m2 = jnp.max(rest, axis=-1, keepdims=True)
        i2 = jnp.min(jnp.where(rest == m2, lane, LANES), axis=-1, keepdims=True)
        e = jnp.exp(m2 - m1)
        g1 = 1.0 / (1.0 + e)
        g2 = e / (1.0 + e)
        idx_ref[...] = jnp.where(lane == 0, i1, jnp.where(lane == 1, i2, 0))
        gate_ref[...] = jnp.where(lane == 0, g1, jnp.where(lane == 1, g2, 0.0))


def _norm(x2, g, seq, *, scale=None, shift=None, router=None, out_dtype, eps=NORM_EPS, name):
    t, d = x2.shape
    tm = _tile(seq, 256)
    per_b = seq // tm
    in_specs = [pl.BlockSpec((tm, d), lambda i: (i, 0)), pl.BlockSpec((1, d), lambda i: (0, 0))]
    args = [x2, g.reshape(1, d)]
    if scale is not None:
        in_specs += [pl.BlockSpec((1, 1, d), lambda i: (i // per_b, 0, 0))] * 2
        args += [scale, shift]
    out_specs = [pl.BlockSpec((tm, d), lambda i: (i, 0))]
    out_shape = [jax.ShapeDtypeStruct((t, d), out_dtype)]
    if router is not None:
        w_r, b_r = router
        in_specs += [pl.BlockSpec((d, LANES), lambda i: (0, 0)), pl.BlockSpec((1, LANES), lambda i: (0, 0))]
        args += [w_r, b_r]
        out_specs += [pl.BlockSpec((tm, LANES), lambda i: (i, 0))] * 2
        out_shape += [jax.ShapeDtypeStruct((t, LANES), I32), jax.ShapeDtypeStruct((t, LANES), F32)]
    blocks = 3 * _nbytes((tm, d), F32) + _nbytes((d, LANES), F32)
    outs = pl.pallas_call(
        functools.partial(_norm_kernel, modulated=scale is not None, router=router is not None, eps=eps),
        grid=(t // tm,), in_specs=in_specs, out_specs=out_specs, out_shape=out_shape,
        compiler_params=_cparams(("arbitrary",), blocks), name=name,
    )(*args)
    return outs if router is not None else outs[0]


def _mm_kernel(*refs, n_a, a_of_w, n_extra, epi, has_eids):
    refs = list(refs)
    if has_eids:
        refs = refs[1:]
    n_w = len(a_of_w)
    a_refs = refs[:n_a]
    w_refs = refs[n_a:n_a + n_w]
    e_refs = refs[n_a + n_w:n_a + n_w + n_extra]
    o_refs = refs[n_a + n_w + n_extra:]
    a_vals = [a[...].astype(BF16) for a in a_refs]
    accs = []
    for ai, w_ref in zip(a_of_w, w_refs):
        w = w_ref[0] if len(w_ref.shape) == 3 else w_ref[...]
        accs.append(jnp.dot(a_vals[ai], w, preferred_element_type=F32))
    outs = epi(accs, [e[...] for e in e_refs])
    for o_ref, o in zip(o_refs, outs):
        o_ref[...] = o.astype(o_ref.dtype)


def _matmul(a_list, w_list, a_of_w, extras, epi, out_dtypes, n_cols, *, tm, tn, eids=None, name):
    m = a_list[0].shape[0]
    in_specs, blocks = [], 0
    for a in a_list:
        in_specs.append(pl.BlockSpec((tm, a.shape[1]), lambda i, j, *_: (i, 0)))
        blocks += _nbytes((tm, a.shape[1]), a.dtype)
    for w in w_list:
        if w.ndim == 2:
            in_specs.append(pl.BlockSpec((w.shape[0], tn), lambda i, j, *_: (0, j)))
        else:
            in_specs.append(pl.BlockSpec((1, w.shape[1], tn), lambda i, j, e: (e[i], 0, j)))
        blocks += _nbytes((w.shape[-2], tn), w.dtype)
    for arr, bs, im in extras:
        in_specs.append(pl.BlockSpec(bs, im))
        blocks += _nbytes(bs, arr.dtype)
    out_specs = [pl.BlockSpec((tm, tn), lambda i, j, *_: (i, j)) for _ in out_dtypes]
    out_shape = [jax.ShapeDtypeStruct((m, n_cols), dt) for dt in out_dtypes]
    blocks += sum(_nbytes((tm, tn), dt) for dt in out_dtypes) + len(w_list) * _nbytes((tm, tn), F32)
    kern = functools.partial(_mm_kernel, n_a=len(a_list), a_of_w=tuple(a_of_w), n_extra=len(extras),
                             epi=epi, has_eids=eids is not None)
    grid_spec = pltpu.PrefetchScalarGridSpec(
        num_scalar_prefetch=0 if eids is None else 1, grid=(m // tm, n_cols // tn),
        in_specs=in_specs, out_specs=out_specs)
    args = ([] if eids is None else [eids]) + list(a_list) + list(w_list) + [e[0] for e in extras]
    outs = pl.pallas_call(kern, grid_spec=grid_spec, out_shape=out_shape,
                          compiler_params=_cparams(("arbitrary", "arbitrary"), blocks), name=name)(*args)
    return outs


def _nt_dot(a, b):
    return lax.dot_general(a, b, (((1,), (1,)), ((), ())), preferred_element_type=F32)


def _lane_blocks(x):
    return [x[:, c * LANES:(c + 1) * LANES] for c in range(x.shape[1] // LANES)]


def _softmax_step(lg, off, vt, carry):
    m, l, acc = carry
    rows, width = lg.shape
    block_max = functools.reduce(jnp.maximum, _lane_blocks(lg))
    row_max = jnp.broadcast_to(jnp.max(block_max, axis=-1, keepdims=True), (rows, LANES))
    m_new = jnp.maximum(m, row_max + off)
    alpha = jnp.exp2(m - m_new)
    p = jnp.exp2(lg - jnp.tile(m_new - off, (1, width // LANES)))
    l = alpha * l + functools.reduce(jnp.add, _lane_blocks(p))
    acc = (jnp.tile(alpha, (1, acc.shape[1] // LANES)) * acc
           + jnp.dot(p.astype(BF16), vt, preferred_element_type=F32))
    return m_new, l, acc


def _softmax_init(rows, width):
    return (jnp.full((rows, LANES), NEG, F32), jnp.zeros((rows, LANES), F32), jnp.zeros((rows, width), F32))


def _softmax_finish(carry):
    _, l, acc = carry
    return acc / jnp.sum(l, axis=-1, keepdims=True)


def _tile_loops(n_full, n_kv, past_step, edge_step, carry):
    n_wide = n_full // WIDE_TILES
    carry = lax.fori_loop(0, n_wide, lambda i, c: past_step(i * WIDE_TILES, WIDE_TILES, c), carry)
    carry = lax.fori_loop(n_wide * WIDE_TILES, n_full, lambda t, c: past_step(t, 1, c), carry)
    return lax.fori_loop(n_full, n_kv, edge_step, carry)


def _num_kv_tiles(q0, tq, tk, l_valid):
    kmax = jnp.minimum(l_valid, (((q0 + tq - 1) >> CHUNK_SHIFT) + 1) * CHUNK)
    return (kmax + tk - 1) // tk


def _dsa_kernel(qa_ref, iq_ref, iw_ref, k_ref, v_ref, ik_ref, o_ref, keys_ref, bias_ref, jlim_ref, *,
                tq, tk, n_past, l_valid, topk, index_bits):
    q0 = n_past + pl.program_id(1) * tq
    n_kv = _num_kv_tiles(q0, tq, tk, l_valid)
    row = lax.broadcasted_iota(I32, (tq, tk), 0)
    col = lax.broadcasted_iota(I32, (tq, tk), 1)
    qpos = q0 + row
    lane = lax.broadcasted_iota(I32, (tq, LANES), 1)

    iw = iw_ref[0] * (I_Q ** -0.5)
    iq = iq_ref[0].astype(F32)
    iq_heads = []
    for h in range(IDX_HEADS):
        pair = iq[:, (h // 2) * LANES:(h // 2 + 1) * LANES]
        keep = (lane < IDX_DIM) if h % 2 == 0 else (lane >= IDX_DIM)
        iq_heads.append(jnp.where(keep, pair, 0.0).astype(BF16))
    iq_all = jnp.concatenate(iq_heads, axis=0)

    def to_key(x):
        bits = lax.bitcast_convert_type(x, I32)
        return bits ^ ((bits >> 31) & 0x7FFFFFFF)

    def score_tiles(t, n, carry):
        top1, top2 = carry
        start = pl.multiple_of(t * tk, tk)
        ikt = ik_ref[0, pl.ds(start, n * tk), :]
        rel = jnp.maximum(_nt_dot(iq_all, ikt), 0.0)
        score = jnp.zeros((tq, n * tk), F32)
        for h in range(IDX_HEADS):
            score = score + iw[:, h:h + 1] * rel[h * tq:(h + 1) * tq]
        for i in range(n):
            kpos = start + i * tk + col
            adm = ((kpos >> CHUNK_SHIFT) <= (qpos >> CHUNK_SHIFT)) & (kpos < l_valid)
            masked = jnp.where(adm, score[:, i * tk:(i + 1) * tk], -jnp.inf)
            keys_ref[t + i] = to_key(masked)
            for blk in _lane_blocks(masked):
                top2 = jnp.maximum(top2, jnp.minimum(top1, blk))
                top1 = jnp.maximum(top1, blk)
        return top1, top2

    n_pair = n_kv // 2
    lows = jnp.full((tq, LANES), -jnp.inf, F32)
    tops = lax.fori_loop(0, n_pair, lambda i, c: score_tiles(2 * i, 2, c), (lows, lows))
    top1, top2 = lax.fori_loop(2 * n_pair, n_kv, lambda t, c: score_tiles(t, 1, c), tops)

    def count(pred):
        def body(t, acc):
            x = jnp.where(pred(keys_ref[t], t), 1.0, 0.0)
            part = x[:, 0:LANES]
            for c in range(1, tk // LANES):
                part = part + x[:, c * LANES:(c + 1) * LANES]
            return acc + part
        acc = lax.fori_loop(0, n_kv, body, jnp.zeros((tq, LANES), F32))
        return jnp.sum(acc, axis=-1, keepdims=True)

    def value_bit(it, res):
        cand = res | lax.shift_left(jnp.int32(1), 31 - it)
        cand_key = cand ^ INT_MIN
        cnt = count(lambda k, t: k >= cand_key)
        return jnp.where(cnt >= topk, cand, res)

    lo_key = to_key(jnp.min(top2, axis=-1, keepdims=True))
    hi_key = to_key(jnp.max(top1, axis=-1, keepdims=True))
    shared = jnp.min(lax.clz(lo_key ^ hi_key).astype(F32)).astype(I32)
    first_bit = jnp.minimum(shared, 31)
    settled = ~lax.shift_right_logical(jnp.int32(-1), first_bit)
    thr = lax.fori_loop(first_bit, 32, value_bit, (lo_key ^ INT_MIN) & settled) ^ INT_MIN
    n_gt = count(lambda k, t: k > thr)
    n_ge = count(lambda k, t: k >= thr)
    want_ties = topk - n_gt
    need = ((n_ge - n_gt) > want_ties) & (thr > MASKED_KEY)
    jlim_ref[...] = jnp.full((tq, LANES), NO_LIMIT, I32)

    @pl.when(jnp.max(jnp.where(need, 1.0, 0.0)) > 0.0)
    def _():
        def index_bit(it, j_lim):
            cand = j_lim | lax.shift_left(jnp.int32(1), index_bits - 1 - it)
            below = count(lambda k, t: (k == thr) & ((t * tk + col) < cand))
            return jnp.where(below <= want_ties - 1.0, cand, j_lim)
        j_lim = lax.fori_loop(0, index_bits, index_bit, jnp.zeros((tq, 1), I32))
        jlim_ref[...] = jnp.broadcast_to(jnp.where(need, j_lim, NO_LIMIT), (tq, LANES))

    j_lim = jlim_ref[:, 0:1]

    def bias_tile(t, carry):
        k = keys_ref[t]
        kpos = t * tk + col
        sel = (k > thr) | ((k == thr) & (kpos <= j_lim))
        sel = sel & (k > MASKED_KEY)
        bias_ref[t] = jnp.where(sel, 0.0, NEG)
        return carry

    lax.fori_loop(0, n_kv, bias_tile, 0)

    qk_scale = HEAD_DIM ** -0.5 * LOG2E
    n_full = jnp.minimum(q0, l_valid) // tk
    rel_pos = {n: (lax.broadcasted_iota(I32, (tq, n * tk), 1) - lax.broadcasted_iota(I32, (tq, n * tk), 0)
                   ).astype(F32) for n in (1, WIDE_TILES)}
    for j in range(A_KV_HEADS):
        q4 = jnp.concatenate([qa_ref[0, :, (j * A_GROUP + g) * HEAD_DIM:(j * A_GROUP + g + 1) * HEAD_DIM]
                              for g in range(A_GROUP)], axis=0)
        slopes2 = [2.0 ** -(j * A_GROUP + g + 1) * LOG2E for g in range(A_GROUP)]
        slope_rows = jnp.concatenate([jnp.full((tq, LANES), s2, F32) for s2 in slopes2], axis=0)

        def tile_operands(t, n, j=j, q4=q4):
            start = pl.multiple_of(t * tk, tk)
            kt = k_ref[0, pl.ds(start, n * tk), j * HEAD_DIM:(j + 1) * HEAD_DIM]
            vt = v_ref[0, pl.ds(start, n * tk), j * HEAD_DIM:(j + 1) * HEAD_DIM]
            bias = bias_ref[t] if n == 1 else jnp.concatenate([bias_ref[t + i] for i in range(n)], axis=1)
            return start, _nt_dot(q4, kt), vt, bias

        def past_step(t, n, carry, slopes2=slopes2, slope_rows=slope_rows):
            start, s, vt, bias = tile_operands(t, n)
            lg = jnp.concatenate([s[g * tq:(g + 1) * tq] * qk_scale + (rel_pos[n] * slopes2[g] + bias)
                                  for g in range(A_GROUP)], axis=0)
            off = slope_rows * (start - q0).astype(F32)
            return _softmax_step(lg, off, vt, carry)

        def edge_step(t, carry, slopes2=slopes2):
            start, s, vt, bias = tile_operands(t, 1)
            dist = jnp.abs(qpos - (start + col)).astype(F32)
            lg = jnp.concatenate([s[g * tq:(g + 1) * tq] * qk_scale - slopes2[g] * dist + bias
                                  for g in range(A_GROUP)], axis=0)
            return _softmax_step(lg, 0.0, vt, carry)

        out = _softmax_finish(_tile_loops(n_full, n_kv, past_step, edge_step,
                                          _softmax_init(A_GROUP * tq, HEAD_DIM)))
        for g in range(A_GROUP):
            h = j * A_GROUP + g
            o_ref[0, :, h * HEAD_DIM:(h + 1) * HEAD_DIM] = out[g * tq:(g + 1) * tq].astype(o_ref.dtype)


def _dsa_attention(zb, zf, keys, off, *, n_past, l_valid, tq, tk, topk, name):
    b, s, _ = zb.shape
    karr, kcol, varr, vcol, ikarr, ikcol = keys
    lp = karr.shape[1]
    n_tiles = lp // tk
    kern = functools.partial(_dsa_kernel, tq=tq, tk=tk, n_past=n_past, l_valid=l_valid, topk=topk,
                             index_bits=lp.bit_length())
    blocks = (_nbytes((tq, A_Q + I_Q), BF16) + _nbytes((tq, LANES), F32) + 2 * _nbytes((lp, A_KV), BF16)
              + _nbytes((lp, LANES), BF16) + _nbytes((tq, A_Q), BF16) + _nbytes((tq, lp), F32)
              + 8 * _nbytes((A_GROUP * tq, tk), F32))
    return pl.pallas_call(
        kern,
        grid=(b, s // tq),
        in_specs=[pl.BlockSpec((1, tq, A_Q), lambda bi, qi: (bi, qi, off['aq'] // A_Q)),
                  pl.BlockSpec((1, tq, I_Q), lambda bi, qi: (bi, qi, off['iq'] // I_Q)),
                  pl.BlockSpec((1, tq, LANES), lambda bi, qi: (bi, qi, off['iw'] // LANES)),
                  pl.BlockSpec((1, lp, A_KV), lambda bi, qi: (bi, 0, kcol)),
                  pl.BlockSpec((1, lp, A_KV), lambda bi, qi: (bi, 0, vcol)),
                  pl.BlockSpec((1, lp, LANES), lambda bi, qi: (bi, 0, ikcol))],
        out_specs=pl.BlockSpec((1, tq, A_Q), lambda bi, qi: (bi, qi, 0)),
        out_shape=jax.ShapeDtypeStruct((b, s, A_Q), BF16),
        scratch_shapes=[pltpu.VMEM((n_tiles, tq, tk), I32), pltpu.VMEM((n_tiles, tq, tk), F32),
                        pltpu.VMEM((tq, LANES), I32)],
        compiler_params=_cparams(("arbitrary", "arbitrary"), blocks),
        name=name,
    )(zb, zb, zf, karr, varr, ikarr)


def _diff_kernel(q_ref, k_ref, v_ref, lq_ref, g_ref, o_ref, *, tq, tk, n_past, l_valid, lam_init):
    h = pl.program_id(1)
    q0 = n_past + pl.program_id(2) * tq
    n_kv = _num_kv_tiles(q0, tq, tk, l_valid)
    row = lax.broadcasted_iota(I32, (tq, tk), 0)
    col = lax.broadcasted_iota(I32, (tq, tk), 1)
    qpos = q0 + row
    slope2 = jnp.float32(1.0)
    for hh in range(B_HEADS):
        slope2 = jnp.where(h == hh, jnp.float32(2.0 ** (-8.0 * (hh + 1) / B_HEADS) * LOG2E), slope2)
    qk_scale = HEAD_DIM ** -0.5 * LOG2E
    n_full = jnp.minimum(q0, l_valid) // tk
    rel_bias = {n: (lax.broadcasted_iota(I32, (tq, n * tk), 1) - lax.broadcasted_iota(I32, (tq, n * tk), 0)
                    ).astype(F32) * slope2 for n in (1, WIDE_TILES)}
    qc = [q_ref[0, :, c * HEAD_DIM:(c + 1) * HEAD_DIM] for c in range(2)]

    def tile_operands(t, n):
        start = pl.multiple_of(t * tk, tk)
        s = [_nt_dot(qc[c], k_ref[0, pl.ds(start, n * tk), c * HEAD_DIM:(c + 1) * HEAD_DIM]) for c in range(2)]
        return start, s, v_ref[0, pl.ds(start, n * tk), :]

    def past_step(t, n, carry):
        start, s, vt = tile_operands(t, n)
        lg = jnp.concatenate([s[c] * qk_scale + rel_bias[n] for c in range(2)], axis=0)
        return _softmax_step(lg, slope2 * (start - q0).astype(F32), vt, carry)

    def edge_step(t, carry):
        start, s, vt = tile_operands(t, 1)
        kpos = start + col
        adm = ((kpos >> CHUNK_SHIFT) <= (qpos >> CHUNK_SHIFT)) & (kpos < l_valid)
        bias = jnp.where(adm, -slope2 * jnp.abs(qpos - kpos).astype(F32), NEG)
        lg = jnp.concatenate([s[c] * qk_scale + bias for c in range(2)], axis=0)
        return _softmax_step(lg, 0.0, vt, carry)

    out = _softmax_finish(_tile_loops(n_full, n_kv, past_step, edge_step,
                                      _softmax_init(2 * tq, 2 * HEAD_DIM)))
    lq = lq_ref[...]
    lam = (jnp.exp(jnp.sum(lq[0:1] * lq[1:2], axis=-1, keepdims=True))
           - jnp.exp(jnp.sum(lq[2:3] * lq[3:4], axis=-1, keepdims=True)) + lam_init)
    o = out[:tq] - lam * out[tq:]
    y = o * lax.rsqrt(jnp.mean(o * o, axis=-1, keepdims=True) + SUBLN_EPS)
    y = (y * g_ref[...]) * (1.0 - lam_init)
    o_ref[0] = y.astype(o_ref.dtype)


def _diff_attention(zb, keys, lam_qk_l, g_subln_l, off, *, n_past, l_valid, tq, tk, lam_init, name):
    b, s, _ = zb.shape
    karr, kcol0, varr, vcol0 = keys
    lp = karr.shape[1]
    w = 2 * HEAD_DIM
    kern = functools.partial(_diff_kernel, tq=tq, tk=tk, n_past=n_past, l_valid=l_valid, lam_init=lam_init)
    blocks = (2 * _nbytes((tq, w), BF16) + 2 * _nbytes((lp, w), BF16) + 8 * _nbytes((2 * tq, tk), F32))
    return pl.pallas_call(
        kern,
        grid=(b, B_HEADS, s // tq),
        in_specs=[pl.BlockSpec((1, tq, w), lambda bi, h, qi: (bi, qi, off['bq'] // w + h)),
                  pl.BlockSpec((1, lp, w), lambda bi, h, qi: (bi, 0, kcol0 + h)),
                  pl.BlockSpec((1, lp, w), lambda bi, h, qi: (bi, 0, vcol0 + h)),
                  pl.BlockSpec((4, HEAD_DIM), lambda bi, h, qi: (0, 0)),
                  pl.BlockSpec((1, w), lambda bi, h, qi: (0, 0))],
        out_specs=pl.BlockSpec((1, tq, w), lambda bi, h, qi: (bi, qi, h)),
        out_shape=jax.ShapeDtypeStruct((b, s, B_V), BF16),
        compiler_params=_cparams(("arbitrary", "arbitrary", "arbitrary"), blocks),
        name=name,
    )(zb, karr, varr, lam_qk_l, g_subln_l.reshape(1, w))


def _gather_kernel(idx_ref, src_ref, o_ref, sem, *, rows):
    base = pl.program_id(0) * rows

    def row_copy(r):
        return pltpu.make_async_copy(src_ref.at[pl.ds(idx_ref[base + r], 1)], o_ref.at[pl.ds(r, 1)], sem)

    def start(r, carry):
        row_copy(r).start()
        return carry

    def wait(r, carry):
        row_copy(r).wait()
        return carry

    lax.fori_loop(0, rows, start, 0)
    lax.fori_loop(0, rows, wait, 0)


def _gather_rows(src, idx, *, name):
    n = idx.shape[0]
    d = src.shape[1]
    rows = _tile(n, 128)
    return pl.pallas_call(
        functools.partial(_gather_kernel, rows=rows),
        grid_spec=pltpu.PrefetchScalarGridSpec(
            num_scalar_prefetch=1, grid=(n // rows,),
            in_specs=[pl.BlockSpec(memory_space=pl.ANY)],
            out_specs=pl.BlockSpec((rows, d), lambda i, idx_ref: (i, 0)),
            scratch_shapes=[pltpu.SemaphoreType.DMA(())]),
        out_shape=jax.ShapeDtypeStruct((n, d), src.dtype),
        compiler_params=_cparams(("arbitrary",), _nbytes((rows, d), src.dtype)),
        name=name,
    )(idx, src)


def _combine_kernel(x_ref, y0_ref, y1_ref, rg_ref, gate_ref, o_ref):
    rg = rg_ref[...]
    f = rg[:, 0:1] * y0_ref[...] + rg[:, 1:2] * y1_ref[...]
    o_ref[...] = x_ref[...] + gate_ref[0] * f


def _moe_combine(x2, yg2, rgate, gate_f, seq, *, name):
    t, d = x2.shape
    tm = _tile(seq, 256)
    per_b = seq // tm
    return pl.pallas_call(
        _combine_kernel,
        grid=(t // tm,),
        in_specs=[pl.BlockSpec((tm, d), lambda i: (i, 0)),
                  pl.BlockSpec((tm, d), lambda i: (i, 0)),
                  pl.BlockSpec((tm, d), lambda i: (i, 1)),
                  pl.BlockSpec((tm, LANES), lambda i: (i, 0)),
                  pl.BlockSpec((1, 1, d), lambda i: (i // per_b, 0, 0))],
        out_specs=pl.BlockSpec((tm, d), lambda i: (i, 0)),
        out_shape=jax.ShapeDtypeStruct((t, d), F32),
        compiler_params=_cparams(("arbitrary",), 4 * _nbytes((tm, d), F32)),
        name=name,
    )(x2, yg2, yg2, rgate, gate_f)


def _silu_mul(accs, extras):
    u, v = accs
    return [(u * jax.nn.sigmoid(u)) * v]


def _moe_ffn(x2, h2, ridx, rgate, gate_f, seq, w_gate, w_up, w_down, *, tag):
    t, d = x2.shape
    fp = w_gate.shape[-1]
    a = t * TOP_K_EXPERTS
    blk = _tile(a, 512)
    n_blocks = -(-(a + N_EXPERTS * (blk - 1)) // blk)
    p = n_blocks * blk
    e_flat = ridx[:, :TOP_K_EXPERTS].reshape(-1)
    onehot = (e_flat[:, None] == jnp.arange(N_EXPERTS, dtype=I32)[None, :]).astype(I32)
    rank = jnp.sum((jnp.cumsum(onehot, axis=0) - onehot) * onehot, axis=1)
    counts = jnp.sum(onehot, axis=0)
    padded = ((counts + blk - 1) // blk) * blk
    pends = jnp.cumsum(padded)
    pstarts = pends - padded
    dest = (pstarts[e_flat] + rank).astype(I32)
    tok_buf = jnp.zeros((p,), I32).at[dest].set(jnp.arange(a, dtype=I32) // TOP_K_EXPERTS)
    block_expert = jnp.clip(jnp.searchsorted(pends, jnp.arange(n_blocks, dtype=I32) * blk, side='right'),
                            0, N_EXPERTS - 1).astype(I32)

    xb = _gather_rows(h2, tok_buf, name=f"moe_gather_{tag}")
    tn = _tile(fp, 512)
    (act,) = _matmul([xb], [w_gate, w_up], [0, 0], [], _silu_mul, [BF16], fp, tm=blk, tn=tn,
                     eids=block_expert, name=f"moe_up_{tag}")
    (yb,) = _matmul([act], [w_down], [0], [], lambda accs, ex: accs, [F32], d, tm=blk, tn=_tile(d, 512),
                    eids=block_expert, name=f"moe_down_{tag}")
    yg = _gather_rows(yb, dest, name=f"moe_ungather_{tag}")
    return _moe_combine(x2, yg.reshape(t, TOP_K_EXPERTS * d), rgate, gate_f, seq, name=f"moe_combine_{tag}")


def _trunk(x, mods, past_all, wts, tag):
    b, s, d = x.shape
    t = b * s
    depth = len(wts['w_in'])
    off, nz = _z_layout(d)
    x2 = x.reshape(t, d)
    tn_d = _tile(d, 512)
    tm_b = _tile(s, 512)
    tm_f = _tile(t, 1024)
    per_b = s // tm_b
    rows = []
    for l in range(depth):
        shift_a, scale_a, gate_a, shift_f, scale_f, gate_f = [
            mods[l][:, i * d:(i + 1) * d].reshape(b, 1, d) for i in range(N_ADA)]
        batch_tile = lambda i, j, *_: (i // per_b, 0, j)

        h = _norm(x2, wts['g_attn'][l], s, scale=scale_a, shift=shift_a, out_dtype=BF16,
                  name=f"norm_attn_{tag}{l}")
        zf, zb = _matmul([h], [wts['w_in'][l]], [0], [], lambda accs, ex: [accs[0], accs[0]], [F32, BF16], nz,
                         tm=tm_f, tn=IN_PROJ_TN, name=f"in_proj_{tag}{l}")
        zf3, zb3 = zf.reshape(b, s, nz), zb.reshape(b, s, nz)
        cut = lambda z, name, width: z[:, :, off[name]:off[name] + width]
        rows.append((cut(zf3, 'ak', A_KV), cut(zf3, 'av', A_KV), cut(zf3, 'ik', IDX_DIM),
                     cut(zf3, 'bk', B_QK), cut(zf3, 'bv', B_V)))

        if past_all is None:
            n_past, l_valid = 0, s
            tk = _tile(s, 512)
            tq_a, tq_b = _tile(s, 128), _tile(s, 256)
            dsa_keys = (zb3, off['ak'] // A_KV, zb3, off['av'] // A_KV, zb3, off['ik'] // LANES)
            diff_keys = (zb3, off['bk'] // (2 * HEAD_DIM), zb3, off['bv'] // (2 * HEAD_DIM))
        else:
            n_past = past_all[0].shape[2]
            l_valid = n_past + s
            tk = 512
            lp = _round_up(l_valid, tk)
            tq_a = tq_b = s

            def with_past(cached, name, width):
                return jnp.concatenate([cached.astype(BF16), cut(zb3, name, width),
                                        jnp.zeros((b, lp - l_valid, width), BF16)], axis=1)

            past = [c[l].reshape(b, n_past, -1) for c in past_all]
            dsa_keys = (with_past(past[0], 'ak', A_KV), 0, with_past(past[1], 'av', A_KV), 0,
                        with_past(jnp.concatenate([past[2], past[2]], axis=-1), 'ik', LANES), 0)
            diff_keys = (with_past(past[3], 'bk', B_QK), 0, with_past(past[4], 'bv', B_V), 0)
        topk = min(TOPK_MAX, l_valid // 4)
        oa = _dsa_attention(zb3, zf3, dsa_keys, off, n_past=n_past, l_valid=l_valid, tq=tq_a, tk=tk,
                            topk=topk, name=f"dsa_{tag}{l}")
        lam_init = 0.8 - 0.6 * math.exp(-0.3 * l)
        ob = _diff_attention(zb3, diff_keys, wts['lam_qk'][l], wts['g_subln'][l], off, n_past=n_past,
                             l_valid=l_valid, tq=tq_b, tk=tk, lam_init=lam_init, name=f"diff_{tag}{l}")

        def merge_epi(accs, ex):
            return [jax.nn.sigmoid(ex[0]) * accs[0] + jax.nn.sigmoid(ex[1]) * accs[1]]

        (merged,) = _matmul(
            [oa.reshape(t, A_Q), ob.reshape(t, B_V)], [wts['w_out_a'][l], wts['w_out_b'][l]], [0, 1],
            [(zf, (tm_f, tn_d), lambda i, j, *_: (i, off['ga'] // tn_d + j)),
             (zf, (tm_f, tn_d), lambda i, j, *_: (i, off['gb'] // tn_d + j))],
            merge_epi, [BF16], d, tm=tm_f, tn=tn_d, name=f"merge_{tag}{l}")

        def resid_epi(accs, ex):
            return [ex[0] + ex[1][0] * accs[0]]

        (x2,) = _matmul([merged], [wts['w_out'][l]], [0],
                        [(x2, (tm_b, tn_d), lambda i, j, *_: (i, j)), (gate_a, (1, 1, tn_d), batch_tile)],
                        resid_epi, [F32], d, tm=tm_b, tn=tn_d, name=f"out_proj_{tag}{l}")

        if l % 2 == 0:
            i = l // 2
            h = _norm(x2, wts['g_ffn'][l], s, scale=scale_f, shift=shift_f, out_dtype=BF16,
                      name=f"norm_ffn_{tag}{l}")
            fp = wts['w_ff_gate'][i].shape[-1]
            (act,) = _matmul([h], [wts['w_ff_gate'][i], wts['w_ff_up'][i]], [0, 0], [], _silu_mul, [BF16], fp,
                             tm=tm_f, tn=_tile(fp, 512), name=f"ffn_up_{tag}{l}")
            (x2,) = _matmul([act], [wts['w_ff_down'][i]], [0],
                            [(x2, (tm_b, tn_d), lambda i, j, *_: (i, j)), (gate_f, (1, 1, tn_d), batch_tile)],
                            resid_epi, [F32], d, tm=tm_b, tn=tn_d, name=f"ffn_down_{tag}{l}")
        else:
            i = l // 2
            h2, ridx, rgate = _norm(x2, wts['g_ffn'][l], s, scale=scale_f, shift=shift_f,
                                    router=(wts['w_router'][i], wts['b_router'][i]), out_dtype=F32,
                                    name=f"norm_router_{tag}{l}")
            x2 = _moe_ffn(x2, h2, ridx, rgate, gate_f, s, wts['w_moe_gate'][i], wts['w_moe_up'][i],
                          wts['w_moe_down'][i], tag=f"{tag}{l}")
    y = _norm(x2, wts['g_final'], s, out_dtype=F32, name=f"norm_final_{tag}")
    tails = ((A_KV_HEADS, HEAD_DIM), (A_KV_HEADS, HEAD_DIM), (IDX_DIM,), (B_HEADS, 2, HEAD_DIM),
             (B_HEADS, 2 * HEAD_DIM))
    state = tuple(jnp.stack([r[j] for r in rows]).reshape((depth, b, s) + tails[j]) for j in range(5))
    return y.reshape(b, s, d), state


def _prep_weights(w_in, w_out_a, w_out_b, w_out, w_ff_gate, w_ff_up, w_ff_down, w_router, b_router,
                  w_moe_gate, w_moe_up, w_moe_down, d_model):
    f = w_ff_gate.shape[-1]
    fp = _round_up(f, 512)
    pad_cols = lambda w: jnp.pad(w.astype(BF16), [(0, 0)] * (w.ndim - 1) + [(0, fp - f)])
    pad_rows = lambda w: jnp.pad(w.astype(BF16), [(0, 0)] * (w.ndim - 2) + [(0, fp - f), (0, 0)])
    n_moe = w_router.shape[0]
    return dict(
        w_in=[_prep_w_in(w_in[l], d_model) for l in range(w_in.shape[0])],
        w_out_a=w_out_a.astype(BF16), w_out_b=w_out_b.astype(BF16), w_out=w_out.astype(BF16),
        w_ff_gate=pad_cols(w_ff_gate), w_ff_up=pad_cols(w_ff_up), w_ff_down=pad_rows(w_ff_down),
        w_router=[jnp.pad(w_router[i], ((0, 0), (0, LANES - N_EXPERTS))) for i in range(n_moe)],
        b_router=[jnp.pad(b_router[i], (0, LANES - N_EXPERTS)).reshape(1, LANES) for i in range(n_moe)],
        w_moe_gate=pad_cols(w_moe_gate), w_moe_up=pad_cols(w_moe_up), w_moe_down=pad_rows(w_moe_down),
    )


def kernel(x_prompt, x_sample, c_prompt, c_sample, cache_dsa_k, cache_dsa_v, cache_idx_k, cache_diff_k, cache_diff_v, w_ada, b_ada, g_attn, w_in, w_out_a, w_out_b, w_out, lam_qk, g_subln, g_ffn, w_ff_gate, w_ff_up, w_ff_down, w_router, b_router, w_moe_gate, w_moe_up, w_moe_down, g_final):
    d = x_prompt.shape[-1]
    wts = _prep_weights(w_in, w_out_a, w_out_b, w_out, w_ff_gate, w_ff_up, w_ff_down, w_router, b_router,
                        w_moe_gate, w_moe_up, w_moe_down, d)
    wts.update(g_attn=g_attn, g_ffn=g_ffn, g_final=g_final, lam_qk=lam_qk, g_subln=g_subln)
    nb_p = c_prompt.shape[0]
    mods = _ada_mod(jnp.concatenate([c_prompt, c_sample], axis=0), w_ada, b_ada)
    y_p, st_p = _trunk(x_prompt, mods[:, :nb_p], None, wts, "p")
    past_all = (cache_dsa_k, cache_dsa_v, cache_idx_k, cache_diff_k, cache_diff_v)
    y_s, st_s = _trunk(x_sample, mods[:, nb_p:], past_all, wts, "s")
    return (y_p, y_s) + st_p + st_s
```

```python
import functools
import itertools
import math

import jax
import jax.numpy as jnp
from jax import lax
from jax.experimental import pallas as pl
from jax.experimental.pallas import tpu as pltpu

F32, BF16, I32 = jnp.float32, jnp.bfloat16, jnp.int32

CHUNK = 64
CHUNK_SHIFT = CHUNK.bit_length() - 1
HEAD_DIM = 128
A_HEADS = 8
A_KV_HEADS = 2
A_GROUP = A_HEADS // A_KV_HEADS
IDX_HEADS = 8
IDX_DIM = 64
TOPK_MAX = 256
B_HEADS = 4
N_EXPERTS = 8
TOP_K_EXPERTS = 2
N_ADA = 6
NORM_EPS = 1e-6
SUBLN_EPS = 1e-5
A_Q = A_HEADS * HEAD_DIM
A_KV = A_KV_HEADS * HEAD_DIM
I_Q = IDX_HEADS * IDX_DIM
B_QK = B_HEADS * 2 * HEAD_DIM
B_V = B_HEADS * 2 * HEAD_DIM

LANES = 128
VMEM_BYTES_V7X = 64 * 2 ** 20
INT_MIN = -2 ** 31
NEG = -1e30
NO_LIMIT = 2 ** 30
LOG2E = math.log2(math.e)
IN_PROJ_TN = 512
MASKED_KEY = 0x807FFFFF - 2 ** 32
WIDE_TILES = 4


def _cparams(dims, block_bytes):
    limit = min(max(2 * int(block_bytes) + (8 << 20), 32 << 20), VMEM_BYTES_V7X - (6 << 20))
    return pltpu.CompilerParams(dimension_semantics=dims, vmem_limit_bytes=limit)


def _nbytes(shape, dtype):
    return math.prod(shape) * jnp.dtype(dtype).itemsize


def _tile(n, pref):
    if n <= pref:
        return n
    t = pref
    while n % t:
        t //= 2
    return t


def _round_up(n, m):
    return -(-n // m) * m


def _z_layout(d_model):
    off, o = {}, 0
    for name, n in (('aq', A_Q), ('ak', A_KV), ('av', A_KV), ('iq', I_Q), ('bq', B_QK), ('bk', B_QK),
                    ('bv', B_V), ('ga', d_model), ('gb', d_model), ('ik', LANES), ('iw', LANES)):
        off[name] = o
        o += n
    return off, _round_up(o, IN_PROJ_TN)


def _prep_w_in(w, d_model):
    sizes = (A_Q, A_KV, A_KV, I_Q, IDX_DIM, IDX_HEADS, B_QK, B_QK, B_V, d_model, d_model)
    names = ('aq', 'ak', 'av', 'iq', 'ik', 'iw', 'bq', 'bk', 'bv', 'ga', 'gb')
    parts, o = {}, 0
    for n, s in zip(names, sizes):
        parts[n] = w[:, o:o + s]
        o += s
    cols = [parts[n] for n in ('aq', 'ak', 'av', 'iq', 'bq', 'bk', 'bv', 'ga', 'gb')]
    cols += [parts['ik'], parts['ik'], parts['iw']]
    used = sum(c.shape[1] for c in cols)
    cols.append(jnp.zeros((w.shape[0], _z_layout(d_model)[1] - used), w.dtype))
    return jnp.concatenate(cols, axis=1).astype(BF16)


def _ada_kernel(c_ref, w_ref, b_ref, o_ref):
    c = c_ref[...]
    s = c * jax.nn.sigmoid(c)
    o_ref[0] = jnp.dot(s.astype(BF16), w_ref[0].astype(BF16), preferred_element_type=F32) + b_ref[0]


def _ada_mod(c_all, w_ada, b_ada):
    depth, d, n = w_ada.shape
    nb = c_all.shape[0]
    tn = _tile(n, 1024)
    blocks = _nbytes((nb, d), F32) + _nbytes((d, tn), F32) * 2 + _nbytes((nb, tn), F32)
    return pl.pallas_call(
        _ada_kernel,
        grid=(depth, n // tn),
        in_specs=[pl.BlockSpec((nb, d), lambda l, j: (0, 0)),
                  pl.BlockSpec((1, d, tn), lambda l, j: (l, 0, j)),
                  pl.BlockSpec((1, 1, tn), lambda l, j: (l, 0, j))],
        out_specs=pl.BlockSpec((1, nb, tn), lambda l, j: (l, 0, j)),
        out_shape=jax.ShapeDtypeStruct((depth, nb, n), F32),
        compiler_params=_cparams(("arbitrary", "arbitrary"), blocks),
        name="ada_mod",
    )(c_all, w_ada, b_ada.reshape(depth, 1, n))


def _norm_kernel(*refs, modulated, router, eps):
    refs = list(refs)
    x_ref, g_ref = refs[:2]
    pos = 2
    x = x_ref[...]
    y = x * lax.rsqrt(jnp.mean(x * x, axis=-1, keepdims=True) + eps)
    y = y * g_ref[...]
    if modulated:
        scale_ref, shift_ref = refs[pos:pos + 2]
        pos += 2
        y = y * (1.0 + scale_ref[0]) + shift_ref[0]
    if router:
        wr_ref, br_ref = refs[pos:pos + 2]
        pos += 2
    o_ref = refs[pos]
    o_ref[...] = y.astype(o_ref.dtype)
    if router:
        idx_ref, gate_ref = refs[pos + 1:pos + 3]
        logits = jnp.dot(y, wr_ref[...], preferred_element_type=F32,
                         precision=lax.Precision.HIGHEST) + br_ref[...]
        lane = lax.broadcasted_iota(I32, logits.shape, 1)
        logits = jnp.where(lane < N_EXPERTS, logits, -jnp.inf)
        m1 = jnp.max(logits, axis=-1, keepdims=True)
        i1 = jnp.min(jnp.where(logits == m1, lane, LANES), axis=-1, keepdims=True)
        rest = jnp.where(lane == i1, -jnp.inf, logits)
        m2 = jnp.max(rest, axis=-1, keepdims=True)
        i2 = jnp.min(jnp.where(rest == m2, lane, LANES), axis=-1, keepdims=True)
        e = jnp.exp(m2 - m1)
        g1 = 1.0 / (1.0 + e)
        g2 = e / (1.0 + e)
        idx_ref[...] = jnp.where(lane == 0, i1, jnp.where(lane == 1, i2, 0))
        gate_ref[...] = jnp.where(lane == 0, g1, jnp.where(lane == 1, g2, 0.0))


def _norm(x2, g, seq, *, scale=None, shift=None, router=None, out_dtype, eps=NORM_EPS, name):
    t, d = x2.shape
    tm = _tile(seq, 256)
    per_b = seq // tm
    in_specs = [pl.BlockSpec((tm, d), lambda i: (i, 0)), pl.BlockSpec((1, d), lambda i: (0, 0))]
    args = [x2, g.reshape(1, d)]
    if scale is not None:
        in_specs += [pl.BlockSpec((1, 1, d), lambda i: (i // per_b, 0, 0))] * 2
        args += [scale, shift]
    out_specs = [pl.BlockSpec((tm, d), lambda i: (i, 0))]
    out_shape = [jax.ShapeDtypeStruct((t, d), out_dtype)]
    if router is not None:
        w_r, b_r = router
        in_specs += [pl.BlockSpec((d, LANES), lambda i: (0, 0)), pl.BlockSpec((1, LANES), lambda i: (0, 0))]
        args += [w_r, b_r]
        out_specs += [pl.BlockSpec((tm, LANES), lambda i: (i, 0))] * 2
        out_shape += [jax.ShapeDtypeStruct((t, LANES), I32), jax.ShapeDtypeStruct((t, LANES), F32)]
    blocks = 3 * _nbytes((tm, d), F32) + _nbytes((d, LANES), F32)
    outs = pl.pallas_call(
        functools.partial(_norm_kernel, modulated=scale is not None, router=router is not None, eps=eps),
        grid=(t // tm,), in_specs=in_specs, out_specs=out_specs, out_shape=out_shape,
        compiler_params=_cparams(("arbitrary",), blocks), name=name,
    )(*args)
    return outs if router is not None else outs[0]


def _mm_kernel(*refs, n_a, a_of_w, n_extra, epi, has_eids):
    refs = list(refs)
    if has_eids:
        refs = refs[1:]
    n_w = len(a_of_w)
    a_refs = refs[:n_a]
    w_refs = refs[n_a:n_a + n_w]
    e_refs = refs[n_a + n_w:n_a + n_w + n_extra]
    o_refs = refs[n_a + n_w + n_extra:]
    a_vals = [a[...].astype(BF16) for a in a_refs]
    accs = []
    for ai, w_ref in zip(a_of_w, w_refs):
        w = w_ref[0] if len(w_ref.shape) == 3 else w_ref[...]
        accs.append(jnp.dot(a_vals[ai], w, preferred_element_type=F32))
    outs = epi(accs, [e[...] for e in e_refs])
    for o_ref, o in zip(o_refs, outs):
        o_ref[...] = o.astype(o_ref.dtype)


def _matmul(a_list, w_list, a_of_w, extras, epi, out_dtypes, n_cols, *, tm, tn, eids=None, name):
    m = a_list[0].shape[0]
    in_specs, blocks = [], 0
    for a in a_list:
        in_specs.append(pl.BlockSpec((tm, a.shape[1]), lambda i, j, *_: (i, 0)))
        blocks += _nbytes((tm, a.shape[1]), a.dtype)
    for w in w_list:
        if w.ndim == 2:
            in_specs.append(pl.BlockSpec((w.shape[0], tn), lambda i, j, *_: (0, j)))
        else:
            in_specs.append(pl.BlockSpec((1, w.shape[1], tn), lambda i, j, e: (e[i], 0, j)))
        blocks += _nbytes((w.shape[-2], tn), w.dtype)
    for arr, bs, im in extras:
        in_specs.append(pl.BlockSpec(bs, im))
        blocks += _nbytes(bs, arr.dtype)
    out_specs = [pl.BlockSpec((tm, tn), lambda i, j, *_: (i, j)) for _ in out_dtypes]
    out_shape = [jax.ShapeDtypeStruct((m, n_cols), dt) for dt in out_dtypes]
    blocks += sum(_nbytes((tm, tn), dt) for dt in out_dtypes) + len(w_list) * _nbytes((tm, tn), F32)
    kern = functools.partial(_mm_kernel, n_a=len(a_list), a_of_w=tuple(a_of_w), n_extra=len(extras),
                             epi=epi, has_eids=eids is not None)
    grid_spec = pltpu.PrefetchScalarGridSpec(
        num_scalar_prefetch=0 if eids is None else 1, grid=(m // tm, n_cols // tn),
        in_specs=in_specs, out_specs=out_specs)
    args = ([] if eids is None else [eids]) + list(a_list) + list(w_list) + [e[0] for e in extras]
    outs = pl.pallas_call(kern, grid_spec=grid_spec, out_shape=out_shape,
                          compiler_params=_cparams(("arbitrary", "arbitrary"), blocks), name=name)(*args)
    return outs


def _nt_dot(a, b):
    return lax.dot_general(a, b, (((1,), (1,)), ((), ())), preferred_element_type=F32)


def _lane_blocks(x):
    return [x[:, c * LANES:(c + 1) * LANES] for c in range(x.shape[1] // LANES)]


def _softmax_step(lg, off, vt, carry):
    m, l, acc = carry
    rows, width = lg.shape
    block_max = functools.reduce(jnp.maximum, _lane_blocks(lg))
    row_max = jnp.broadcast_to(jnp.max(block_max, axis=-1, keepdims=True), (rows, LANES))
    m_new = jnp.maximum(m, row_max + off)
    alpha = jnp.exp2(m - m_new)
    p = jnp.exp2(lg - jnp.tile(m_new - off, (1, width // LANES)))
    l = alpha * l + functools.reduce(jnp.add, _lane_blocks(p))
    acc = (jnp.tile(alpha, (1, acc.shape[1] // LANES)) * acc
           + jnp.dot(p.astype(BF16), vt, preferred_element_type=F32))
    return m_new, l, acc


def _softmax_init(rows, width):
    return (jnp.full((rows, LANES), NEG, F32), jnp.zeros((rows, LANES), F32), jnp.zeros((rows, width), F32))


def _softmax_finish(carry):
    _, l, acc = carry
    return acc / jnp.sum(l, axis=-1, keepdims=True)


def _rel_pos(rows, cols):
    return (lax.broadcasted_iota(I32, (rows, cols), 1) - lax.broadcasted_iota(I32, (rows, cols), 0)).astype(F32)


def _tile_loops(n_full, n_kv, past_step, edge_step, carry):
    n_wide = n_full // WIDE_TILES
    carry = lax.fori_loop(0, n_wide, lambda i, c: past_step(i * WIDE_TILES, WIDE_TILES, c), carry)
    carry = lax.fori_loop(n_wide * WIDE_TILES, n_full, lambda t, c: past_step(t, 1, c), carry)
    return lax.fori_loop(n_full, n_kv, edge_step, carry)


def _num_kv_tiles(q0, tq, tk, l_valid):
    kmax = jnp.minimum(l_valid, (((q0 + tq - 1) >> CHUNK_SHIFT) + 1) * CHUNK)
    return (kmax + tk - 1) // tk


def _dsa_kernel(qa_ref, iq_ref, iw_ref, k_ref, v_ref, ik_ref, o_ref, keys_ref, bias_ref, jlim_ref, relpos_ref, *,
                tq, tk, n_past, l_valid, topk, index_bits):
    q0 = n_past + pl.program_id(1) * tq
    n_kv = _num_kv_tiles(q0, tq, tk, l_valid)
    row = lax.broadcasted_iota(I32, (tq, tk), 0)
    col = lax.broadcasted_iota(I32, (tq, tk), 1)
    qpos = q0 + row
    lane = lax.broadcasted_iota(I32, (tq, LANES), 1)

    iw = iw_ref[0] * (I_Q ** -0.5)
    iq = iq_ref[0].astype(F32)
    iq_heads = []
    for h in range(IDX_HEADS):
        pair = iq[:, (h // 2) * LANES:(h // 2 + 1) * LANES]
        keep = (lane < IDX_DIM) if h % 2 == 0 else (lane >= IDX_DIM)
        iq_heads.append(jnp.where(keep, pair, 0.0).astype(BF16))
    iq_all = jnp.concatenate(iq_heads, axis=0)

    def to_key(x):
        bits = lax.bitcast_convert_type(x, I32)
        return bits ^ ((bits >> 31) & 0x7FFFFFFF)

    def score_tiles(t, n, carry):
        top1, top2 = carry
        start = pl.multiple_of(t * tk, tk)
        ikt = ik_ref[0, pl.ds(start, n * tk), :]
        rel = jnp.maximum(_nt_dot(iq_all, ikt), 0.0)
        score = jnp.zeros((tq, n * tk), F32)
        for h in range(IDX_HEADS):
            score = score + iw[:, h:h + 1] * rel[h * tq:(h + 1) * tq]
        for i in range(n):
            kpos = start + i * tk + col
            adm = ((kpos >> CHUNK_SHIFT) <= (qpos >> CHUNK_SHIFT)) & (kpos < l_valid)
            masked = jnp.where(adm, score[:, i * tk:(i + 1) * tk], -jnp.inf)
            keys_ref[t + i] = to_key(masked)
            for blk in _lane_blocks(masked):
                top2 = jnp.maximum(top2, jnp.minimum(top1, blk))
                top1 = jnp.maximum(top1, blk)
        return top1, top2

    n_pair = n_kv // 2
    lows = jnp.full((tq, LANES), -jnp.inf, F32)
    tops = lax.fori_loop(0, n_pair, lambda i, c: score_tiles(2 * i, 2, c), (lows, lows))
    top1, top2 = lax.fori_loop(2 * n_pair, n_kv, lambda t, c: score_tiles(t, 1, c), tops)

    def count(pred):
        def body(t, acc):
            x = jnp.where(pred(keys_ref[t], t), 1.0, 0.0)
            part = x[:, 0:LANES]
            for c in range(1, tk // LANES):
                part = part + x[:, c * LANES:(c + 1) * LANES]
            return acc + part
        acc = lax.fori_loop(0, n_kv, body, jnp.zeros((tq, LANES), F32))
        return jnp.sum(acc, axis=-1, keepdims=True)

    def value_bit(it, res):
        cand = res | lax.shift_left(jnp.int32(1), 31 - it)
        cand_key = cand ^ INT_MIN
        cnt = count(lambda k, t: k >= cand_key)
        return jnp.where(cnt >= topk, cand, res)

    lo_key = to_key(jnp.min(top2, axis=-1, keepdims=True))
    hi_key = to_key(jnp.max(top1, axis=-1, keepdims=True))
    shared = jnp.min(lax.clz(lo_key ^ hi_key).astype(F32)).astype(I32)
    first_bit = jnp.minimum(shared, 31)
    settled = ~lax.shift_right_logical(jnp.int32(-1), first_bit)
    thr = lax.fori_loop(first_bit, 32, value_bit, (lo_key ^ INT_MIN) & settled) ^ INT_MIN
    n_gt = count(lambda k, t: k > thr)
    n_ge = count(lambda k, t: k >= thr)
    want_ties = topk - n_gt
    need = ((n_ge - n_gt) > want_ties) & (thr > MASKED_KEY)
    jlim_ref[...] = jnp.full((tq, LANES), NO_LIMIT, I32)

    @pl.when(jnp.max(jnp.where(need, 1.0, 0.0)) > 0.0)
    def _():
        def index_bit(it, j_lim):
            cand = j_lim | lax.shift_left(jnp.int32(1), index_bits - 1 - it)
            below = count(lambda k, t: (k == thr) & ((t * tk + col) < cand))
            return jnp.where(below <= want_ties - 1.0, cand, j_lim)
        j_lim = lax.fori_loop(0, index_bits, index_bit, jnp.zeros((tq, 1), I32))
        jlim_ref[...] = jnp.broadcast_to(jnp.where(need, j_lim, NO_LIMIT), (tq, LANES))

    j_lim = jlim_ref[:, 0:1]

    def bias_tile(t, carry):
        k = keys_ref[t]
        kpos = t * tk + col
        sel = (k > thr) | ((k == thr) & (kpos <= j_lim))
        sel = sel & (k > MASKED_KEY)
        bias_ref[t] = jnp.where(sel, 0.0, NEG)
        return carry

    lax.fori_loop(0, n_kv, bias_tile, 0)

    qk_scale = HEAD_DIM ** -0.5 * LOG2E
    n_full = jnp.minimum(q0, l_valid) // tk

    @pl.when((pl.program_id(0) == 0) & (pl.program_id(1) == 0))
    def _():
        relpos_ref[...] = _rel_pos(tq, WIDE_TILES * tk)

    for j in range(A_KV_HEADS):
        q4 = jnp.concatenate([qa_ref[0, :, (j * A_GROUP + g) * HEAD_DIM:(j * A_GROUP + g + 1) * HEAD_DIM]
                              for g in range(A_GROUP)], axis=0)
        slopes2 = [2.0 ** -(j * A_GROUP + g + 1) * LOG2E for g in range(A_GROUP)]
        slope_rows = jnp.concatenate([jnp.full((tq, LANES), s2, F32) for s2 in slopes2], axis=0)

        def tile_operands(t, n, j=j, q4=q4):
            start = pl.multiple_of(t * tk, tk)
            kt = k_ref[0, pl.ds(start, n * tk), j * HEAD_DIM:(j + 1) * HEAD_DIM]
            vt = v_ref[0, pl.ds(start, n * tk), j * HEAD_DIM:(j + 1) * HEAD_DIM]
            bias = bias_ref[t] if n == 1 else jnp.concatenate([bias_ref[t + i] for i in range(n)], axis=1)
            return start, _nt_dot(q4, kt), vt, bias

        def past_step(t, n, carry, slopes2=slopes2, slope_rows=slope_rows):
            start, s, vt, bias = tile_operands(t, n)
            lg = jnp.concatenate([s[g * tq:(g + 1) * tq] * qk_scale + (relpos_ref[:, :n * tk] * slopes2[g] + bias)
                                  for g in range(A_GROUP)], axis=0)
            off = slope_rows * (start - q0).astype(F32)
            return _softmax_step(lg, off, vt, carry)

        def edge_step(t, carry, slopes2=slopes2):
            start, s, vt, bias = tile_operands(t, 1)
            dist = jnp.abs(qpos - (start + col)).astype(F32)
            lg = jnp.concatenate([s[g * tq:(g + 1) * tq] * qk_scale - slopes2[g] * dist + bias
                                  for g in range(A_GROUP)], axis=0)
            return _softmax_step(lg, 0.0, vt, carry)

        out = _softmax_finish(_tile_loops(n_full, n_kv, past_step, edge_step,
                                          _softmax_init(A_GROUP * tq, HEAD_DIM)))
        for g in range(A_GROUP):
            h = j * A_GROUP + g
            o_ref[0, :, h * HEAD_DIM:(h + 1) * HEAD_DIM] = out[g * tq:(g + 1) * tq].astype(o_ref.dtype)


def _dsa_attention(zb, zf, keys, off, *, n_past, l_valid, tq, tk, topk, name):
    b, s, _ = zb.shape
    karr, kcol, varr, vcol, ikarr, ikcol = keys
    lp = karr.shape[1]
    n_tiles = lp // tk
    kern = functools.partial(_dsa_kernel, tq=tq, tk=tk, n_past=n_past, l_valid=l_valid, topk=topk,
                             index_bits=lp.bit_length())
    blocks = (_nbytes((tq, A_Q + I_Q), BF16) + _nbytes((tq, LANES), F32) + 2 * _nbytes((lp, A_KV), BF16)
              + _nbytes((lp, LANES), BF16) + _nbytes((tq, A_Q), BF16) + _nbytes((tq, lp), F32)
              + 8 * _nbytes((A_GROUP * tq, tk), F32))
    return pl.pallas_call(
        kern,
        grid=(b, s // tq),
        in_specs=[pl.BlockSpec((1, tq, A_Q), lambda bi, qi: (bi, qi, off['aq'] // A_Q)),
                  pl.BlockSpec((1, tq, I_Q), lambda bi, qi: (bi, qi, off['iq'] // I_Q)),
                  pl.BlockSpec((1, tq, LANES), lambda bi, qi: (bi, qi, off['iw'] // LANES)),
                  pl.BlockSpec((1, lp, A_KV), lambda bi, qi: (bi, 0, kcol)),
                  pl.BlockSpec((1, lp, A_KV), lambda bi, qi: (bi, 0, vcol)),
                  pl.BlockSpec((1, lp, LANES), lambda bi, qi: (bi, 0, ikcol))],
        out_specs=pl.BlockSpec((1, tq, A_Q), lambda bi, qi: (bi, qi, 0)),
        out_shape=jax.ShapeDtypeStruct((b, s, A_Q), BF16),
        scratch_shapes=[pltpu.VMEM((n_tiles, tq, tk), I32), pltpu.VMEM((n_tiles, tq, tk), F32),
                        pltpu.VMEM((tq, LANES), I32), pltpu.VMEM((tq, WIDE_TILES * tk), F32)],
        compiler_params=_cparams(("arbitrary", "arbitrary"), blocks),
        name=name,
    )(zb, zb, zf, karr, varr, ikarr)


def _diff_kernel(q_ref, k_ref, v_ref, lq_ref, g_ref, o_ref, relbias_ref, *, tq, tk, n_past, l_valid, lam_init):
    h = pl.program_id(1)
    q0 = n_past + pl.program_id(2) * tq
    n_kv = _num_kv_tiles(q0, tq, tk, l_valid)
    row = lax.broadcasted_iota(I32, (tq, tk), 0)
    col = lax.broadcasted_iota(I32, (tq, tk), 1)
    qpos = q0 + row
    slope2 = jnp.float32(1.0)
    for hh in range(B_HEADS):
        slope2 = jnp.where(h == hh, jnp.float32(2.0 ** (-8.0 * (hh + 1) / B_HEADS) * LOG2E), slope2)
    qk_scale = HEAD_DIM ** -0.5 * LOG2E
    n_full = jnp.minimum(q0, l_valid) // tk

    @pl.when(pl.program_id(2) == 0)
    def _():
        relbias_ref[...] = _rel_pos(tq, WIDE_TILES * tk) * slope2

    qc = [q_ref[0, :, c * HEAD_DIM:(c + 1) * HEAD_DIM] for c in range(2)]

    def tile_operands(t, n):
        start = pl.multiple_of(t * tk, tk)
        s = [_nt_dot(qc[c], k_ref[0, pl.ds(start, n * tk), c * HEAD_DIM:(c + 1) * HEAD_DIM]) for c in range(2)]
        return start, s, v_ref[0, pl.ds(start, n * tk), :]

    def past_step(t, n, carry):
        start, s, vt = tile_operands(t, n)
        lg = jnp.concatenate([s[c] * qk_scale + relbias_ref[:, :n * tk] for c in range(2)], axis=0)
        return _softmax_step(lg, slope2 * (start - q0).astype(F32), vt, carry)

    def edge_step(t, carry):
        start, s, vt = tile_operands(t, 1)
        kpos = start + col
        adm = ((kpos >> CHUNK_SHIFT) <= (qpos >> CHUNK_SHIFT)) & (kpos < l_valid)
        bias = jnp.where(adm, -slope2 * jnp.abs(qpos - kpos).astype(F32), NEG)
        lg = jnp.concatenate([s[c] * qk_scale + bias for c in range(2)], axis=0)
        return _softmax_step(lg, 0.0, vt, carry)

    out = _softmax_finish(_tile_loops(n_full, n_kv, past_step, edge_step,
                                      _softmax_init(2 * tq, 2 * HEAD_DIM)))
    lq = lq_ref[...]
    lam = (jnp.exp(jnp.sum(lq[0:1] * lq[1:2], axis=-1, keepdims=True))
           - jnp.exp(jnp.sum(lq[2:3] * lq[3:4], axis=-1, keepdims=True)) + lam_init)
    o = out[:tq] - lam * out[tq:]
    y = o * lax.rsqrt(jnp.mean(o * o, axis=-1, keepdims=True) + SUBLN_EPS)
    y = (y * g_ref[...]) * (1.0 - lam_init)
    o_ref[0] = y.astype(o_ref.dtype)


def _diff_attention(zb, keys, lam_qk_l, g_subln_l, off, *, n_past, l_valid, tq, tk, lam_init, name):
    b, s, _ = zb.shape
    karr, kcol0, varr, vcol0 = keys
    lp = karr.shape[1]
    w = 2 * HEAD_DIM
    kern = functools.partial(_diff_kernel, tq=tq, tk=tk, n_past=n_past, l_valid=l_valid, lam_init=lam_init)
    blocks = (2 * _nbytes((tq, w), BF16) + 2 * _nbytes((lp, w), BF16) + 8 * _nbytes((2 * tq, tk), F32))
    return pl.pallas_call(
        kern,
        grid=(b, B_HEADS, s // tq),
        in_specs=[pl.BlockSpec((1, tq, w), lambda bi, h, qi: (bi, qi, off['bq'] // w + h)),
                  pl.BlockSpec((1, lp, w), lambda bi, h, qi: (bi, 0, kcol0 + h)),
                  pl.BlockSpec((1, lp, w), lambda bi, h, qi: (bi, 0, vcol0 + h)),
                  pl.BlockSpec((4, HEAD_DIM), lambda bi, h, qi: (0, 0)),
                  pl.BlockSpec((1, w), lambda bi, h, qi: (0, 0))],
        out_specs=pl.BlockSpec((1, tq, w), lambda bi, h, qi: (bi, qi, h)),
        out_shape=jax.ShapeDtypeStruct((b, s, B_V), BF16),
        scratch_shapes=[pltpu.VMEM((tq, WIDE_TILES * tk), F32)],
        compiler_params=_cparams(("arbitrary", "arbitrary", "arbitrary"), blocks),
        name=name,
    )(zb, karr, varr, lam_qk_l, g_subln_l.reshape(1, w))


def _row_gather(src_ref, idx_ref, idx_base, idx_stride, dst_ref, sem, n_rows):
    def row_copy(r):
        return pltpu.make_async_copy(src_ref.at[pl.ds(idx_ref[idx_base + r * idx_stride], 1)],
                                     dst_ref.at[pl.ds(r, 1)], sem)

    def start():
        lax.fori_loop(0, n_rows, lambda r, c: (row_copy(r).start(), c)[1], 0)

    def wait():
        pltpu.make_async_copy(dst_ref.at[pl.ds(0, n_rows)], dst_ref.at[pl.ds(0, n_rows)], sem).wait()

    return start, wait


def _moe_up_kernel(eid_ref, tok_ref, h_ref, wg_ref, wu_ref, o_ref, rows_ref, rows16_ref, sems, *, blk, n_blocks):
    i, j = pl.program_id(0), pl.program_id(1)
    slot = i % 2

    def gather(block, buf):
        return _row_gather(h_ref, tok_ref, block * blk, 1, rows_ref.at[buf], sems.at[buf], blk)

    @pl.when((i == 0) & (j == 0))
    def _():
        gather(0, 0)[0]()

    @pl.when(j == 0)
    def _():
        gather(i, slot)[1]()

        @pl.when(i + 1 < n_blocks)
        def _():
            gather(i + 1, 1 - slot)[0]()

        rows16_ref[...] = rows_ref[slot].astype(BF16)

    a = rows16_ref[...]
    u = jnp.dot(a, wg_ref[0], preferred_element_type=F32)
    v = jnp.dot(a, wu_ref[0], preferred_element_type=F32)
    o_ref[...] = ((u * jax.nn.sigmoid(u)) * v).astype(o_ref.dtype)


def _moe_up(h2, tok_buf, block_expert, w_gate, w_up, *, blk, name):
    p = tok_buf.shape[0]
    d = h2.shape[1]
    fp = w_gate.shape[-1]
    tn = _tile(fp, 512)
    n_blocks = p // blk
    blocks = (_nbytes((blk, d), F32) + _nbytes((blk, d), BF16) // 2 + 2 * _nbytes((d, tn), BF16)
              + _nbytes((blk, tn), BF16) + _nbytes((blk, tn), F32))
    return pl.pallas_call(
        functools.partial(_moe_up_kernel, blk=blk, n_blocks=n_blocks),
        grid_spec=pltpu.PrefetchScalarGridSpec(
            num_scalar_prefetch=2, grid=(n_blocks, fp // tn),
            in_specs=[pl.BlockSpec(memory_space=pl.ANY),
                      pl.BlockSpec((1, d, tn), lambda i, j, e, tok: (e[i], 0, j)),
                      pl.BlockSpec((1, d, tn), lambda i, j, e, tok: (e[i], 0, j))],
            out_specs=pl.BlockSpec((blk, tn), lambda i, j, e, tok: (i, j)),
            scratch_shapes=[pltpu.VMEM((2, blk, d), F32), pltpu.VMEM((blk, d), BF16),
                            pltpu.SemaphoreType.DMA((2,))]),
        out_shape=jax.ShapeDtypeStruct((p, fp), BF16),
        compiler_params=_cparams(("arbitrary", "arbitrary"), blocks),
        name=name,
    )(block_expert, tok_buf, h2, w_gate, w_up)


def _combine_kernel(dest_ref, x_ref, y_ref, rg_ref, gate_ref, o_ref, picked_ref, sems, *, tm):
    base = pl.program_id(0) * tm * TOP_K_EXPERTS
    copies = [_row_gather(y_ref, dest_ref, base + k, TOP_K_EXPERTS, picked_ref.at[k], sems.at[k], tm)
              for k in range(TOP_K_EXPERTS)]
    for start, _ in copies:
        start()
    for _, wait in copies:
        wait()
    rg = rg_ref[...]
    f = rg[:, 0:1] * picked_ref[0] + rg[:, 1:2] * picked_ref[1]
    o_ref[...] = x_ref[...] + gate_ref[0] * f


def _moe_combine(x2, yb, dest, rgate, gate_f, seq, *, name):
    t, d = x2.shape
    tm = _tile(seq, 256)
    per_b = seq // tm
    return pl.pallas_call(
        functools.partial(_combine_kernel, tm=tm),
        grid_spec=pltpu.PrefetchScalarGridSpec(
            num_scalar_prefetch=1, grid=(t // tm,),
            in_specs=[pl.BlockSpec((tm, d), lambda i, dst: (i, 0)),
                      pl.BlockSpec(memory_space=pl.ANY),
                      pl.BlockSpec((tm, LANES), lambda i, dst: (i, 0)),
                      pl.BlockSpec((1, 1, d), lambda i, dst: (i // per_b, 0, 0))],
            out_specs=pl.BlockSpec((tm, d), lambda i, dst: (i, 0)),
            scratch_shapes=[pltpu.VMEM((TOP_K_EXPERTS, tm, d), F32), pltpu.SemaphoreType.DMA((TOP_K_EXPERTS,))]),
        out_shape=jax.ShapeDtypeStruct((t, d), F32),
        compiler_params=_cparams(("arbitrary",), 3 * _nbytes((tm, d), F32)),
        name=name,
    )(dest, x2, yb, rgate, gate_f)


STATE_COLS = (('ak', A_KV), ('av', A_KV), ('ik', LANES), ('bk', B_QK), ('bv', B_V))


def _state_tails():
    return ((A_KV_HEADS, HEAD_DIM), (A_KV_HEADS, HEAD_DIM), (IDX_DIM,), (B_HEADS, 2, HEAD_DIM),
            (B_HEADS, 2 * HEAD_DIM))


def _state_kernel(*refs, depth):
    n = len(STATE_COLS)
    ins, outs = refs[:depth * n], refs[depth * n:]
    for l in range(depth):
        @pl.when(pl.program_id(0) == l)
        def _(l=l):
            ak, av, ik, bk, bv = ins[l * n:(l + 1) * n]
            for j in range(A_KV_HEADS):
                outs[0][0, 0, :, j, :] = ak[0, :, j * HEAD_DIM:(j + 1) * HEAD_DIM]
                outs[1][0, 0, :, j, :] = av[0, :, j * HEAD_DIM:(j + 1) * HEAD_DIM]
            outs[2][0, 0] = ik[0, :, :IDX_DIM]
            for h in range(B_HEADS):
                for c in range(2):
                    lo = (2 * h + c) * HEAD_DIM
                    outs[3][0, 0, :, h, c, :] = bk[0, :, lo:lo + HEAD_DIM]
                outs[4][0, 0, :, h, :] = bv[0, :, 2 * h * HEAD_DIM:2 * (h + 1) * HEAD_DIM]


def _state_rows(zf3_layers, off, *, name):
    depth = len(zf3_layers)
    b, s, _ = zf3_layers[0].shape
    ts = _tile(s, 256)
    in_specs, args = [], []
    for l in range(depth):
        for cname, width in STATE_COLS:
            blk = off[cname] // width
            in_specs.append(pl.BlockSpec(
                (1, ts, width),
                lambda li, bi, i, l=l, blk=blk: (jnp.where(li == l, bi, 0), jnp.where(li == l, i, 0), blk)))
            args.append(zf3_layers[l])
    tails = _state_tails()
    out_specs = [pl.BlockSpec((1, 1, ts) + t, lambda li, bi, i, nt=len(t): (li, bi, i) + (0,) * nt) for t in tails]
    out_shape = [jax.ShapeDtypeStruct((depth, b, s) + t, F32) for t in tails]
    blocks = (depth + 1) * sum(_nbytes((ts, w), F32) for _, w in STATE_COLS)
    return pl.pallas_call(
        functools.partial(_state_kernel, depth=depth),
        grid=(depth, b, s // ts), in_specs=in_specs, out_specs=out_specs, out_shape=out_shape,
        compiler_params=_cparams(("arbitrary", "arbitrary", "arbitrary"), blocks), name=name,
    )(*args)


def _pack_kernel(c_ref, new_ref, o_ref, *, n_cached_tiles, tail):
    i = pl.program_id(1)
    piece = tail[-1]

    @pl.when(i < n_cached_tiles)
    def _():
        for n, idx in enumerate(itertools.product(*[range(t) for t in tail[:-1]])):
            o_ref[0, :, n * piece:(n + 1) * piece] = c_ref[(0, 0, slice(None)) + idx + (slice(None),)].astype(BF16)

    @pl.when(i >= n_cached_tiles)
    def _():
        o_ref[0] = new_ref[0]


def _pack_keys(cache, layer, new_rows, *, tk, name):
    _, b, p = cache.shape[:3]
    tail = cache.shape[3:]
    w = math.prod(tail)
    assert p % tk == 0 and new_rows.shape == (b, tk, w)
    n_cached_tiles = p // tk
    zeros = (0,) * len(tail)
    return pl.pallas_call(
        functools.partial(_pack_kernel, n_cached_tiles=n_cached_tiles, tail=tail),
        grid=(b, n_cached_tiles + 1),
        in_specs=[pl.BlockSpec((1, 1, tk) + tail,
                               lambda bi, i: (layer, bi, jnp.minimum(i, n_cached_tiles - 1)) + zeros),
                  pl.BlockSpec((1, tk, w), lambda bi, i: (bi, 0, 0))],
        out_specs=pl.BlockSpec((1, tk, w), lambda bi, i: (bi, i, 0)),
        out_shape=jax.ShapeDtypeStruct((b, p + tk, w), BF16),
        compiler_params=_cparams(("arbitrary", "arbitrary"), _nbytes((tk, w), F32) + 2 * _nbytes((tk, w), BF16)),
        name=name,
    )(cache, new_rows)


def _silu_mul(accs, extras):
    u, v = accs
    return [(u * jax.nn.sigmoid(u)) * v]


def _moe_ffn(x2, h2, ridx, rgate, gate_f, seq, w_gate, w_up, w_down, *, tag):
    t, d = x2.shape
    fp = w_gate.shape[-1]
    a = t * TOP_K_EXPERTS
    blk = _tile(a, 512)
    n_blocks = -(-(a + N_EXPERTS * (blk - 1)) // blk)
    p = n_blocks * blk
    e_flat = ridx[:, :TOP_K_EXPERTS].reshape(-1)
    onehot = (e_flat[:, None] == jnp.arange(N_EXPERTS, dtype=I32)[None, :]).astype(I32)
    rank = jnp.sum((jnp.cumsum(onehot, axis=0) - onehot) * onehot, axis=1)
    counts = jnp.sum(onehot, axis=0)
    padded = ((counts + blk - 1) // blk) * blk
    pends = jnp.cumsum(padded)
    pstarts = pends - padded
    dest = (pstarts[e_flat] + rank).astype(I32)
    tok_buf = jnp.zeros((p,), I32).at[dest].set(jnp.arange(a, dtype=I32) // TOP_K_EXPERTS)
    block_expert = jnp.clip(jnp.searchsorted(pends, jnp.arange(n_blocks, dtype=I32) * blk, side='right'),
                            0, N_EXPERTS - 1).astype(I32)

    act = _moe_up(h2, tok_buf, block_expert, w_gate, w_up, blk=blk, name=f"moe_up_{tag}")
    (yb,) = _matmul([act], [w_down], [0], [], lambda accs, ex: accs, [F32], d, tm=blk, tn=_tile(d, 512),
                    eids=block_expert, name=f"moe_down_{tag}")
    return _moe_combine(x2, yb, dest, rgate, gate_f, seq, name=f"moe_combine_{tag}")


def _trunk(x, mods, past_all, wts, tag):
    b, s, d = x.shape
    t = b * s
    depth = len(wts['w_in'])
    off, nz = _z_layout(d)
    x2 = x.reshape(t, d)
    tn_d = _tile(d, 512)
    tm_b = _tile(s, 512)
    tm_f = _tile(t, 1024)
    per_b = s // tm_b
    zf3_layers = []
    for l in range(depth):
        shift_a, scale_a, gate_a, shift_f, scale_f, gate_f = [
            mods[l][:, i * d:(i + 1) * d].reshape(b, 1, d) for i in range(N_ADA)]
        batch_tile = lambda i, j, *_: (i // per_b, 0, j)

        h = _norm(x2, wts['g_attn'][l], s, scale=scale_a, shift=shift_a, out_dtype=BF16,
                  name=f"norm_attn_{tag}{l}")
        zf, zb = _matmul([h], [wts['w_in'][l]], [0], [], lambda accs, ex: [accs[0], accs[0]], [F32, BF16], nz,
                         tm=tm_f, tn=IN_PROJ_TN, name=f"in_proj_{tag}{l}")
        zf3, zb3 = zf.reshape(b, s, nz), zb.reshape(b, s, nz)
        cut = lambda z, name, width: z[:, :, off[name]:off[name] + width]
        zf3_layers.append(zf3)

        if past_all is None:
            n_past, l_valid = 0, s
            tk = _tile(s, 512)
            tq_a, tq_b = _tile(s, 128), _tile(s, 256)
            dsa_keys = (zb3, off['ak'] // A_KV, zb3, off['av'] // A_KV, zb3, off['ik'] // LANES)
            diff_keys = (zb3, off['bk'] // (2 * HEAD_DIM), zb3, off['bv'] // (2 * HEAD_DIM))
        else:
            n_past = past_all[0].shape[2]
            l_valid = n_past + s
            tk = 512
            tq_a = tq_b = s

            def new_rows(name, width):
                return jnp.concatenate([cut(zb3, name, width), jnp.zeros((b, tk - s, width), BF16)], axis=1)

            def with_past(ci, name, width):
                return _pack_keys(past_all[ci], l, new_rows(name, width), tk=tk, name=f"pack_{name}_{tag}{l}")

            ik_past = past_all[2][l].astype(BF16)
            ik_all = jnp.concatenate([jnp.concatenate([ik_past, ik_past], axis=-1), new_rows('ik', LANES)], axis=1)
            dsa_keys = (with_past(0, 'ak', A_KV), 0, with_past(1, 'av', A_KV), 0, ik_all, 0)
            diff_keys = (with_past(3, 'bk', B_QK), 0, with_past(4, 'bv', B_V), 0)
        topk = min(TOPK_MAX, l_valid // 4)
        oa = _dsa_attention(zb3, zf3, dsa_keys, off, n_past=n_past, l_valid=l_valid, tq=tq_a, tk=tk,
                            topk=topk, name=f"dsa_{tag}{l}")
        lam_init = 0.8 - 0.6 * math.exp(-0.3 * l)
        ob = _diff_attention(zb3, diff_keys, wts['lam_qk'][l], wts['g_subln'][l], off, n_past=n_past,
                             l_valid=l_valid, tq=tq_b, tk=tk, lam_init=lam_init, name=f"diff_{tag}{l}")

        def merge_epi(accs, ex):
            return [jax.nn.sigmoid(ex[0]) * accs[0] + jax.nn.sigmoid(ex[1]) * accs[1]]

        (merged,) = _matmul(
            [oa.reshape(t, A_Q), ob.reshape(t, B_V)], [wts['w_out_a'][l], wts['w_out_b'][l]], [0, 1],
            [(zf, (tm_f, tn_d), lambda i, j, *_: (i, off['ga'] // tn_d + j)),
             (zf, (tm_f, tn_d), lambda i, j, *_: (i, off['gb'] // tn_d + j))],
            merge_epi, [BF16], d, tm=tm_f, tn=tn_d, name=f"merge_{tag}{l}")

        def resid_epi(accs, ex):
            return [ex[0] + ex[1][0] * accs[0]]

        (x2,) = _matmul([merged], [wts['w_out'][l]], [0],
                        [(x2, (tm_b, tn_d), lambda i, j, *_: (i, j)), (gate_a, (1, 1, tn_d), batch_tile)],
                        resid_epi, [F32], d, tm=tm_b, tn=tn_d, name=f"out_proj_{tag}{l}")

        if l % 2 == 0:
            i = l // 2
            h = _norm(x2, wts['g_ffn'][l], s, scale=scale_f, shift=shift_f, out_dtype=BF16,
                      name=f"norm_ffn_{tag}{l}")
            fp = wts['w_ff_gate'][i].shape[-1]
            (act,) = _matmul([h], [wts['w_ff_gate'][i], wts['w_ff_up'][i]], [0, 0], [], _silu_mul, [BF16], fp,
                             tm=tm_f, tn=_tile(fp, 512), name=f"ffn_up_{tag}{l}")
            (x2,) = _matmul([act], [wts['w_ff_down'][i]], [0],
                            [(x2, (tm_b, tn_d), lambda i, j, *_: (i, j)), (gate_f, (1, 1, tn_d), batch_tile)],
                            resid_epi, [F32], d, tm=tm_b, tn=tn_d, name=f"ffn_down_{tag}{l}")
        else:
            i = l // 2
            h2, ridx, rgate = _norm(x2, wts['g_ffn'][l], s, scale=scale_f, shift=shift_f,
                                    router=(wts['w_router'][i], wts['b_router'][i]), out_dtype=F32,
                                    name=f"norm_router_{tag}{l}")
            x2 = _moe_ffn(x2, h2, ridx, rgate, gate_f, s, wts['w_moe_gate'][i], wts['w_moe_up'][i],
                          wts['w_moe_down'][i], tag=f"{tag}{l}")
    y = _norm(x2, wts['g_final'], s, out_dtype=F32, name=f"norm_final_{tag}")
    state = tuple(_state_rows(zf3_layers, off, name=f"state_rows_{tag}"))
    return y.reshape(b, s, d), state


def _prep_weights(w_in, w_out_a, w_out_b, w_out, w_ff_gate, w_ff_up, w_ff_down, w_router, b_router,
                  w_moe_gate, w_moe_up, w_moe_down, d_model):
    f = w_ff_gate.shape[-1]
    fp = _round_up(f, 512)
    pad_cols = lambda w: jnp.pad(w, [(0, 0)] * (w.ndim - 1) + [(0, fp - f)]).astype(BF16)
    pad_rows = lambda w: jnp.pad(w, [(0, 0)] * (w.ndim - 2) + [(0, fp - f), (0, 0)]).astype(BF16)
    n_moe = w_router.shape[0]
    return dict(
        w_in=[_prep_w_in(w_in[l], d_model) for l in range(w_in.shape[0])],
        w_out_a=w_out_a.astype(BF16), w_out_b=w_out_b.astype(BF16), w_out=w_out.astype(BF16),
        w_ff_gate=pad_cols(w_ff_gate), w_ff_up=pad_cols(w_ff_up), w_ff_down=pad_rows(w_ff_down),
        w_router=[jnp.pad(w_router[i], ((0, 0), (0, LANES - N_EXPERTS))) for i in range(n_moe)],
        b_router=[jnp.pad(b_router[i], (0, LANES - N_EXPERTS)).reshape(1, LANES) for i in range(n_moe)],
        w_moe_gate=pad_cols(w_moe_gate), w_moe_up=pad_cols(w_moe_up), w_moe_down=pad_rows(w_moe_down),
    )


def kernel(x_prompt, x_sample, c_prompt, c_sample, cache_dsa_k, cache_dsa_v, cache_idx_k, cache_diff_k, cache_diff_v, w_ada, b_ada, g_attn, w_in, w_out_a, w_out_b, w_out, lam_qk, g_subln, g_ffn, w_ff_gate, w_ff_up, w_ff_down, w_router, b_router, w_moe_gate, w_moe_up, w_moe_down, g_final):
    d = x_prompt.shape[-1]
    wts = _prep_weights(w_in, w_out_a, w_out_b, w_out, w_ff_gate, w_ff_up, w_ff_down, w_router, b_router,
                        w_moe_gate, w_moe_up, w_moe_down, d)
    wts.update(g_attn=g_attn, g_ffn=g_ffn, g_final=g_final, lam_qk=lam_qk, g_subln=g_subln)
    nb_p = c_prompt.shape[0]
    mods = _ada_mod(jnp.concatenate([c_prompt, c_sample], axis=0), w_ada, b_ada)
    y_p, st_p = _trunk(x_prompt, mods[:, :nb_p], None, wts, "p")
    past_all = (cache_dsa_k, cache_dsa_v, cache_idx_k, cache_diff_k, cache_diff_v)
    y_s, st_s = _trunk(x_sample, mods[:, nb_p:], past_all, wts, "s")
    return (y_p, y_s) + st_p + st_s
```

```python
import functools
import itertools
import math

import jax
import jax.numpy as jnp
from jax import lax
from jax.experimental import pallas as pl
from jax.experimental.pallas import tpu as pltpu

F32, BF16, I32 = jnp.float32, jnp.bfloat16, jnp.int32

CHUNK = 64
CHUNK_SHIFT = CHUNK.bit_length() - 1
HEAD_DIM = 128
A_HEADS = 8
A_KV_HEADS = 2
A_GROUP = A_HEADS // A_KV_HEADS
IDX_HEADS = 8
IDX_DIM = 64
TOPK_MAX = 256
B_HEADS = 4
N_EXPERTS = 8
TOP_K_EXPERTS = 2
N_ADA = 6
NORM_EPS = 1e-6
SUBLN_EPS = 1e-5
A_Q = A_HEADS * HEAD_DIM
A_KV = A_KV_HEADS * HEAD_DIM
I_Q = IDX_HEADS * IDX_DIM
B_QK = B_HEADS * 2 * HEAD_DIM
B_V = B_HEADS * 2 * HEAD_DIM

LANES = 128
VMEM_BYTES_V7X = 64 * 2 ** 20
NEG = -1e30
NO_LIMIT = 2 ** 30
LOG2E = math.log2(math.e)
IN_PROJ_TN = 512
MASKED_KEY = 0x807FFFFF - 2 ** 32
WIDE_TILES = 4
ISSUE_UNROLL = 8


def _cparams(dims, block_bytes):
    limit = min(max(2 * int(block_bytes) + (8 << 20), 32 << 20), VMEM_BYTES_V7X - (6 << 20))
    return pltpu.CompilerParams(dimension_semantics=dims, vmem_limit_bytes=limit)


def _nbytes(shape, dtype):
    return math.prod(shape) * jnp.dtype(dtype).itemsize


def _tile(n, pref):
    if n <= pref:
        return n
    t = pref
    while n % t:
        t //= 2
    return t


def _round_up(n, m):
    return -(-n // m) * m


def _z_layout(d_model):
    off, o = {}, 0
    for name, n in (('aq', A_Q), ('ak', A_KV), ('av', A_KV), ('iq', I_Q), ('bq', B_QK), ('bk', B_QK),
                    ('bv', B_V), ('ga', d_model), ('gb', d_model), ('ik', LANES), ('iw', LANES)):
        off[name] = o
        o += n
    return off, _round_up(o, IN_PROJ_TN)


def _prep_w_in(w, d_model):
    sizes = (A_Q, A_KV, A_KV, I_Q, IDX_DIM, IDX_HEADS, B_QK, B_QK, B_V, d_model, d_model)
    names = ('aq', 'ak', 'av', 'iq', 'ik', 'iw', 'bq', 'bk', 'bv', 'ga', 'gb')
    parts, o = {}, 0
    for n, s in zip(names, sizes):
        parts[n] = w[:, o:o + s]
        o += s
    cols = [parts[n] for n in ('aq', 'ak', 'av', 'iq', 'bq', 'bk', 'bv', 'ga', 'gb')]
    cols += [parts['ik'], parts['ik'], parts['iw']]
    used = sum(c.shape[1] for c in cols)
    cols.append(jnp.zeros((w.shape[0], _z_layout(d_model)[1] - used), w.dtype))
    return jnp.concatenate(cols, axis=1).astype(BF16)


def _ada_kernel(c_ref, w_ref, b_ref, o_ref):
    c = c_ref[...]
    s = c * jax.nn.sigmoid(c)
    o_ref[0] = jnp.dot(s.astype(BF16), w_ref[0].astype(BF16), preferred_element_type=F32) + b_ref[0]


def _ada_mod(c_all, w_ada, b_ada):
    depth, d, n = w_ada.shape
    nb = c_all.shape[0]
    tn = _tile(n, 1024)
    blocks = _nbytes((nb, d), F32) + _nbytes((d, tn), F32) * 2 + _nbytes((nb, tn), F32)
    return pl.pallas_call(
        _ada_kernel,
        grid=(depth, n // tn),
        in_specs=[pl.BlockSpec((nb, d), lambda l, j: (0, 0)),
                  pl.BlockSpec((1, d, tn), lambda l, j: (l, 0, j)),
                  pl.BlockSpec((1, 1, tn), lambda l, j: (l, 0, j))],
        out_specs=pl.BlockSpec((1, nb, tn), lambda l, j: (l, 0, j)),
        out_shape=jax.ShapeDtypeStruct((depth, nb, n), F32),
        compiler_params=_cparams(("arbitrary", "arbitrary"), blocks),
        name="ada_mod",
    )(c_all, w_ada, b_ada.reshape(depth, 1, n))


def _norm_kernel(*refs, modulated, router, eps):
    refs = list(refs)
    x_ref, g_ref = refs[:2]
    pos = 2
    x = x_ref[...]
    y = x * lax.rsqrt(jnp.mean(x * x, axis=-1, keepdims=True) + eps)
    y = y * g_ref[...]
    if modulated:
        scale_ref, shift_ref = refs[pos:pos + 2]
        pos += 2
        y = y * (1.0 + scale_ref[0]) + shift_ref[0]
    if router:
        wr_ref, br_ref = refs[pos:pos + 2]
        pos += 2
    o_ref = refs[pos]
    o_ref[...] = y.astype(o_ref.dtype)
    if router:
        idx_ref, gate_ref = refs[pos + 1:pos + 3]
        logits = jnp.dot(y, wr_ref[...], preferred_element_type=F32,
                         precision=lax.Precision.HIGHEST) + br_ref[...]
        lane = lax.broadcasted_iota(I32, logits.shape, 1)
        logits = jnp.where(lane < N_EXPERTS, logits, -jnp.inf)
        m1 = jnp.max(logits, axis=-1, keepdims=True)
        i1 = jnp.min(jnp.where(logits == m1, lane, LANES), axis=-1, keepdims=True)
        rest = jnp.where(lane == i1, -jnp.inf, logits)
        m2 = jnp.max(rest, axis=-1, keepdims=True)
        i2 = jnp.min(jnp.where(rest == m2, lane, LANES), axis=-1, keepdims=True)
        e = jnp.exp(m2 - m1)
        g1 = 1.0 / (1.0 + e)
        g2 = e / (1.0 + e)
        idx_ref[...] = jnp.where(lane == 0, i1, jnp.where(lane == 1, i2, 0))
        gate_ref[...] = jnp.where(lane == 0, g1, jnp.where(lane == 1, g2, 0.0))


def _norm(x2, g, seq, *, scale=None, shift=None, router=None, out_dtype, eps=NORM_EPS, name):
    t, d = x2.shape
    tm = _tile(seq, 256)
    per_b = seq // tm
    in_specs = [pl.BlockSpec((tm, d), lambda i: (i, 0)), pl.BlockSpec((1, d), lambda i: (0, 0))]
    args = [x2, g.reshape(1, d)]
    if scale is not None:
        in_specs += [pl.BlockSpec((1, 1, d), lambda i: (i // per_b, 0, 0))] * 2
        args += [scale, shift]
    out_specs = [pl.BlockSpec((tm, d), lambda i: (i, 0))]
    out_shape = [jax.ShapeDtypeStruct((t, d), out_dtype)]
    if router is not None:
        w_r, b_r = router
        in_specs += [pl.BlockSpec((d, LANES), lambda i: (0, 0)), pl.BlockSpec((1, LANES), lambda i: (0, 0))]
        args += [w_r, b_r]
        out_specs += [pl.BlockSpec((tm, LANES), lambda i: (i, 0))] * 2
        out_shape += [jax.ShapeDtypeStruct((t, LANES), I32), jax.ShapeDtypeStruct((t, LANES), F32)]
    blocks = 3 * _nbytes((tm, d), F32) + _nbytes((d, LANES), F32)
    outs = pl.pallas_call(
        functools.partial(_norm_kernel, modulated=scale is not None, router=router is not None, eps=eps),
        grid=(t // tm,), in_specs=in_specs, out_specs=out_specs, out_shape=out_shape,
        compiler_params=_cparams(("arbitrary",), blocks), name=name,
    )(*args)
    return outs if router is not None else outs[0]


def _mm_kernel(*refs, n_a, a_of_w, n_extra, epi, has_eids):
    refs = list(refs)
    if has_eids:
        refs = refs[1:]
    n_w = len(a_of_w)
    a_refs = refs[:n_a]
    w_refs = refs[n_a:n_a + n_w]
    e_refs = refs[n_a + n_w:n_a + n_w + n_extra]
    o_refs = refs[n_a + n_w + n_extra:]
    a_vals = [a[...].astype(BF16) for a in a_refs]
    accs = []
    for ai, w_ref in zip(a_of_w, w_refs):
        w = w_ref[0] if len(w_ref.shape) == 3 else w_ref[...]
        accs.append(jnp.dot(a_vals[ai], w, preferred_element_type=F32))
    outs = epi(accs, [e[...] for e in e_refs])
    for o_ref, o in zip(o_refs, outs):
        o_ref[...] = o.astype(o_ref.dtype)


def _matmul(a_list, w_list, a_of_w, extras, epi, out_dtypes, n_cols, *, tm, tn, eids=None, name):
    m = a_list[0].shape[0]
    in_specs, blocks = [], 0
    for a in a_list:
        in_specs.append(pl.BlockSpec((tm, a.shape[1]), lambda i, j, *_: (i, 0)))
        blocks += _nbytes((tm, a.shape[1]), a.dtype)
    for w in w_list:
        if w.ndim == 2:
            in_specs.append(pl.BlockSpec((w.shape[0], tn), lambda i, j, *_: (0, j)))
        else:
            in_specs.append(pl.BlockSpec((1, w.shape[1], tn), lambda i, j, e: (e[i], 0, j)))
        blocks += _nbytes((w.shape[-2], tn), w.dtype)
    for arr, bs, im in extras:
        in_specs.append(pl.BlockSpec(bs, im))
        blocks += _nbytes(bs, arr.dtype)
    out_specs = [pl.BlockSpec((tm, tn), lambda i, j, *_: (i, j)) for _ in out_dtypes]
    out_shape = [jax.ShapeDtypeStruct((m, n_cols), dt) for dt in out_dtypes]
    blocks += sum(_nbytes((tm, tn), dt) for dt in out_dtypes) + len(w_list) * _nbytes((tm, tn), F32)
    kern = functools.partial(_mm_kernel, n_a=len(a_list), a_of_w=tuple(a_of_w), n_extra=len(extras),
                             epi=epi, has_eids=eids is not None)
    grid_spec = pltpu.PrefetchScalarGridSpec(
        num_scalar_prefetch=0 if eids is None else 1, grid=(m // tm, n_cols // tn),
        in_specs=in_specs, out_specs=out_specs)
    args = ([] if eids is None else [eids]) + list(a_list) + list(w_list) + [e[0] for e in extras]
    outs = pl.pallas_call(kern, grid_spec=grid_spec, out_shape=out_shape,
                          compiler_params=_cparams(("arbitrary", "arbitrary"), blocks), name=name)(*args)
    return outs


def _nt_dot(a, b):
    return lax.dot_general(a, b, (((1,), (1,)), ((), ())), preferred_element_type=F32)


def _lane_blocks(x):
    return [x[:, c * LANES:(c + 1) * LANES] for c in range(x.shape[1] // LANES)]


def _softmax_step(lg, off, vt, carry):
    m, l, acc = carry
    rows, width = lg.shape
    block_max = functools.reduce(jnp.maximum, _lane_blocks(lg))
    row_max = jnp.broadcast_to(jnp.max(block_max, axis=-1, keepdims=True), (rows, LANES))
    m_new = jnp.maximum(m, row_max + off)
    alpha = jnp.exp2(m - m_new)
    p = jnp.exp2(lg - jnp.tile(m_new - off, (1, width // LANES)))
    l = alpha * l + functools.reduce(jnp.add, _lane_blocks(p))
    acc = (jnp.tile(alpha, (1, acc.shape[1] // LANES)) * acc
           + jnp.dot(p.astype(BF16), vt, preferred_element_type=F32))
    return m_new, l, acc


def _softmax_init(rows, width):
    return (jnp.full((rows, LANES), NEG, F32), jnp.zeros((rows, LANES), F32), jnp.zeros((rows, width), F32))


def _softmax_finish(carry):
    _, l, acc = carry
    return acc / jnp.sum(l, axis=-1, keepdims=True)


def _rel_pos(rows, cols):
    return (lax.broadcasted_iota(I32, (rows, cols), 1) - lax.broadcasted_iota(I32, (rows, cols), 0)).astype(F32)


def _tile_loops(n_full, n_kv, past_step, edge_step, carry):
    n_wide = n_full // WIDE_TILES
    carry = lax.fori_loop(0, n_wide, lambda i, c: past_step(i * WIDE_TILES, WIDE_TILES, c), carry)
    carry = lax.fori_loop(n_wide * WIDE_TILES, n_full, lambda t, c: past_step(t, 1, c), carry)
    return lax.fori_loop(n_full, n_kv, edge_step, carry)


def _num_kv_tiles(q0, tq, tk, l_valid):
    kmax = jnp.minimum(l_valid, (((q0 + tq - 1) >> CHUNK_SHIFT) + 1) * CHUNK)
    return (kmax + tk - 1) // tk


def _dsa_kernel(qa_ref, iq_ref, iw_ref, k_ref, v_ref, ik_ref, o_ref, keys_ref, bias_ref, jlim_ref, relpos_ref, *,
                tq, tk, n_past, l_valid, topk, index_bits):
    q0 = n_past + pl.program_id(1) * tq
    n_kv = _num_kv_tiles(q0, tq, tk, l_valid)
    row = lax.broadcasted_iota(I32, (tq, tk), 0)
    col = lax.broadcasted_iota(I32, (tq, tk), 1)
    qpos = q0 + row
    lane = lax.broadcasted_iota(I32, (tq, LANES), 1)

    iw = iw_ref[0] * (I_Q ** -0.5)
    iq = iq_ref[0].astype(F32)
    iq_heads = []
    for h in range(IDX_HEADS):
        pair = iq[:, (h // 2) * LANES:(h // 2 + 1) * LANES]
        keep = (lane < IDX_DIM) if h % 2 == 0 else (lane >= IDX_DIM)
        iq_heads.append(jnp.where(keep, pair, 0.0).astype(BF16))
    iq_all = jnp.concatenate(iq_heads, axis=0)

    def to_key(x):
        bits = lax.bitcast_convert_type(x, I32)
        return bits ^ ((bits >> 31) & 0x7FFFFFFF)

    def score_tiles(t, n, carry):
        top1, top2 = carry
        start = pl.multiple_of(t * tk, tk)
        ikt = ik_ref[0, pl.ds(start, n * tk), :]
        rel = jnp.maximum(_nt_dot(iq_all, ikt), 0.0)
        score = jnp.zeros((tq, n * tk), F32)
        for h in range(IDX_HEADS):
            score = score + iw[:, h:h + 1] * rel[h * tq:(h + 1) * tq]
        for i in range(n):
            kpos = start + i * tk + col
            adm = ((kpos >> CHUNK_SHIFT) <= (qpos >> CHUNK_SHIFT)) & (kpos < l_valid)
            masked = jnp.where(adm, score[:, i * tk:(i + 1) * tk], -jnp.inf)
            keys_ref[t + i] = to_key(masked)
            for blk in _lane_blocks(masked):
                top2 = jnp.maximum(top2, jnp.minimum(top1, blk))
                top1 = jnp.maximum(top1, blk)
        return top1, top2

    n_pair = n_kv // 2
    lows = jnp.full((tq, LANES), -jnp.inf, F32)
    tops = lax.fori_loop(0, n_pair, lambda i, c: score_tiles(2 * i, 2, c), (lows, lows))
    top1, top2 = lax.fori_loop(2 * n_pair, n_kv, lambda t, c: score_tiles(t, 1, c), tops)

    def count(pred):
        def body(t, acc):
            x = jnp.where(pred(keys_ref[t], t), 1.0, 0.0)
            part = x[:, 0:LANES]
            for c in range(1, tk // LANES):
                part = part + x[:, c * LANES:(c + 1) * LANES]
            return acc + part
        acc = lax.fori_loop(0, n_kv, body, jnp.zeros((tq, LANES), F32))
        return jnp.sum(acc, axis=-1, keepdims=True)

    lo0 = to_key(jnp.min(top2, axis=-1, keepdims=True))
    hi0 = to_key(jnp.max(top1, axis=-1, keepdims=True))

    def any_open(lo, hi, cnt_lo):
        return jnp.max(jnp.where((lo != hi) & (cnt_lo != topk), 1.0, 0.0))

    def halve(state):
        lo, hi, cnt_lo, _ = state
        gap = hi - lo
        mid = lo + lax.shift_right_logical(gap, 1) + (gap & 1)
        cnt = count(lambda k, t: k >= mid)
        keep = cnt >= topk
        lo, hi, cnt_lo = jnp.where(keep, mid, lo), jnp.where(keep, hi, mid - 1), jnp.where(keep, cnt, cnt_lo)
        return lo, hi, cnt_lo, any_open(lo, hi, cnt_lo)

    unknown = jnp.full((tq, 1), -1.0, F32)
    thr = lax.while_loop(lambda state: state[3] > 0.0, halve,
                         (lo0, hi0, unknown, any_open(lo0, hi0, unknown)))[0]
    n_gt = count(lambda k, t: k > thr)
    n_ge = count(lambda k, t: k >= thr)
    want_ties = topk - n_gt
    need = ((n_ge - n_gt) > want_ties) & (thr > MASKED_KEY)
    jlim_ref[...] = jnp.full((tq, LANES), NO_LIMIT, I32)

    @pl.when(jnp.max(jnp.where(need, 1.0, 0.0)) > 0.0)
    def _():
        def index_bit(it, j_lim):
            cand = j_lim | lax.shift_left(jnp.int32(1), index_bits - 1 - it)
            below = count(lambda k, t: (k == thr) & ((t * tk + col) < cand))
            return jnp.where(below <= want_ties - 1.0, cand, j_lim)
        j_lim = lax.fori_loop(0, index_bits, index_bit, jnp.zeros((tq, 1), I32))
        jlim_ref[...] = jnp.broadcast_to(jnp.where(need, j_lim, NO_LIMIT), (tq, LANES))

    j_lim = jlim_ref[:, 0:1]

    def bias_tile(t, carry):
        k = keys_ref[t]
        kpos = t * tk + col
        sel = (k > thr) | ((k == thr) & (kpos <= j_lim))
        sel = sel & (k > MASKED_KEY)
        bias_ref[t] = jnp.where(sel, 0.0, NEG)
        return carry

    lax.fori_loop(0, n_kv, bias_tile, 0)

    qk_scale = HEAD_DIM ** -0.5 * LOG2E
    n_full = jnp.minimum(q0, l_valid) // tk

    @pl.when((pl.program_id(0) == 0) & (pl.program_id(1) == 0))
    def _():
        relpos_ref[...] = _rel_pos(tq, WIDE_TILES * tk)

    for j in range(A_KV_HEADS):
        q4 = jnp.concatenate([qa_ref[0, :, (j * A_GROUP + g) * HEAD_DIM:(j * A_GROUP + g + 1) * HEAD_DIM]
                              for g in range(A_GROUP)], axis=0)
        slopes2 = [2.0 ** -(j * A_GROUP + g + 1) * LOG2E for g in range(A_GROUP)]
        slope_rows = jnp.concatenate([jnp.full((tq, LANES), s2, F32) for s2 in slopes2], axis=0)

        def tile_operands(t, n, j=j, q4=q4):
            start = pl.multiple_of(t * tk, tk)
            kt = k_ref[0, pl.ds(start, n * tk), j * HEAD_DIM:(j + 1) * HEAD_DIM]
            vt = v_ref[0, pl.ds(start, n * tk), j * HEAD_DIM:(j + 1) * HEAD_DIM]
            bias = bias_ref[t] if n == 1 else jnp.concatenate([bias_ref[t + i] for i in range(n)], axis=1)
            return start, _nt_dot(q4, kt), vt, bias

        def past_step(t, n, carry, slopes2=slopes2, slope_rows=slope_rows):
            start, s, vt, bias = tile_operands(t, n)
            lg = jnp.concatenate([s[g * tq:(g + 1) * tq] * qk_scale + (relpos_ref[:, :n * tk] * slopes2[g] + bias)
                                  for g in range(A_GROUP)], axis=0)
            off = slope_rows * (start - q0).astype(F32)
            return _softmax_step(lg, off, vt, carry)

        def edge_step(t, carry, slopes2=slopes2):
            start, s, vt, bias = tile_operands(t, 1)
            dist = jnp.abs(qpos - (start + col)).astype(F32)
            lg = jnp.concatenate([s[g * tq:(g + 1) * tq] * qk_scale - slopes2[g] * dist + bias
                                  for g in range(A_GROUP)], axis=0)
            return _softmax_step(lg, 0.0, vt, carry)

        out = _softmax_finish(_tile_loops(n_full, n_kv, past_step, edge_step,
                                          _softmax_init(A_GROUP * tq, HEAD_DIM)))
        for g in range(A_GROUP):
            h = j * A_GROUP + g
            o_ref[0, :, h * HEAD_DIM:(h + 1) * HEAD_DIM] = out[g * tq:(g + 1) * tq].astype(o_ref.dtype)


def _dsa_attention(zb, zf, keys, off, *, n_past, l_valid, tq, tk, topk, name):
    b, s, _ = zb.shape
    karr, kcol, varr, vcol, ikarr, ikcol = keys
    lp = karr.shape[1]
    n_tiles = lp // tk
    kern = functools.partial(_dsa_kernel, tq=tq, tk=tk, n_past=n_past, l_valid=l_valid, topk=topk,
                             index_bits=lp.bit_length())
    blocks = (_nbytes((tq, A_Q + I_Q), BF16) + _nbytes((tq, LANES), F32) + 2 * _nbytes((lp, A_KV), BF16)
              + _nbytes((lp, LANES), BF16) + _nbytes((tq, A_Q), BF16) + _nbytes((tq, lp), F32)
              + 8 * _nbytes((A_GROUP * tq, tk), F32))
    return pl.pallas_call(
        kern,
        grid=(b, s // tq),
        in_specs=[pl.BlockSpec((1, tq, A_Q), lambda bi, qi: (bi, qi, off['aq'] // A_Q)),
                  pl.BlockSpec((1, tq, I_Q), lambda bi, qi: (bi, qi, off['iq'] // I_Q)),
                  pl.BlockSpec((1, tq, LANES), lambda bi, qi: (bi, qi, off['iw'] // LANES)),
                  pl.BlockSpec((1, lp, A_KV), lambda bi, qi: (bi, 0, kcol)),
                  pl.BlockSpec((1, lp, A_KV), lambda bi, qi: (bi, 0, vcol)),
                  pl.BlockSpec((1, lp, LANES), lambda bi, qi: (bi, 0, ikcol))],
        out_specs=pl.BlockSpec((1, tq, A_Q), lambda bi, qi: (bi, qi, 0)),
        out_shape=jax.ShapeDtypeStruct((b, s, A_Q), BF16),
        scratch_shapes=[pltpu.VMEM((n_tiles, tq, tk), I32), pltpu.VMEM((n_tiles, tq, tk), F32),
                        pltpu.VMEM((tq, LANES), I32), pltpu.VMEM((tq, WIDE_TILES * tk), F32)],
        compiler_params=_cparams(("arbitrary", "arbitrary"), blocks),
        name=name,
    )(zb, zb, zf, karr, varr, ikarr)


def _diff_kernel(q_ref, k_ref, v_ref, lq_ref, g_ref, o_ref, relbias_ref, *, tq, tk, n_past, l_valid, lam_init):
    h = pl.program_id(1)
    q0 = n_past + pl.program_id(2) * tq
    n_kv = _num_kv_tiles(q0, tq, tk, l_valid)
    row = lax.broadcasted_iota(I32, (tq, tk), 0)
    col = lax.broadcasted_iota(I32, (tq, tk), 1)
    qpos = q0 + row
    slope2 = jnp.float32(1.0)
    for hh in range(B_HEADS):
        slope2 = jnp.where(h == hh, jnp.float32(2.0 ** (-8.0 * (hh + 1) / B_HEADS) * LOG2E), slope2)
    qk_scale = HEAD_DIM ** -0.5 * LOG2E
    n_full = jnp.minimum(q0, l_valid) // tk

    @pl.when(pl.program_id(2) == 0)
    def _():
        relbias_ref[...] = _rel_pos(tq, WIDE_TILES * tk) * slope2

    qc = [q_ref[0, :, c * HEAD_DIM:(c + 1) * HEAD_DIM] for c in range(2)]

    def tile_operands(t, n):
        start = pl.multiple_of(t * tk, tk)
        s = [_nt_dot(qc[c], k_ref[0, pl.ds(start, n * tk), c * HEAD_DIM:(c + 1) * HEAD_DIM]) for c in range(2)]
        return start, s, v_ref[0, pl.ds(start, n * tk), :]

    def past_step(t, n, carry):
        start, s, vt = tile_operands(t, n)
        lg = jnp.concatenate([s[c] * qk_scale + relbias_ref[:, :n * tk] for c in range(2)], axis=0)
        return _softmax_step(lg, slope2 * (start - q0).astype(F32), vt, carry)

    def edge_step(t, carry):
        start, s, vt = tile_operands(t, 1)
        kpos = start + col
        adm = ((kpos >> CHUNK_SHIFT) <= (qpos >> CHUNK_SHIFT)) & (kpos < l_valid)
        bias = jnp.where(adm, -slope2 * jnp.abs(qpos - kpos).astype(F32), NEG)
        lg = jnp.concatenate([s[c] * qk_scale + bias for c in range(2)], axis=0)
        return _softmax_step(lg, 0.0, vt, carry)

    out = _softmax_finish(_tile_loops(n_full, n_kv, past_step, edge_step,
                                      _softmax_init(2 * tq, 2 * HEAD_DIM)))
    lq = lq_ref[...]
    lam = (jnp.exp(jnp.sum(lq[0:1] * lq[1:2], axis=-1, keepdims=True))
           - jnp.exp(jnp.sum(lq[2:3] * lq[3:4], axis=-1, keepdims=True)) + lam_init)
    o = out[:tq] - lam * out[tq:]
    y = o * lax.rsqrt(jnp.mean(o * o, axis=-1, keepdims=True) + SUBLN_EPS)
    y = (y * g_ref[...]) * (1.0 - lam_init)
    o_ref[0] = y.astype(o_ref.dtype)


def _diff_attention(zb, keys, lam_qk_l, g_subln_l, off, *, n_past, l_valid, tq, tk, lam_init, name):
    b, s, _ = zb.shape
    karr, kcol0, varr, vcol0 = keys
    lp = karr.shape[1]
    w = 2 * HEAD_DIM
    kern = functools.partial(_diff_kernel, tq=tq, tk=tk, n_past=n_past, l_valid=l_valid, lam_init=lam_init)
    blocks = (2 * _nbytes((tq, w), BF16) + 2 * _nbytes((lp, w), BF16) + 8 * _nbytes((2 * tq, tk), F32))
    return pl.pallas_call(
        kern,
        grid=(b, B_HEADS, s // tq),
        in_specs=[pl.BlockSpec((1, tq, w), lambda bi, h, qi: (bi, qi, off['bq'] // w + h)),
                  pl.BlockSpec((1, lp, w), lambda bi, h, qi: (bi, 0, kcol0 + h)),
                  pl.BlockSpec((1, lp, w), lambda bi, h, qi: (bi, 0, vcol0 + h)),
                  pl.BlockSpec((4, HEAD_DIM), lambda bi, h, qi: (0, 0)),
                  pl.BlockSpec((1, w), lambda bi, h, qi: (0, 0))],
        out_specs=pl.BlockSpec((1, tq, w), lambda bi, h, qi: (bi, qi, h)),
        out_shape=jax.ShapeDtypeStruct((b, s, B_V), BF16),
        scratch_shapes=[pltpu.VMEM((tq, WIDE_TILES * tk), F32)],
        compiler_params=_cparams(("arbitrary", "arbitrary", "arbitrary"), blocks),
        name=name,
    )(zb, karr, varr, lam_qk_l, g_subln_l.reshape(1, w))


def _row_gather(src_ref, idx_ref, idx_base, idx_stride, dst_ref, sem, n_rows):
    def row_copy(r):
        return pltpu.make_async_copy(src_ref.at[pl.ds(idx_ref[idx_base + r * idx_stride], 1)],
                                     dst_ref.at[pl.ds(r, 1)], sem)

    def start():
        lax.fori_loop(0, n_rows, lambda r, c: (row_copy(r).start(), c)[1], 0, unroll=ISSUE_UNROLL)

    def wait():
        pltpu.make_async_copy(dst_ref.at[pl.ds(0, n_rows)], dst_ref.at[pl.ds(0, n_rows)], sem).wait()

    return start, wait


def _moe_up_kernel(eid_ref, tok_ref, h_ref, wg_ref, wu_ref, o_ref, rows_ref, rows16_ref, sems, *, blk, n_blocks):
    i, j = pl.program_id(0), pl.program_id(1)
    slot = i % 2

    def gather(block, buf):
        return _row_gather(h_ref, tok_ref, block * blk, 1, rows_ref.at[buf], sems.at[buf], blk)

    @pl.when((i == 0) & (j == 0))
    def _():
        gather(0, 0)[0]()

    @pl.when(j == 0)
    def _():
        gather(i, slot)[1]()

        @pl.when(i + 1 < n_blocks)
        def _():
            gather(i + 1, 1 - slot)[0]()

        rows16_ref[...] = rows_ref[slot].astype(BF16)

    a = rows16_ref[...]
    u = jnp.dot(a, wg_ref[0], preferred_element_type=F32)
    v = jnp.dot(a, wu_ref[0], preferred_element_type=F32)
    o_ref[...] = ((u * jax.nn.sigmoid(u)) * v).astype(o_ref.dtype)


def _moe_up(h2, tok_buf, block_expert, w_gate, w_up, *, blk, name):
    p = tok_buf.shape[0]
    d = h2.shape[1]
    fp = w_gate.shape[-1]
    tn = _tile(fp, 512)
    n_blocks = p // blk
    blocks = (_nbytes((blk, d), F32) + _nbytes((blk, d), BF16) // 2 + 2 * _nbytes((d, tn), BF16)
              + _nbytes((blk, tn), BF16) + _nbytes((blk, tn), F32))
    return pl.pallas_call(
        functools.partial(_moe_up_kernel, blk=blk, n_blocks=n_blocks),
        grid_spec=pltpu.PrefetchScalarGridSpec(
            num_scalar_prefetch=2, grid=(n_blocks, fp // tn),
            in_specs=[pl.BlockSpec(memory_space=pl.ANY),
                      pl.BlockSpec((1, d, tn), lambda i, j, e, tok: (e[i], 0, j)),
                      pl.BlockSpec((1, d, tn), lambda i, j, e, tok: (e[i], 0, j))],
            out_specs=pl.BlockSpec((blk, tn), lambda i, j, e, tok: (i, j)),
            scratch_shapes=[pltpu.VMEM((2, blk, d), F32), pltpu.VMEM((blk, d), BF16),
                            pltpu.SemaphoreType.DMA((2,))]),
        out_shape=jax.ShapeDtypeStruct((p, fp), BF16),
        compiler_params=_cparams(("arbitrary", "arbitrary"), blocks),
        name=name,
    )(block_expert, tok_buf, h2, w_gate, w_up)


def _combine_kernel(dest_ref, x_ref, y_ref, rg_ref, gate_ref, o_ref, picked_ref, sems, *, tm, n_steps):
    i = pl.program_id(0)
    slot = i % 2

    def fetch(step, buf):
        base = step * tm * TOP_K_EXPERTS
        return [_row_gather(y_ref, dest_ref, base + k, TOP_K_EXPERTS, picked_ref.at[buf, k], sems.at[buf, k], tm)
                for k in range(TOP_K_EXPERTS)]

    @pl.when(i == 0)
    def _():
        for start, _ in fetch(0, 0):
            start()

    for _, wait in fetch(i, slot):
        wait()

    @pl.when(i + 1 < n_steps)
    def _():
        for start, _ in fetch(i + 1, 1 - slot):
            start()

    rg = rg_ref[...]
    f = rg[:, 0:1] * picked_ref[slot, 0] + rg[:, 1:2] * picked_ref[slot, 1]
    o_ref[...] = x_ref[...] + gate_ref[0] * f


def _moe_combine(x2, yb, dest, rgate, gate_f, seq, *, name):
    t, d = x2.shape
    tm = _tile(seq, 256)
    per_b = seq // tm
    return pl.pallas_call(
        functools.partial(_combine_kernel, tm=tm, n_steps=t // tm),
        grid_spec=pltpu.PrefetchScalarGridSpec(
            num_scalar_prefetch=1, grid=(t // tm,),
            in_specs=[pl.BlockSpec((tm, d), lambda i, dst: (i, 0)),
                      pl.BlockSpec(memory_space=pl.ANY),
                      pl.BlockSpec((tm, LANES), lambda i, dst: (i, 0)),
                      pl.BlockSpec((1, 1, d), lambda i, dst: (i // per_b, 0, 0))],
            out_specs=pl.BlockSpec((tm, d), lambda i, dst: (i, 0)),
            scratch_shapes=[pltpu.VMEM((2, TOP_K_EXPERTS, tm, d), F32),
                            pltpu.SemaphoreType.DMA((2, TOP_K_EXPERTS))]),
        out_shape=jax.ShapeDtypeStruct((t, d), F32),
        compiler_params=_cparams(("arbitrary",), 4 * _nbytes((tm, d), F32)),
        name=name,
    )(dest, x2, yb, rgate, gate_f)


STATE_COLS = (('ak', A_KV), ('av', A_KV), ('ik', LANES), ('bk', B_QK), ('bv', B_V))


def _state_tails():
    return ((A_KV_HEADS, HEAD_DIM), (A_KV_HEADS, HEAD_DIM), (IDX_DIM,), (B_HEADS, 2, HEAD_DIM),
            (B_HEADS, 2 * HEAD_DIM))


def _state_kernel(*refs, depth):
    n = len(STATE_COLS)
    ins, outs = refs[:depth * n], refs[depth * n:]
    for l in range(depth):
        @pl.when(pl.program_id(0) == l)
        def _(l=l):
            ak, av, ik, bk, bv = ins[l * n:(l + 1) * n]
            for j in range(A_KV_HEADS):
                outs[0][0, 0, :, j, :] = ak[0, :, j * HEAD_DIM:(j + 1) * HEAD_DIM]
                outs[1][0, 0, :, j, :] = av[0, :, j * HEAD_DIM:(j + 1) * HEAD_DIM]
            outs[2][0, 0] = ik[0, :, :IDX_DIM]
            for h in range(B_HEADS):
                for c in range(2):
                    lo = (2 * h + c) * HEAD_DIM
                    outs[3][0, 0, :, h, c, :] = bk[0, :, lo:lo + HEAD_DIM]
                outs[4][0, 0, :, h, :] = bv[0, :, 2 * h * HEAD_DIM:2 * (h + 1) * HEAD_DIM]


def _state_rows(zf3_layers, off, *, name):
    depth = len(zf3_layers)
    b, s, _ = zf3_layers[0].shape
    ts = _tile(s, 256)
    in_specs, args = [], []
    for l in range(depth):
        for cname, width in STATE_COLS:
            blk = off[cname] // width
            in_specs.append(pl.BlockSpec(
                (1, ts, width),
                lambda li, bi, i, l=l, blk=blk: (jnp.where(li == l, bi, 0), jnp.where(li == l, i, 0), blk)))
            args.append(zf3_layers[l])
    tails = _state_tails()
    out_specs = [pl.BlockSpec((1, 1, ts) + t, lambda li, bi, i, nt=len(t): (li, bi, i) + (0,) * nt) for t in tails]
    out_shape = [jax.ShapeDtypeStruct((depth, b, s) + t, F32) for t in tails]
    blocks = (depth + 1) * sum(_nbytes((ts, w), F32) for _, w in STATE_COLS)
    return pl.pallas_call(
        functools.partial(_state_kernel, depth=depth),
        grid=(depth, b, s // ts), in_specs=in_specs, out_specs=out_specs, out_shape=out_shape,
        compiler_params=_cparams(("arbitrary", "arbitrary", "arbitrary"), blocks), name=name,
    )(*args)


def _pack_kernel(c_ref, new_ref, o_ref, *, n_cached_tiles, tail):
    i = pl.program_id(1)
    piece = tail[-1]

    @pl.when(i < n_cached_tiles)
    def _():
        for n, idx in enumerate(itertools.product(*[range(t) for t in tail[:-1]])):
            o_ref[0, :, n * piece:(n + 1) * piece] = c_ref[(0, 0, slice(None)) + idx + (slice(None),)].astype(BF16)

    @pl.when(i >= n_cached_tiles)
    def _():
        o_ref[0] = new_ref[0]


def _pack_keys(cache, layer, new_rows, *, tk, name):
    _, b, p = cache.shape[:3]
    tail = cache.shape[3:]
    w = math.prod(tail)
    assert p % tk == 0 and new_rows.shape == (b, tk, w)
    n_cached_tiles = p // tk
    zeros = (0,) * len(tail)
    return pl.pallas_call(
        functools.partial(_pack_kernel, n_cached_tiles=n_cached_tiles, tail=tail),
        grid=(b, n_cached_tiles + 1),
        in_specs=[pl.BlockSpec((1, 1, tk) + tail,
                               lambda bi, i: (layer, bi, jnp.minimum(i, n_cached_tiles - 1)) + zeros),
                  pl.BlockSpec((1, tk, w), lambda bi, i: (bi, 0, 0))],
        out_specs=pl.BlockSpec((1, tk, w), lambda bi, i: (bi, i, 0)),
        out_shape=jax.ShapeDtypeStruct((b, p + tk, w), BF16),
        compiler_params=_cparams(("arbitrary", "arbitrary"), _nbytes((tk, w), F32) + 2 * _nbytes((tk, w), BF16)),
        name=name,
    )(cache, new_rows)


def _silu_mul(accs, extras):
    u, v = accs
    return [(u * jax.nn.sigmoid(u)) * v]


def _moe_ffn(x2, h2, ridx, rgate, gate_f, seq, w_gate, w_up, w_down, *, tag):
    t, d = x2.shape
    fp = w_gate.shape[-1]
    a = t * TOP_K_EXPERTS
    blk = _tile(a, 512)
    n_blocks = -(-(a + N_EXPERTS * (blk - 1)) // blk)
    p = n_blocks * blk
    e_flat = ridx[:, :TOP_K_EXPERTS].reshape(-1)
    onehot = (e_flat[:, None] == jnp.arange(N_EXPERTS, dtype=I32)[None, :]).astype(I32)
    rank = jnp.sum((jnp.cumsum(onehot, axis=0) - onehot) * onehot, axis=1)
    counts = jnp.sum(onehot, axis=0)
    padded = ((counts + blk - 1) // blk) * blk
    pends = jnp.cumsum(padded)
    pstarts = pends - padded
    dest = (pstarts[e_flat] + rank).astype(I32)
    tok_buf = jnp.zeros((p,), I32).at[dest].set(jnp.arange(a, dtype=I32) // TOP_K_EXPERTS)
    block_expert = jnp.clip(jnp.searchsorted(pends, jnp.arange(n_blocks, dtype=I32) * blk, side='right'),
                            0, N_EXPERTS - 1).astype(I32)

    act = _moe_up(h2, tok_buf, block_expert, w_gate, w_up, blk=blk, name=f"moe_up_{tag}")
    (yb,) = _matmul([act], [w_down], [0], [], lambda accs, ex: accs, [F32], d, tm=blk, tn=_tile(d, 512),
                    eids=block_expert, name=f"moe_down_{tag}")
    return _moe_combine(x2, yb, dest, rgate, gate_f, seq, name=f"moe_combine_{tag}")


def _trunk(x, mods, past_all, wts, tag):
    b, s, d = x.shape
    t = b * s
    depth = len(wts['w_in'])
    off, nz = _z_layout(d)
    x2 = x.reshape(t, d)
    tn_d = _tile(d, 512)
    tm_b = _tile(s, 512)
    tm_f = _tile(t, 1024)
    per_b = s // tm_b
    zf3_layers = []
    for l in range(depth):
        shift_a, scale_a, gate_a, shift_f, scale_f, gate_f = [
            mods[l][:, i * d:(i + 1) * d].reshape(b, 1, d) for i in range(N_ADA)]
        batch_tile = lambda i, j, *_: (i // per_b, 0, j)

        h = _norm(x2, wts['g_attn'][l], s, scale=scale_a, shift=shift_a, out_dtype=BF16,
                  name=f"norm_attn_{tag}{l}")
        zf, zb = _matmul([h], [wts['w_in'][l]], [0], [], lambda accs, ex: [accs[0], accs[0]], [F32, BF16], nz,
                         tm=tm_f, tn=IN_PROJ_TN, name=f"in_proj_{tag}{l}")
        zf3, zb3 = zf.reshape(b, s, nz), zb.reshape(b, s, nz)
        cut = lambda z, name, width: z[:, :, off[name]:off[name] + width]
        zf3_layers.append(zf3)

        if past_all is None:
            n_past, l_valid = 0, s
            tk = _tile(s, 512)
            tq_a, tq_b = _tile(s, 128), _tile(s, 256)
            dsa_keys = (zb3, off['ak'] // A_KV, zb3, off['av'] // A_KV, zb3, off['ik'] // LANES)
            diff_keys = (zb3, off['bk'] // (2 * HEAD_DIM), zb3, off['bv'] // (2 * HEAD_DIM))
        else:
            n_past = past_all[0].shape[2]
            l_valid = n_past + s
            tk = 512
            tq_a = tq_b = s

            def new_rows(name, width):
                return jnp.concatenate([cut(zb3, name, width), jnp.zeros((b, tk - s, width), BF16)], axis=1)

            def with_past(ci, name, width):
                return _pack_keys(past_all[ci], l, new_rows(name, width), tk=tk, name=f"pack_{name}_{tag}{l}")

            ik_past = past_all[2][l].astype(BF16)
            ik_all = jnp.concatenate([jnp.concatenate([ik_past, ik_past], axis=-1), new_rows('ik', LANES)], axis=1)
            dsa_keys = (with_past(0, 'ak', A_KV), 0, with_past(1, 'av', A_KV), 0, ik_all, 0)
            diff_keys = (with_past(3, 'bk', B_QK), 0, with_past(4, 'bv', B_V), 0)
        topk = min(TOPK_MAX, l_valid // 4)
        oa = _dsa_attention(zb3, zf3, dsa_keys, off, n_past=n_past, l_valid=l_valid, tq=tq_a, tk=tk,
                            topk=topk, name=f"dsa_{tag}{l}")
        lam_init = 0.8 - 0.6 * math.exp(-0.3 * l)
        ob = _diff_attention(zb3, diff_keys, wts['lam_qk'][l], wts['g_subln'][l], off, n_past=n_past,
                             l_valid=l_valid, tq=tq_b, tk=tk, lam_init=lam_init, name=f"diff_{tag}{l}")

        def merge_epi(accs, ex):
            return [jax.nn.sigmoid(ex[0]) * accs[0] + jax.nn.sigmoid(ex[1]) * accs[1]]

        (merged,) = _matmul(
            [oa.reshape(t, A_Q), ob.reshape(t, B_V)], [wts['w_out_a'][l], wts['w_out_b'][l]], [0, 1],
            [(zf, (tm_f, tn_d), lambda i, j, *_: (i, off['ga'] // tn_d + j)),
             (zf, (tm_f, tn_d), lambda i, j, *_: (i, off['gb'] // tn_d + j))],
            merge_epi, [BF16], d, tm=tm_f, tn=tn_d, name=f"merge_{tag}{l}")

        def resid_epi(accs, ex):
            return [ex[0] + ex[1][0] * accs[0]]

        (x2,) = _matmul([merged], [wts['w_out'][l]], [0],
                        [(x2, (tm_b, tn_d), lambda i, j, *_: (i, j)), (gate_a, (1, 1, tn_d), batch_tile)],
                        resid_epi, [F32], d, tm=tm_b, tn=tn_d, name=f"out_proj_{tag}{l}")

        if l % 2 == 0:
            i = l // 2
            h = _norm(x2, wts['g_ffn'][l], s, scale=scale_f, shift=shift_f, out_dtype=BF16,
                      name=f"norm_ffn_{tag}{l}")
            fp = wts['w_ff_gate'][i].shape[-1]
            (act,) = _matmul([h], [wts['w_ff_gate'][i], wts['w_ff_up'][i]], [0, 0], [], _silu_mul, [BF16], fp,
                             tm=tm_f, tn=_tile(fp, 512), name=f"ffn_up_{tag}{l}")
            (x2,) = _matmul([act], [wts['w_ff_down'][i]], [0],
                            [(x2, (tm_b, tn_d), lambda i, j, *_: (i, j)), (gate_f, (1, 1, tn_d), batch_tile)],
                            resid_epi, [F32], d, tm=tm_b, tn=tn_d, name=f"ffn_down_{tag}{l}")
        else:
            i = l // 2
            h2, ridx, rgate = _norm(x2, wts['g_ffn'][l], s, scale=scale_f, shift=shift_f,
                                    router=(wts['w_router'][i], wts['b_router'][i]), out_dtype=F32,
                                    name=f"norm_router_{tag}{l}")
            x2 = _moe_ffn(x2, h2, ridx, rgate, gate_f, s, wts['w_moe_gate'][i], wts['w_moe_up'][i],
                          wts['w_moe_down'][i], tag=f"{tag}{l}")
    y = _norm(x2, wts['g_final'], s, out_dtype=F32, name=f"norm_final_{tag}")
    state = tuple(_state_rows(zf3_layers, off, name=f"state_rows_{tag}"))
    return y.reshape(b, s, d), state


def _prep_weights(w_in, w_out_a, w_out_b, w_out, w_ff_gate, w_ff_up, w_ff_down, w_router, b_router,
                  w_moe_gate, w_moe_up, w_moe_down, d_model):
    f = w_ff_gate.shape[-1]
    fp = _round_up(f, 512)
    pad_cols = lambda w: jnp.pad(w, [(0, 0)] * (w.ndim - 1) + [(0, fp - f)]).astype(BF16)
    pad_rows = lambda w: jnp.pad(w, [(0, 0)] * (w.ndim - 2) + [(0, fp - f), (0, 0)]).astype(BF16)
    n_moe = w_router.shape[0]
    return dict(
        w_in=[_prep_w_in(w_in[l], d_model) for l in range(w_in.shape[0])],
        w_out_a=w_out_a.astype(BF16), w_out_b=w_out_b.astype(BF16), w_out=w_out.astype(BF16),
        w_ff_gate=pad_cols(w_ff_gate), w_ff_up=pad_cols(w_ff_up), w_ff_down=pad_rows(w_ff_down),
        w_router=[jnp.pad(w_router[i], ((0, 0), (0, LANES - N_EXPERTS))) for i in range(n_moe)],
        b_router=[jnp.pad(b_router[i], (0, LANES - N_EXPERTS)).reshape(1, LANES) for i in range(n_moe)],
        w_moe_gate=pad_cols(w_moe_gate), w_moe_up=pad_cols(w_moe_up), w_moe_down=pad_rows(w_moe_down),
    )


def kernel(x_prompt, x_sample, c_prompt, c_sample, cache_dsa_k, cache_dsa_v, cache_idx_k, cache_diff_k, cache_diff_v, w_ada, b_ada, g_attn, w_in, w_out_a, w_out_b, w_out, lam_qk, g_subln, g_ffn, w_ff_gate, w_ff_up, w_ff_down, w_router, b_router, w_moe_gate, w_moe_up, w_moe_down, g_final):
    d = x_prompt.shape[-1]
    wts = _prep_weights(w_in, w_out_a, w_out_b, w_out, w_ff_gate, w_ff_up, w_ff_down, w_router, b_router,
                        w_moe_gate, w_moe_up, w_moe_down, d)
    wts.update(g_attn=g_attn, g_ffn=g_ffn, g_final=g_final, lam_qk=lam_qk, g_subln=g_subln)
    nb_p = c_prompt.shape[0]
    mods = _ada_mod(jnp.concatenate([c_prompt, c_sample], axis=0), w_ada, b_ada)
    y_p, st_p = _trunk(x_prompt, mods[:, :nb_p], None, wts, "p")
    past_all = (cache_dsa_k, cache_dsa_v, cache_idx_k, cache_diff_k, cache_diff_v)
    y_s, st_s = _trunk(x_sample, mods[:, nb_p:], past_all, wts, "s")
    return (y_p, y_s) + st_p + st_s
```

```python
import functools
import itertools
import math

import jax
import jax.numpy as jnp
from jax import lax
from jax.experimental import pallas as pl
from jax.experimental.pallas import tpu as pltpu

F32, BF16, I32 = jnp.float32, jnp.bfloat16, jnp.int32

CHUNK = 64
CHUNK_SHIFT = CHUNK.bit_length() - 1
HEAD_DIM = 128
A_HEADS = 8
A_KV_HEADS = 2
A_GROUP = A_HEADS // A_KV_HEADS
IDX_HEADS = 8
IDX_DIM = 64
TOPK_MAX = 256
B_HEADS = 4
N_EXPERTS = 8
TOP_K_EXPERTS = 2
N_ADA = 6
NORM_EPS = 1e-6
SUBLN_EPS = 1e-5
A_Q = A_HEADS * HEAD_DIM
A_KV = A_KV_HEADS * HEAD_DIM
I_Q = IDX_HEADS * IDX_DIM
B_QK = B_HEADS * 2 * HEAD_DIM
B_V = B_HEADS * 2 * HEAD_DIM

LANES = 128
VMEM_BYTES_V7X = 64 * 2 ** 20
NEG = -1e30
NO_LIMIT = 2 ** 30
LOG2E = math.log2(math.e)
IN_PROJ_TN = 512
MASKED_KEY = 0x807FFFFF - 2 ** 32
WIDE_TILES = 4
ISSUE_UNROLL = 8
NARROW_BITS = 12


def _cparams(dims, block_bytes):
    limit = min(max(2 * int(block_bytes) + (8 << 20), 32 << 20), VMEM_BYTES_V7X - (6 << 20))
    return pltpu.CompilerParams(dimension_semantics=dims, vmem_limit_bytes=limit)


def _nbytes(shape, dtype):
    return math.prod(shape) * jnp.dtype(dtype).itemsize


def _tile(n, pref):
    if n <= pref:
        return n
    t = pref
    while n % t:
        t //= 2
    return t


def _round_up(n, m):
    return -(-n // m) * m


def _z_layout(d_model):
    off, o = {}, 0
    for name, n in (('aq', A_Q), ('ak', A_KV), ('av', A_KV), ('iq', I_Q), ('bq', B_QK), ('bk', B_QK),
                    ('bv', B_V), ('ga', d_model), ('gb', d_model), ('ik', LANES), ('iw', LANES)):
        off[name] = o
        o += n
    return off, _round_up(o, IN_PROJ_TN)


def _prep_w_in(w, d_model):
    sizes = (A_Q, A_KV, A_KV, I_Q, IDX_DIM, IDX_HEADS, B_QK, B_QK, B_V, d_model, d_model)
    names = ('aq', 'ak', 'av', 'iq', 'ik', 'iw', 'bq', 'bk', 'bv', 'ga', 'gb')
    parts, o = {}, 0
    for n, s in zip(names, sizes):
        parts[n] = w[:, o:o + s]
        o += s
    cols = [parts[n] for n in ('aq', 'ak', 'av', 'iq', 'bq', 'bk', 'bv', 'ga', 'gb')]
    cols += [parts['ik'], parts['ik'], parts['iw']]
    used = sum(c.shape[1] for c in cols)
    cols.append(jnp.zeros((w.shape[0], _z_layout(d_model)[1] - used), w.dtype))
    return jnp.concatenate(cols, axis=1).astype(BF16)


def _ada_kernel(c_ref, w_ref, b_ref, o_ref):
    c = c_ref[...]
    s = c * jax.nn.sigmoid(c)
    o_ref[0] = jnp.dot(s.astype(BF16), w_ref[0].astype(BF16), preferred_element_type=F32) + b_ref[0]


def _ada_mod(c_all, w_ada, b_ada):
    depth, d, n = w_ada.shape
    nb = c_all.shape[0]
    tn = _tile(n, 1024)
    blocks = _nbytes((nb, d), F32) + _nbytes((d, tn), F32) * 2 + _nbytes((nb, tn), F32)
    return pl.pallas_call(
        _ada_kernel,
        grid=(depth, n // tn),
        in_specs=[pl.BlockSpec((nb, d), lambda l, j: (0, 0)),
                  pl.BlockSpec((1, d, tn), lambda l, j: (l, 0, j)),
                  pl.BlockSpec((1, 1, tn), lambda l, j: (l, 0, j))],
        out_specs=pl.BlockSpec((1, nb, tn), lambda l, j: (l, 0, j)),
        out_shape=jax.ShapeDtypeStruct((depth, nb, n), F32),
        compiler_params=_cparams(("arbitrary", "arbitrary"), blocks),
        name="ada_mod",
    )(c_all, w_ada, b_ada.reshape(depth, 1, n))


def _norm_kernel(*refs, modulated, router, eps):
    refs = list(refs)
    x_ref, g_ref = refs[:2]
    pos = 2
    x = x_ref[...]
    y = x * lax.rsqrt(jnp.mean(x * x, axis=-1, keepdims=True) + eps)
    y = y * g_ref[...]
    if modulated:
        scale_ref, shift_ref = refs[pos:pos + 2]
        pos += 2
        y = y * (1.0 + scale_ref[0]) + shift_ref[0]
    if router:
        wr_ref, br_ref = refs[pos:pos + 2]
        pos += 2
    o_ref = refs[pos]
    o_ref[...] = y.astype(o_ref.dtype)
    if router:
        idx_ref, gate_ref = refs[pos + 1:pos + 3]
        logits = jnp.dot(y, wr_ref[...], preferred_element_type=F32,
                         precision=lax.Precision.HIGHEST) + br_ref[...]
        lane = lax.broadcasted_iota(I32, logits.shape, 1)
        logits = jnp.where(lane < N_EXPERTS, logits, -jnp.inf)
        m1 = jnp.max(logits, axis=-1, keepdims=True)
        i1 = jnp.min(jnp.where(logits == m1, lane, LANES), axis=-1, keepdims=True)
        rest = jnp.where(lane == i1, -jnp.inf, logits)
        m2 = jnp.max(rest, axis=-1, keepdims=True)
        i2 = jnp.min(jnp.where(rest == m2, lane, LANES), axis=-1, keepdims=True)
        e = jnp.exp(m2 - m1)
        g1 = 1.0 / (1.0 + e)
        g2 = e / (1.0 + e)
        idx_ref[...] = jnp.where(lane == 0, i1, jnp.where(lane == 1, i2, 0))
        gate_ref[...] = jnp.where(lane == 0, g1, jnp.where(lane == 1, g2, 0.0))


def _norm(x2, g, seq, *, scale=None, shift=None, router=None, out_dtype, eps=NORM_EPS, name):
    t, d = x2.shape
    tm = _tile(seq, 256)
    per_b = seq // tm
    in_specs = [pl.BlockSpec((tm, d), lambda i: (i, 0)), pl.BlockSpec((1, d), lambda i: (0, 0))]
    args = [x2, g.reshape(1, d)]
    if scale is not None:
        in_specs += [pl.BlockSpec((1, 1, d), lambda i: (i // per_b, 0, 0))] * 2
        args += [scale, shift]
    out_specs = [pl.BlockSpec((tm, d), lambda i: (i, 0))]
    out_shape = [jax.ShapeDtypeStruct((t, d), out_dtype)]
    if router is not None:
        w_r, b_r = router
        in_specs += [pl.BlockSpec((d, LANES), lambda i: (0, 0)), pl.BlockSpec((1, LANES), lambda i: (0, 0))]
        args += [w_r, b_r]
        out_specs += [pl.BlockSpec((tm, LANES), lambda i: (i, 0))] * 2
        out_shape += [jax.ShapeDtypeStruct((t, LANES), I32), jax.ShapeDtypeStruct((t, LANES), F32)]
    blocks = 3 * _nbytes((tm, d), F32) + _nbytes((d, LANES), F32)
    outs = pl.pallas_call(
        functools.partial(_norm_kernel, modulated=scale is not None, router=router is not None, eps=eps),
        grid=(t // tm,), in_specs=in_specs, out_specs=out_specs, out_shape=out_shape,
        compiler_params=_cparams(("arbitrary",), blocks), name=name,
    )(*args)
    return outs if router is not None else outs[0]


def _mm_kernel(*refs, n_a, a_of_w, n_extra, epi, has_eids):
    refs = list(refs)
    if has_eids:
        refs = refs[1:]
    n_w = len(a_of_w)
    a_refs = refs[:n_a]
    w_refs = refs[n_a:n_a + n_w]
    e_refs = refs[n_a + n_w:n_a + n_w + n_extra]
    o_refs = refs[n_a + n_w + n_extra:]
    a_vals = [a[...].astype(BF16) for a in a_refs]
    accs = []
    for ai, w_ref in zip(a_of_w, w_refs):
        w = w_ref[0] if len(w_ref.shape) == 3 else w_ref[...]
        accs.append(jnp.dot(a_vals[ai], w, preferred_element_type=F32))
    outs = epi(accs, [e[...] for e in e_refs])
    for o_ref, o in zip(o_refs, outs):
        o_ref[...] = o.astype(o_ref.dtype)


def _matmul(a_list, w_list, a_of_w, extras, epi, out_dtypes, n_cols, *, tm, tn, eids=None, name):
    m = a_list[0].shape[0]
    in_specs, blocks = [], 0
    for a in a_list:
        in_specs.append(pl.BlockSpec((tm, a.shape[1]), lambda i, j, *_: (i, 0)))
        blocks += _nbytes((tm, a.shape[1]), a.dtype)
    for w in w_list:
        if w.ndim == 2:
            in_specs.append(pl.BlockSpec((w.shape[0], tn), lambda i, j, *_: (0, j)))
        else:
            in_specs.append(pl.BlockSpec((1, w.shape[1], tn), lambda i, j, e: (e[i], 0, j)))
        blocks += _nbytes((w.shape[-2], tn), w.dtype)
    for arr, bs, im in extras:
        in_specs.append(pl.BlockSpec(bs, im))
        blocks += _nbytes(bs, arr.dtype)
    out_specs = [pl.BlockSpec((tm, tn), lambda i, j, *_: (i, j)) for _ in out_dtypes]
    out_shape = [jax.ShapeDtypeStruct((m, n_cols), dt) for dt in out_dtypes]
    blocks += sum(_nbytes((tm, tn), dt) for dt in out_dtypes) + len(w_list) * _nbytes((tm, tn), F32)
    kern = functools.partial(_mm_kernel, n_a=len(a_list), a_of_w=tuple(a_of_w), n_extra=len(extras),
                             epi=epi, has_eids=eids is not None)
    grid_spec = pltpu.PrefetchScalarGridSpec(
        num_scalar_prefetch=0 if eids is None else 1, grid=(m // tm, n_cols // tn),
        in_specs=in_specs, out_specs=out_specs)
    args = ([] if eids is None else [eids]) + list(a_list) + list(w_list) + [e[0] for e in extras]
    outs = pl.pallas_call(kern, grid_spec=grid_spec, out_shape=out_shape,
                          compiler_params=_cparams(("arbitrary", "arbitrary"), blocks), name=name)(*args)
    return outs


def _nt_dot(a, b):
    return lax.dot_general(a, b, (((1,), (1,)), ((), ())), preferred_element_type=F32)


def _lane_blocks(x):
    return [x[:, c * LANES:(c + 1) * LANES] for c in range(x.shape[1] // LANES)]


def _softmax_step(lg, off, vt, carry):
    m, l, acc = carry
    rows, width = lg.shape
    block_max = functools.reduce(jnp.maximum, _lane_blocks(lg))
    row_max = jnp.broadcast_to(jnp.max(block_max, axis=-1, keepdims=True), (rows, LANES))
    m_new = jnp.maximum(m, row_max + off)
    alpha = jnp.exp2(m - m_new)
    p = jnp.exp2(lg - jnp.tile(m_new - off, (1, width // LANES)))
    l = alpha * l + functools.reduce(jnp.add, _lane_blocks(p))
    acc = (jnp.tile(alpha, (1, acc.shape[1] // LANES)) * acc
           + jnp.dot(p.astype(BF16), vt, preferred_element_type=F32))
    return m_new, l, acc


def _softmax_init(rows, width):
    return (jnp.full((rows, LANES), NEG, F32), jnp.zeros((rows, LANES), F32), jnp.zeros((rows, width), F32))


def _softmax_finish(carry):
    _, l, acc = carry
    return acc / jnp.sum(l, axis=-1, keepdims=True)


def _rel_pos(rows, cols):
    return (lax.broadcasted_iota(I32, (rows, cols), 1) - lax.broadcasted_iota(I32, (rows, cols), 0)).astype(F32)


def _tile_loops(n_full, n_kv, past_step, edge_step, carry):
    n_wide = n_full // WIDE_TILES
    carry = lax.fori_loop(0, n_wide, lambda i, c: past_step(i * WIDE_TILES, WIDE_TILES, c), carry)
    carry = lax.fori_loop(n_wide * WIDE_TILES, n_full, lambda t, c: past_step(t, 1, c), carry)
    return lax.fori_loop(n_full, n_kv, edge_step, carry)


def _num_kv_tiles(q0, tq, tk, l_valid):
    kmax = jnp.minimum(l_valid, (((q0 + tq - 1) >> CHUNK_SHIFT) + 1) * CHUNK)
    return (kmax + tk - 1) // tk


def _dsa_kernel(qa_ref, iq_ref, iw_ref, k_ref, v_ref, ik_ref, o_ref, keys_ref, bias_ref, jlim_ref, relpos_ref, *,
                tq, tk, n_past, l_valid, topk, index_bits):
    q0 = n_past + pl.program_id(1) * tq
    n_kv = _num_kv_tiles(q0, tq, tk, l_valid)
    row = lax.broadcasted_iota(I32, (tq, tk), 0)
    col = lax.broadcasted_iota(I32, (tq, tk), 1)
    qpos = q0 + row
    lane = lax.broadcasted_iota(I32, (tq, LANES), 1)

    iw = iw_ref[0] * (I_Q ** -0.5)
    iq = iq_ref[0].astype(F32)
    iq_heads = []
    for h in range(IDX_HEADS):
        pair = iq[:, (h // 2) * LANES:(h // 2 + 1) * LANES]
        keep = (lane < IDX_DIM) if h % 2 == 0 else (lane >= IDX_DIM)
        iq_heads.append(jnp.where(keep, pair, 0.0).astype(BF16))
    iq_all = jnp.concatenate(iq_heads, axis=0)

    def to_key(x):
        bits = lax.bitcast_convert_type(x, I32)
        return bits ^ ((bits >> 31) & 0x7FFFFFFF)

    def score_tiles(t, n, carry):
        top1, top2 = carry
        start = pl.multiple_of(t * tk, tk)
        ikt = ik_ref[0, pl.ds(start, n * tk), :]
        rel = jnp.maximum(_nt_dot(iq_all, ikt), 0.0)
        score = jnp.zeros((tq, n * tk), F32)
        for h in range(IDX_HEADS):
            score = score + iw[:, h:h + 1] * rel[h * tq:(h + 1) * tq]
        for i in range(n):
            kpos = start + i * tk + col
            adm = ((kpos >> CHUNK_SHIFT) <= (qpos >> CHUNK_SHIFT)) & (kpos < l_valid)
            masked = jnp.where(adm, score[:, i * tk:(i + 1) * tk], -jnp.inf)
            keys_ref[t + i] = to_key(masked)
            for blk in _lane_blocks(masked):
                top2 = jnp.maximum(top2, jnp.minimum(top1, blk))
                top1 = jnp.maximum(top1, blk)
        return top1, top2

    n_pair = n_kv // 2
    lows = jnp.full((tq, LANES), -jnp.inf, F32)
    tops = lax.fori_loop(0, n_pair, lambda i, c: score_tiles(2 * i, 2, c), (lows, lows))
    top1, top2 = lax.fori_loop(2 * n_pair, n_kv, lambda t, c: score_tiles(t, 1, c), tops)

    def from_key(k):
        return lax.bitcast_convert_type(k ^ ((k >> 31) & 0x7FFFFFFF), F32)

    def reduce_tiles(tile_fn, combine, init):
        def body(t, acc):
            return combine(acc, functools.reduce(combine, _lane_blocks(tile_fn(keys_ref[t], t))))
        return lax.fori_loop(0, n_kv, body, init)

    def count(pred):
        per_lane = reduce_tiles(lambda k, t: jnp.where(pred(k, t), 1.0, 0.0), jnp.add, jnp.zeros((tq, LANES), F32))
        return jnp.sum(per_lane, axis=-1, keepdims=True)

    def largest_at_most(bound):
        per_lane = reduce_tiles(lambda k, t: jnp.where(k <= bound, k, MASKED_KEY), jnp.maximum,
                                jnp.full((tq, LANES), MASKED_KEY, I32))
        return to_key(jnp.max(from_key(per_lane), axis=-1, keepdims=True))

    def halve(lo, hi, cnt_lo):
        gap = hi - lo
        mid = lo + lax.shift_right_logical(gap, 1) + (gap & 1)
        cnt = count(lambda k, t: k >= mid)
        keep = cnt >= topk
        return jnp.where(keep, mid, lo), jnp.where(keep, hi, mid - 1), jnp.where(keep, cnt, cnt_lo)

    def unfinished(lo, hi, cnt_lo):
        return (lo != hi) & (cnt_lo != topk)

    def any_of(flags):
        return jnp.max(jnp.where(flags, 1.0, 0.0))

    def any_wide(lo, hi, cnt_lo):
        wide = lax.shift_right_logical(hi - lo, NARROW_BITS) != 0
        return any_of(unfinished(lo, hi, cnt_lo) & wide)

    def halve_wide(state):
        lo, hi, cnt_lo = halve(*state[:3])
        return lo, hi, cnt_lo, any_wide(lo, hi, cnt_lo)

    def step_down(state):
        lo, hi, cnt_lo = state[:3]
        todo = unfinished(lo, hi, cnt_lo)
        top = largest_at_most(hi)
        cnt = count(lambda k, t: k >= top)
        hit = cnt >= topk
        lo = jnp.where(todo & hit, top, lo)
        hi = jnp.where(todo, jnp.where(hit, top, top - 1), hi)
        cnt_lo = jnp.where(todo & hit, cnt, cnt_lo)
        lo, hi, cnt_lo = halve(lo, hi, cnt_lo)
        return lo, hi, cnt_lo, any_of(unfinished(lo, hi, cnt_lo))

    lo0 = to_key(jnp.min(top2, axis=-1, keepdims=True))
    hi0 = to_key(jnp.max(top1, axis=-1, keepdims=True))
    unknown = jnp.full((tq, 1), -1.0, F32)
    going = lambda state: state[3] > 0.0
    lo, hi, cnt_lo, _ = lax.while_loop(going, halve_wide, (lo0, hi0, unknown, any_wide(lo0, hi0, unknown)))
    thr = lax.while_loop(going, step_down, (lo, hi, cnt_lo, any_of(unfinished(lo, hi, cnt_lo))))[0]
    n_gt = count(lambda k, t: k > thr)
    n_ge = count(lambda k, t: k >= thr)
    want_ties = topk - n_gt
    need = ((n_ge - n_gt) > want_ties) & (thr > MASKED_KEY)
    jlim_ref[...] = jnp.full((tq, LANES), NO_LIMIT, I32)

    @pl.when(jnp.max(jnp.where(need, 1.0, 0.0)) > 0.0)
    def _():
        def index_bit(it, j_lim):
            cand = j_lim | lax.shift_left(jnp.int32(1), index_bits - 1 - it)
            below = count(lambda k, t: (k == thr) & ((t * tk + col) < cand))
            return jnp.where(below <= want_ties - 1.0, cand, j_lim)
        j_lim = lax.fori_loop(0, index_bits, index_bit, jnp.zeros((tq, 1), I32))
        jlim_ref[...] = jnp.broadcast_to(jnp.where(need, j_lim, NO_LIMIT), (tq, LANES))

    j_lim = jlim_ref[:, 0:1]

    def bias_tile(t, carry):
        k = keys_ref[t]
        kpos = t * tk + col
        sel = (k > thr) | ((k == thr) & (kpos <= j_lim))
        sel = sel & (k > MASKED_KEY)
        bias_ref[t] = jnp.where(sel, 0.0, NEG)
        return carry

    lax.fori_loop(0, n_kv, bias_tile, 0)

    qk_scale = HEAD_DIM ** -0.5 * LOG2E
    n_full = jnp.minimum(q0, l_valid) // tk

    @pl.when((pl.program_id(0) == 0) & (pl.program_id(1) == 0))
    def _():
        relpos_ref[...] = _rel_pos(tq, WIDE_TILES * tk)

    for j in range(A_KV_HEADS):
        q4 = jnp.concatenate([qa_ref[0, :, (j * A_GROUP + g) * HEAD_DIM:(j * A_GROUP + g + 1) * HEAD_DIM]
                              for g in range(A_GROUP)], axis=0)
        slopes2 = [2.0 ** -(j * A_GROUP + g + 1) * LOG2E for g in range(A_GROUP)]
        slope_rows = jnp.concatenate([jnp.full((tq, LANES), s2, F32) for s2 in slopes2], axis=0)

        def tile_operands(t, n, j=j, q4=q4):
            start = pl.multiple_of(t * tk, tk)
            kt = k_ref[0, pl.ds(start, n * tk), j * HEAD_DIM:(j + 1) * HEAD_DIM]
            vt = v_ref[0, pl.ds(start, n * tk), j * HEAD_DIM:(j + 1) * HEAD_DIM]
            bias = bias_ref[t] if n == 1 else jnp.concatenate([bias_ref[t + i] for i in range(n)], axis=1)
            return start, _nt_dot(q4, kt), vt, bias

        def past_step(t, n, carry, slopes2=slopes2, slope_rows=slope_rows):
            start, s, vt, bias = tile_operands(t, n)
            lg = jnp.concatenate([s[g * tq:(g + 1) * tq] * qk_scale + (relpos_ref[:, :n * tk] * slopes2[g] + bias)
                                  for g in range(A_GROUP)], axis=0)
            off = slope_rows * (start - q0).astype(F32)
            return _softmax_step(lg, off, vt, carry)

        def edge_step(t, carry, slopes2=slopes2):
            start, s, vt, bias = tile_operands(t, 1)
            dist = jnp.abs(qpos - (start + col)).astype(F32)
            lg = jnp.concatenate([s[g * tq:(g + 1) * tq] * qk_scale - slopes2[g] * dist + bias
                                  for g in range(A_GROUP)], axis=0)
            return _softmax_step(lg, 0.0, vt, carry)

        out = _softmax_finish(_tile_loops(n_full, n_kv, past_step, edge_step,
                                          _softmax_init(A_GROUP * tq, HEAD_DIM)))
        for g in range(A_GROUP):
            h = j * A_GROUP + g
            o_ref[0, :, h * HEAD_DIM:(h + 1) * HEAD_DIM] = out[g * tq:(g + 1) * tq].astype(o_ref.dtype)


def _dsa_attention(zb, zf, keys, off, *, n_past, l_valid, tq, tk, topk, name):
    b, s, _ = zb.shape
    karr, kcol, varr, vcol, ikarr, ikcol = keys
    lp = karr.shape[1]
    n_tiles = lp // tk
    kern = functools.partial(_dsa_kernel, tq=tq, tk=tk, n_past=n_past, l_valid=l_valid, topk=topk,
                             index_bits=lp.bit_length())
    blocks = (_nbytes((tq, A_Q + I_Q), BF16) + _nbytes((tq, LANES), F32) + 2 * _nbytes((lp, A_KV), BF16)
              + _nbytes((lp, LANES), BF16) + _nbytes((tq, A_Q), BF16) + _nbytes((tq, lp), F32)
              + 8 * _nbytes((A_GROUP * tq, tk), F32))
    return pl.pallas_call(
        kern,
        grid=(b, s // tq),
        in_specs=[pl.BlockSpec((1, tq, A_Q), lambda bi, qi: (bi, qi, off['aq'] // A_Q)),
                  pl.BlockSpec((1, tq, I_Q), lambda bi, qi: (bi, qi, off['iq'] // I_Q)),
                  pl.BlockSpec((1, tq, LANES), lambda bi, qi: (bi, qi, off['iw'] // LANES)),
                  pl.BlockSpec((1, lp, A_KV), lambda bi, qi: (bi, 0, kcol)),
                  pl.BlockSpec((1, lp, A_KV), lambda bi, qi: (bi, 0, vcol)),
                  pl.BlockSpec((1, lp, LANES), lambda bi, qi: (bi, 0, ikcol))],
        out_specs=pl.BlockSpec((1, tq, A_Q), lambda bi, qi: (bi, qi, 0)),
        out_shape=jax.ShapeDtypeStruct((b, s, A_Q), BF16),
        scratch_shapes=[pltpu.VMEM((n_tiles, tq, tk), I32), pltpu.VMEM((n_tiles, tq, tk), F32),
                        pltpu.VMEM((tq, LANES), I32), pltpu.VMEM((tq, WIDE_TILES * tk), F32)],
        compiler_params=_cparams(("arbitrary", "arbitrary"), blocks),
        name=name,
    )(zb, zb, zf, karr, varr, ikarr)


def _diff_kernel(q_ref, k_ref, v_ref, lq_ref, g_ref, o_ref, relbias_ref, *, tq, tk, n_past, l_valid, lam_init):
    h = pl.program_id(1)
    q0 = n_past + pl.program_id(2) * tq
    n_kv = _num_kv_tiles(q0, tq, tk, l_valid)
    row = lax.broadcasted_iota(I32, (tq, tk), 0)
    col = lax.broadcasted_iota(I32, (tq, tk), 1)
    qpos = q0 + row
    slope2 = jnp.float32(1.0)
    for hh in range(B_HEADS):
        slope2 = jnp.where(h == hh, jnp.float32(2.0 ** (-8.0 * (hh + 1) / B_HEADS) * LOG2E), slope2)
    qk_scale = HEAD_DIM ** -0.5 * LOG2E
    n_full = jnp.minimum(q0, l_valid) // tk

    @pl.when(pl.program_id(2) == 0)
    def _():
        relbias_ref[...] = _rel_pos(tq, WIDE_TILES * tk) * slope2

    qc = [q_ref[0, :, c * HEAD_DIM:(c + 1) * HEAD_DIM] for c in range(2)]

    def tile_operands(t, n):
        start = pl.multiple_of(t * tk, tk)
        s = [_nt_dot(qc[c], k_ref[0, pl.ds(start, n * tk), c * HEAD_DIM:(c + 1) * HEAD_DIM]) for c in range(2)]
        return start, s, v_ref[0, pl.ds(start, n * tk), :]

    def past_step(t, n, carry):
        start, s, vt = tile_operands(t, n)
        lg = jnp.concatenate([s[c] * qk_scale + relbias_ref[:, :n * tk] for c in range(2)], axis=0)
        return _softmax_step(lg, slope2 * (start - q0).astype(F32), vt, carry)

    def edge_step(t, carry):
        start, s, vt = tile_operands(t, 1)
        kpos = start + col
        adm = ((kpos >> CHUNK_SHIFT) <= (qpos >> CHUNK_SHIFT)) & (kpos < l_valid)
        bias = jnp.where(adm, -slope2 * jnp.abs(qpos - kpos).astype(F32), NEG)
        lg = jnp.concatenate([s[c] * qk_scale + bias for c in range(2)], axis=0)
        return _softmax_step(lg, 0.0, vt, carry)

    out = _softmax_finish(_tile_loops(n_full, n_kv, past_step, edge_step,
                                      _softmax_init(2 * tq, 2 * HEAD_DIM)))
    lq = lq_ref[...]
    lam = (jnp.exp(jnp.sum(lq[0:1] * lq[1:2], axis=-1, keepdims=True))
           - jnp.exp(jnp.sum(lq[2:3] * lq[3:4], axis=-1, keepdims=True)) + lam_init)
    o = out[:tq] - lam * out[tq:]
    y = o * lax.rsqrt(jnp.mean(o * o, axis=-1, keepdims=True) + SUBLN_EPS)
    y = (y * g_ref[...]) * (1.0 - lam_init)
    o_ref[0] = y.astype(o_ref.dtype)


def _diff_attention(zb, keys, lam_qk_l, g_subln_l, off, *, n_past, l_valid, tq, tk, lam_init, name):
    b, s, _ = zb.shape
    karr, kcol0, varr, vcol0 = keys
    lp = karr.shape[1]
    w = 2 * HEAD_DIM
    kern = functools.partial(_diff_kernel, tq=tq, tk=tk, n_past=n_past, l_valid=l_valid, lam_init=lam_init)
    blocks = (2 * _nbytes((tq, w), BF16) + 2 * _nbytes((lp, w), BF16) + 8 * _nbytes((2 * tq, tk), F32))
    return pl.pallas_call(
        kern,
        grid=(b, B_HEADS, s // tq),
        in_specs=[pl.BlockSpec((1, tq, w), lambda bi, h, qi: (bi, qi, off['bq'] // w + h)),
                  pl.BlockSpec((1, lp, w), lambda bi, h, qi: (bi, 0, kcol0 + h)),
                  pl.BlockSpec((1, lp, w), lambda bi, h, qi: (bi, 0, vcol0 + h)),
                  pl.BlockSpec((4, HEAD_DIM), lambda bi, h, qi: (0, 0)),
                  pl.BlockSpec((1, w), lambda bi, h, qi: (0, 0))],
        out_specs=pl.BlockSpec((1, tq, w), lambda bi, h, qi: (bi, qi, h)),
        out_shape=jax.ShapeDtypeStruct((b, s, B_V), BF16),
        scratch_shapes=[pltpu.VMEM((tq, WIDE_TILES * tk), F32)],
        compiler_params=_cparams(("arbitrary", "arbitrary", "arbitrary"), blocks),
        name=name,
    )(zb, karr, varr, lam_qk_l, g_subln_l.reshape(1, w))


def _row_gather(src_ref, idx_ref, idx_base, idx_stride, dst_ref, sem, n_rows):
    def row_copy(r):
        return pltpu.make_async_copy(src_ref.at[pl.ds(idx_ref[idx_base + r * idx_stride], 1)],
                                     dst_ref.at[pl.ds(r, 1)], sem)

    def start():
        lax.fori_loop(0, n_rows, lambda r, c: (row_copy(r).start(), c)[1], 0, unroll=ISSUE_UNROLL)

    def wait():
        pltpu.make_async_copy(dst_ref.at[pl.ds(0, n_rows)], dst_ref.at[pl.ds(0, n_rows)], sem).wait()

    return start, wait


def _moe_up_kernel(eid_ref, tok_ref, h_ref, wg_ref, wu_ref, o_ref, rows_ref, rows16_ref, sems, *, blk, n_blocks):
    i, j = pl.program_id(0), pl.program_id(1)
    slot = i % 2

    def gather(block, buf):
        return _row_gather(h_ref, tok_ref, block * blk, 1, rows_ref.at[buf], sems.at[buf], blk)

    @pl.when((i == 0) & (j == 0))
    def _():
        gather(0, 0)[0]()

    @pl.when(j == 0)
    def _():
        gather(i, slot)[1]()

        @pl.when(i + 1 < n_blocks)
        def _():
            gather(i + 1, 1 - slot)[0]()

        rows16_ref[...] = rows_ref[slot].astype(BF16)

    a = rows16_ref[...]
    u = jnp.dot(a, wg_ref[0], preferred_element_type=F32)
    v = jnp.dot(a, wu_ref[0], preferred_element_type=F32)
    o_ref[...] = ((u * jax.nn.sigmoid(u)) * v).astype(o_ref.dtype)


def _moe_up(h2, tok_buf, block_expert, w_gate, w_up, *, blk, name):
    p = tok_buf.shape[0]
    d = h2.shape[1]
    fp = w_gate.shape[-1]
    tn = _tile(fp, 512)
    n_blocks = p // blk
    blocks = (_nbytes((blk, d), F32) + _nbytes((blk, d), BF16) // 2 + 2 * _nbytes((d, tn), BF16)
              + _nbytes((blk, tn), BF16) + _nbytes((blk, tn), F32))
    return pl.pallas_call(
        functools.partial(_moe_up_kernel, blk=blk, n_blocks=n_blocks),
        grid_spec=pltpu.PrefetchScalarGridSpec(
            num_scalar_prefetch=2, grid=(n_blocks, fp // tn),
            in_specs=[pl.BlockSpec(memory_space=pl.ANY),
                      pl.BlockSpec((1, d, tn), lambda i, j, e, tok: (e[i], 0, j)),
                      pl.BlockSpec((1, d, tn), lambda i, j, e, tok: (e[i], 0, j))],
            out_specs=pl.BlockSpec((blk, tn), lambda i, j, e, tok: (i, j)),
            scratch_shapes=[pltpu.VMEM((2, blk, d), F32), pltpu.VMEM((blk, d), BF16),
                            pltpu.SemaphoreType.DMA((2,))]),
        out_shape=jax.ShapeDtypeStruct((p, fp), BF16),
        compiler_params=_cparams(("arbitrary", "arbitrary"), blocks),
        name=name,
    )(block_expert, tok_buf, h2, w_gate, w_up)


def _combine_kernel(dest_ref, x_ref, y_ref, rg_ref, gate_ref, o_ref, picked_ref, sems, *, tm, n_steps):
    i = pl.program_id(0)
    slot = i % 2

    def fetch(step, buf):
        base = step * tm * TOP_K_EXPERTS
        return [_row_gather(y_ref, dest_ref, base + k, TOP_K_EXPERTS, picked_ref.at[buf, k], sems.at[buf, k], tm)
                for k in range(TOP_K_EXPERTS)]

    @pl.when(i == 0)
    def _():
        for start, _ in fetch(0, 0):
            start()

    for _, wait in fetch(i, slot):
        wait()

    @pl.when(i + 1 < n_steps)
    def _():
        for start, _ in fetch(i + 1, 1 - slot):
            start()

    rg = rg_ref[...]
    f = rg[:, 0:1] * picked_ref[slot, 0] + rg[:, 1:2] * picked_ref[slot, 1]
    o_ref[...] = x_ref[...] + gate_ref[0] * f


def _moe_combine(x2, yb, dest, rgate, gate_f, seq, *, name):
    t, d = x2.shape
    tm = _tile(seq, 256)
    per_b = seq // tm
    return pl.pallas_call(
        functools.partial(_combine_kernel, tm=tm, n_steps=t // tm),
        grid_spec=pltpu.PrefetchScalarGridSpec(
            num_scalar_prefetch=1, grid=(t // tm,),
            in_specs=[pl.BlockSpec((tm, d), lambda i, dst: (i, 0)),
                      pl.BlockSpec(memory_space=pl.ANY),
                      pl.BlockSpec((tm, LANES), lambda i, dst: (i, 0)),
                      pl.BlockSpec((1, 1, d), lambda i, dst: (i // per_b, 0, 0))],
            out_specs=pl.BlockSpec((tm, d), lambda i, dst: (i, 0)),
            scratch_shapes=[pltpu.VMEM((2, TOP_K_EXPERTS, tm, d), F32),
                            pltpu.SemaphoreType.DMA((2, TOP_K_EXPERTS))]),
        out_shape=jax.ShapeDtypeStruct((t, d), F32),
        compiler_params=_cparams(("arbitrary",), 4 * _nbytes((tm, d), F32)),
        name=name,
    )(dest, x2, yb, rgate, gate_f)


STATE_COLS = (('ak', A_KV), ('av', A_KV), ('ik', LANES), ('bk', B_QK), ('bv', B_V))


def _state_tails():
    return ((A_KV_HEADS, HEAD_DIM), (A_KV_HEADS, HEAD_DIM), (IDX_DIM,), (B_HEADS, 2, HEAD_DIM),
            (B_HEADS, 2 * HEAD_DIM))


def _state_kernel(*refs, depth):
    n = len(STATE_COLS)
    ins, outs = refs[:depth * n], refs[depth * n:]
    for l in range(depth):
        @pl.when(pl.program_id(0) == l)
        def _(l=l):
            ak, av, ik, bk, bv = ins[l * n:(l + 1) * n]
            for j in range(A_KV_HEADS):
                outs[0][0, 0, :, j, :] = ak[0, :, j * HEAD_DIM:(j + 1) * HEAD_DIM]
                outs[1][0, 0, :, j, :] = av[0, :, j * HEAD_DIM:(j + 1) * HEAD_DIM]
            outs[2][0, 0] = ik[0, :, :IDX_DIM]
            for h in range(B_HEADS):
                for c in range(2):
                    lo = (2 * h + c) * HEAD_DIM
                    outs[3][0, 0, :, h, c, :] = bk[0, :, lo:lo + HEAD_DIM]
                outs[4][0, 0, :, h, :] = bv[0, :, 2 * h * HEAD_DIM:2 * (h + 1) * HEAD_DIM]


def _state_rows(zf3_layers, off, *, name):
    depth = len(zf3_layers)
    b, s, _ = zf3_layers[0].shape
    ts = _tile(s, 256)
    in_specs, args = [], []
    for l in range(depth):
        for cname, width in STATE_COLS:
            blk = off[cname] // width
            in_specs.append(pl.BlockSpec(
                (1, ts, width),
                lambda li, bi, i, l=l, blk=blk: (jnp.where(li == l, bi, 0), jnp.where(li == l, i, 0), blk)))
            args.append(zf3_layers[l])
    tails = _state_tails()
    out_specs = [pl.BlockSpec((1, 1, ts) + t, lambda li, bi, i, nt=len(t): (li, bi, i) + (0,) * nt) for t in tails]
    out_shape = [jax.ShapeDtypeStruct((depth, b, s) + t, F32) for t in tails]
    blocks = (depth + 1) * sum(_nbytes((ts, w), F32) for _, w in STATE_COLS)
    return pl.pallas_call(
        functools.partial(_state_kernel, depth=depth),
        grid=(depth, b, s // ts), in_specs=in_specs, out_specs=out_specs, out_shape=out_shape,
        compiler_params=_cparams(("arbitrary", "arbitrary", "arbitrary"), blocks), name=name,
    )(*args)


def _pack_kernel(c_hbm, new_ref, o_ref, buf, sems, *, layer, tk, n_cached_tiles, tail):
    b, i = pl.program_id(0), pl.program_id(1)
    slot = i % 2
    piece = tail[-1]
    heads = list(itertools.product(*[range(t) for t in tail[:-1]]))

    def tile_copies(tile, s):
        rows = pl.ds(tile * tk, tk)
        return [pltpu.make_async_copy(c_hbm.at[(layer, b, rows) + idx + (slice(None),)], buf.at[s, n], sems.at[s, n])
                for n, idx in enumerate(heads)]

    @pl.when(i == 0)
    def _():
        for c in tile_copies(0, 0):
            c.start()

    @pl.when(i < n_cached_tiles)
    def _():
        for c in tile_copies(i, slot):
            c.wait()

        @pl.when(i + 1 < n_cached_tiles)
        def _():
            for c in tile_copies(i + 1, 1 - slot):
                c.start()

        for n in range(len(heads)):
            o_ref[0, :, n * piece:(n + 1) * piece] = buf[slot, n].astype(BF16)

    @pl.when(i >= n_cached_tiles)
    def _():
        o_ref[0] = new_ref[0]


def _pack_keys(cache, layer, new_rows, *, tk, name):
    _, b, p = cache.shape[:3]
    tail = cache.shape[3:]
    w = math.prod(tail)
    assert p % tk == 0 and new_rows.shape == (b, tk, w)
    n_cached_tiles = p // tk
    n_heads = w // tail[-1]
    return pl.pallas_call(
        functools.partial(_pack_kernel, layer=layer, tk=tk, n_cached_tiles=n_cached_tiles, tail=tail),
        grid=(b, n_cached_tiles + 1),
        in_specs=[pl.BlockSpec(memory_space=pl.ANY),
                  pl.BlockSpec((1, tk, w), lambda bi, i: (bi, 0, 0))],
        out_specs=pl.BlockSpec((1, tk, w), lambda bi, i: (bi, i, 0)),
        out_shape=jax.ShapeDtypeStruct((b, p + tk, w), BF16),
        scratch_shapes=[pltpu.VMEM((2, n_heads, tk, tail[-1]), F32), pltpu.SemaphoreType.DMA((2, n_heads))],
        compiler_params=_cparams(("arbitrary", "arbitrary"), _nbytes((tk, w), F32) + 2 * _nbytes((tk, w), BF16)),
        name=name,
    )(cache, new_rows)


def _silu_mul(accs, extras):
    u, v = accs
    return [(u * jax.nn.sigmoid(u)) * v]


def _moe_ffn(x2, h2, ridx, rgate, gate_f, seq, w_gate, w_up, w_down, *, tag):
    t, d = x2.shape
    fp = w_gate.shape[-1]
    a = t * TOP_K_EXPERTS
    blk = _tile(a, 512)
    n_blocks = -(-(a + N_EXPERTS * (blk - 1)) // blk)
    p = n_blocks * blk
    e_flat = ridx[:, :TOP_K_EXPERTS].reshape(-1)
    onehot = (e_flat[:, None] == jnp.arange(N_EXPERTS, dtype=I32)[None, :]).astype(I32)
    rank = jnp.sum((jnp.cumsum(onehot, axis=0) - onehot) * onehot, axis=1)
    counts = jnp.sum(onehot, axis=0)
    padded = ((counts + blk - 1) // blk) * blk
    pends = jnp.cumsum(padded)
    pstarts = pends - padded
    dest = (pstarts[e_flat] + rank).astype(I32)
    tok_buf = jnp.zeros((p,), I32).at[dest].set(jnp.arange(a, dtype=I32) // TOP_K_EXPERTS)
    block_expert = jnp.clip(jnp.searchsorted(pends, jnp.arange(n_blocks, dtype=I32) * blk, side='right'),
                            0, N_EXPERTS - 1).astype(I32)

    act = _moe_up(h2, tok_buf, block_expert, w_gate, w_up, blk=blk, name=f"moe_up_{tag}")
    (yb,) = _matmul([act], [w_down], [0], [], lambda accs, ex: accs, [F32], d, tm=blk, tn=_tile(d, 512),
                    eids=block_expert, name=f"moe_down_{tag}")
    return _moe_combine(x2, yb, dest, rgate, gate_f, seq, name=f"moe_combine_{tag}")


def _trunk(x, mods, past_all, wts, tag):
    b, s, d = x.shape
    t = b * s
    depth = len(wts['w_in'])
    off, nz = _z_layout(d)
    x2 = x.reshape(t, d)
    tn_d = _tile(d, 512)
    tm_b = _tile(s, 512)
    tm_f = _tile(t, 1024)
    per_b = s // tm_b
    zf3_layers = []
    for l in range(depth):
        shift_a, scale_a, gate_a, shift_f, scale_f, gate_f = [
            mods[l][:, i * d:(i + 1) * d].reshape(b, 1, d) for i in range(N_ADA)]
        batch_tile = lambda i, j, *_: (i // per_b, 0, j)

        h = _norm(x2, wts['g_attn'][l], s, scale=scale_a, shift=shift_a, out_dtype=BF16,
                  name=f"norm_attn_{tag}{l}")
        zf, zb = _matmul([h], [wts['w_in'][l]], [0], [], lambda accs, ex: [accs[0], accs[0]], [F32, BF16], nz,
                         tm=tm_f, tn=IN_PROJ_TN, name=f"in_proj_{tag}{l}")
        zf3, zb3 = zf.reshape(b, s, nz), zb.reshape(b, s, nz)
        cut = lambda z, name, width: z[:, :, off[name]:off[name] + width]
        zf3_layers.append(zf3)

        if past_all is None:
            n_past, l_valid = 0, s
            tk = _tile(s, 512)
            tq_a, tq_b = _tile(s, 128), _tile(s, 256)
            dsa_keys = (zb3, off['ak'] // A_KV, zb3, off['av'] // A_KV, zb3, off['ik'] // LANES)
            diff_keys = (zb3, off['bk'] // (2 * HEAD_DIM), zb3, off['bv'] // (2 * HEAD_DIM))
        else:
            n_past = past_all[0].shape[2]
            l_valid = n_past + s
            tk = 512
            tq_a = tq_b = s

            def new_rows(name, width):
                return jnp.concatenate([cut(zb3, name, width), jnp.zeros((b, tk - s, width), BF16)], axis=1)

            def with_past(ci, name, width):
                return _pack_keys(past_all[ci], l, new_rows(name, width), tk=tk, name=f"pack_{name}_{tag}{l}")

            ik_past = past_all[2][l].astype(BF16)
            ik_all = jnp.concatenate([jnp.concatenate([ik_past, ik_past], axis=-1), new_rows('ik', LANES)], axis=1)
            dsa_keys = (with_past(0, 'ak', A_KV), 0, with_past(1, 'av', A_KV), 0, ik_all, 0)
            diff_keys = (with_past(3, 'bk', B_QK), 0, with_past(4, 'bv', B_V), 0)
        topk = min(TOPK_MAX, l_valid // 4)
        oa = _dsa_attention(zb3, zf3, dsa_keys, off, n_past=n_past, l_valid=l_valid, tq=tq_a, tk=tk,
                            topk=topk, name=f"dsa_{tag}{l}")
        lam_init = 0.8 - 0.6 * math.exp(-0.3 * l)
        ob = _diff_attention(zb3, diff_keys, wts['lam_qk'][l], wts['g_subln'][l], off, n_past=n_past,
                             l_valid=l_valid, tq=tq_b, tk=tk, lam_init=lam_init, name=f"diff_{tag}{l}")

        def merge_epi(accs, ex):
            return [jax.nn.sigmoid(ex[0]) * accs[0] + jax.nn.sigmoid(ex[1]) * accs[1]]

        (merged,) = _matmul(
            [oa.reshape(t, A_Q), ob.reshape(t, B_V)], [wts['w_out_a'][l], wts['w_out_b'][l]], [0, 1],
            [(zf, (tm_f, tn_d), lambda i, j, *_: (i, off['ga'] // tn_d + j)),
             (zf, (tm_f, tn_d), lambda i, j, *_: (i, off['gb'] // tn_d + j))],
            merge_epi, [BF16], d, tm=tm_f, tn=tn_d, name=f"merge_{tag}{l}")

        def resid_epi(accs, ex):
            return [ex[0] + ex[1][0] * accs[0]]

        (x2,) = _matmul([merged], [wts['w_out'][l]], [0],
                        [(x2, (tm_b, tn_d), lambda i, j, *_: (i, j)), (gate_a, (1, 1, tn_d), batch_tile)],
                        resid_epi, [F32], d, tm=tm_b, tn=tn_d, name=f"out_proj_{tag}{l}")

        if l % 2 == 0:
            i = l // 2
            h = _norm(x2, wts['g_ffn'][l], s, scale=scale_f, shift=shift_f, out_dtype=BF16,
                      name=f"norm_ffn_{tag}{l}")
            fp = wts['w_ff_gate'][i].shape[-1]
            (act,) = _matmul([h], [wts['w_ff_gate'][i], wts['w_ff_up'][i]], [0, 0], [], _silu_mul, [BF16], fp,
                             tm=tm_f, tn=_tile(fp, 512), name=f"ffn_up_{tag}{l}")
            (x2,) = _matmul([act], [wts['w_ff_down'][i]], [0],
                            [(x2, (tm_b, tn_d), lambda i, j, *_: (i, j)), (gate_f, (1, 1, tn_d), batch_tile)],
                            resid_epi, [F32], d, tm=tm_b, tn=tn_d, name=f"ffn_down_{tag}{l}")
        else:
            i = l // 2
            h2, ridx, rgate = _norm(x2, wts['g_ffn'][l], s, scale=scale_f, shift=shift_f,
                                    router=(wts['w_router'][i], wts['b_router'][i]), out_dtype=F32,
                                    name=f"norm_router_{tag}{l}")
            x2 = _moe_ffn(x2, h2, ridx, rgate, gate_f, s, wts['w_moe_gate'][i], wts['w_moe_up'][i],
                          wts['w_moe_down'][i], tag=f"{tag}{l}")
    y = _norm(x2, wts['g_final'], s, out_dtype=F32, name=f"norm_final_{tag}")
    state = tuple(_state_rows(zf3_layers, off, name=f"state_rows_{tag}"))
    return y.reshape(b, s, d), state


def _prep_weights(w_in, w_out_a, w_out_b, w_out, w_ff_gate, w_ff_up, w_ff_down, w_router, b_router,
                  w_moe_gate, w_moe_up, w_moe_down, d_model):
    f = w_ff_gate.shape[-1]
    fp = _round_up(f, 512)
    pad_cols = lambda w: jnp.pad(w, [(0, 0)] * (w.ndim - 1) + [(0, fp - f)]).astype(BF16)
    pad_rows = lambda w: jnp.pad(w, [(0, 0)] * (w.ndim - 2) + [(0, fp - f), (0, 0)]).astype(BF16)
    n_moe = w_router.shape[0]
    return dict(
        w_in=[_prep_w_in(w_in[l], d_model) for l in range(w_in.shape[0])],
        w_out_a=w_out_a.astype(BF16), w_out_b=w_out_b.astype(BF16), w_out=w_out.astype(BF16),
        w_ff_gate=pad_cols(w_ff_gate), w_ff_up=pad_cols(w_ff_up), w_ff_down=pad_rows(w_ff_down),
        w_router=[jnp.pad(w_router[i], ((0, 0), (0, LANES - N_EXPERTS))) for i in range(n_moe)],
        b_router=[jnp.pad(b_router[i], (0, LANES - N_EXPERTS)).reshape(1, LANES) for i in range(n_moe)],
        w_moe_gate=pad_cols(w_moe_gate), w_moe_up=pad_cols(w_moe_up), w_moe_down=pad_rows(w_moe_down),
    )


def kernel(x_prompt, x_sample, c_prompt, c_sample, cache_dsa_k, cache_dsa_v, cache_idx_k, cache_diff_k, cache_diff_v, w_ada, b_ada, g_attn, w_in, w_out_a, w_out_b, w_out, lam_qk, g_subln, g_ffn, w_ff_gate, w_ff_up, w_ff_down, w_router, b_router, w_moe_gate, w_moe_up, w_moe_down, g_final):
    d = x_prompt.shape[-1]
    wts = _prep_weights(w_in, w_out_a, w_out_b, w_out, w_ff_gate, w_ff_up, w_ff_down, w_router, b_router,
                        w_moe_gate, w_moe_up, w_moe_down, d)
    wts.update(g_attn=g_attn, g_ffn=g_ffn, g_final=g_final, lam_qk=lam_qk, g_subln=g_subln)
    nb_p = c_prompt.shape[0]
    mods = _ada_mod(jnp.concatenate([c_prompt, c_sample], axis=0), w_ada, b_ada)
    y_p, st_p = _trunk(x_prompt, mods[:, :nb_p], None, wts, "p")
    past_all = (cache_dsa_k, cache_dsa_v, cache_idx_k, cache_diff_k, cache_diff_v)
    y_s, st_s = _trunk(x_sample, mods[:, nb_p:], past_all, wts, "s")
    return (y_p, y_s) + st_p + st_s
```

```python
import functools
import itertools
import math

import jax
import jax.numpy as jnp
from jax import lax
from jax.experimental import pallas as pl
from jax.experimental.pallas import tpu as pltpu

F32, BF16, I32 = jnp.float32, jnp.bfloat16, jnp.int32

CHUNK = 64
CHUNK_SHIFT = CHUNK.bit_length() - 1
HEAD_DIM = 128
A_HEADS = 8
A_KV_HEADS = 2
A_GROUP = A_HEADS // A_KV_HEADS
IDX_HEADS = 8
IDX_DIM = 64
TOPK_MAX = 256
B_HEADS = 4
N_EXPERTS = 8
TOP_K_EXPERTS = 2
N_ADA = 6
NORM_EPS = 1e-6
SUBLN_EPS = 1e-5
A_Q = A_HEADS * HEAD_DIM
A_KV = A_KV_HEADS * HEAD_DIM
I_Q = IDX_HEADS * IDX_DIM
B_QK = B_HEADS * 2 * HEAD_DIM
B_V = B_HEADS * 2 * HEAD_DIM

LANES = 128
VMEM_BYTES_V7X = 64 * 2 ** 20
NEG = -1e30
NO_LIMIT = 2 ** 30
LOG2E = math.log2(math.e)
IN_PROJ_TN = 512
MASKED_KEY = 0x807FFFFF - 2 ** 32
WIDE_TILES = 4
ISSUE_UNROLL = 8
NARROW_BITS = 12


def _cparams(dims, block_bytes):
    limit = min(max(2 * int(block_bytes) + (8 << 20), 32 << 20), VMEM_BYTES_V7X - (6 << 20))
    return pltpu.CompilerParams(dimension_semantics=dims, vmem_limit_bytes=limit)


def _nbytes(shape, dtype):
    return math.prod(shape) * jnp.dtype(dtype).itemsize


def _tile(n, pref):
    if n <= pref:
        return n
    t = pref
    while n % t:
        t //= 2
    return t


def _round_up(n, m):
    return -(-n // m) * m


def _z_layout(d_model):
    off, o = {}, 0
    for name, n in (('aq', A_Q), ('ak', A_KV), ('av', A_KV), ('iq', I_Q), ('bq', B_QK), ('bk', B_QK),
                    ('bv', B_V), ('ga', d_model), ('gb', d_model), ('ik', LANES), ('iw', LANES)):
        off[name] = o
        o += n
    return off, _round_up(o, IN_PROJ_TN)


def _prep_w_in(w, d_model):
    sizes = (A_Q, A_KV, A_KV, I_Q, IDX_DIM, IDX_HEADS, B_QK, B_QK, B_V, d_model, d_model)
    names = ('aq', 'ak', 'av', 'iq', 'ik', 'iw', 'bq', 'bk', 'bv', 'ga', 'gb')
    parts, o = {}, 0
    for n, s in zip(names, sizes):
        parts[n] = w[:, o:o + s]
        o += s
    cols = [parts[n] for n in ('aq', 'ak', 'av', 'iq', 'bq', 'bk', 'bv', 'ga', 'gb')]
    cols += [parts['ik'], parts['ik'], parts['iw']]
    used = sum(c.shape[1] for c in cols)
    cols.append(jnp.zeros((w.shape[0], _z_layout(d_model)[1] - used), w.dtype))
    return jnp.concatenate(cols, axis=1).astype(BF16)


def _ada_kernel(c_ref, w_ref, b_ref, o_ref):
    c = c_ref[...]
    s = c * jax.nn.sigmoid(c)
    o_ref[0] = jnp.dot(s.astype(BF16), w_ref[0].astype(BF16), preferred_element_type=F32) + b_ref[0]


def _ada_mod(c_all, w_ada, b_ada):
    depth, d, n = w_ada.shape
    nb = c_all.shape[0]
    tn = _tile(n, 1024)
    blocks = _nbytes((nb, d), F32) + _nbytes((d, tn), F32) * 2 + _nbytes((nb, tn), F32)
    return pl.pallas_call(
        _ada_kernel,
        grid=(depth, n // tn),
        in_specs=[pl.BlockSpec((nb, d), lambda l, j: (0, 0)),
                  pl.BlockSpec((1, d, tn), lambda l, j: (l, 0, j)),
                  pl.BlockSpec((1, 1, tn), lambda l, j: (l, 0, j))],
        out_specs=pl.BlockSpec((1, nb, tn), lambda l, j: (l, 0, j)),
        out_shape=jax.ShapeDtypeStruct((depth, nb, n), F32),
        compiler_params=_cparams(("arbitrary", "arbitrary"), blocks),
        name="ada_mod",
    )(c_all, w_ada, b_ada.reshape(depth, 1, n))


def _norm_kernel(*refs, modulated, router, eps):
    refs = list(refs)
    x_ref, g_ref = refs[:2]
    pos = 2
    x = x_ref[...]
    y = x * lax.rsqrt(jnp.mean(x * x, axis=-1, keepdims=True) + eps)
    y = y * g_ref[...]
    if modulated:
        scale_ref, shift_ref = refs[pos:pos + 2]
        pos += 2
        y = y * (1.0 + scale_ref[0]) + shift_ref[0]
    if router:
        wr_ref, br_ref = refs[pos:pos + 2]
        pos += 2
    o_ref = refs[pos]
    o_ref[...] = y.astype(o_ref.dtype)
    if router:
        idx_ref, gate_ref = refs[pos + 1:pos + 3]
        logits = jnp.dot(y, wr_ref[...], preferred_element_type=F32,
                         precision=lax.Precision.HIGHEST) + br_ref[...]
        lane = lax.broadcasted_iota(I32, logits.shape, 1)
        logits = jnp.where(lane < N_EXPERTS, logits, -jnp.inf)
        m1 = jnp.max(logits, axis=-1, keepdims=True)
        i1 = jnp.min(jnp.where(logits == m1, lane, LANES), axis=-1, keepdims=True)
        rest = jnp.where(lane == i1, -jnp.inf, logits)
        m2 = jnp.max(rest, axis=-1, keepdims=True)
        i2 = jnp.min(jnp.where(rest == m2, lane, LANES), axis=-1, keepdims=True)
        e = jnp.exp(m2 - m1)
        g1 = 1.0 / (1.0 + e)
        g2 = e / (1.0 + e)
        idx_ref[...] = jnp.where(lane == 0, i1, jnp.where(lane == 1, i2, 0))
        gate_ref[...] = jnp.where(lane == 0, g1, jnp.where(lane == 1, g2, 0.0))


def _norm(x2, g, seq, *, scale=None, shift=None, router=None, out_dtype, eps=NORM_EPS, name):
    t, d = x2.shape
    tm = _tile(seq, 256)
    per_b = seq // tm
    in_specs = [pl.BlockSpec((tm, d), lambda i: (i, 0)), pl.BlockSpec((1, d), lambda i: (0, 0))]
    args = [x2, g.reshape(1, d)]
    if scale is not None:
        in_specs += [pl.BlockSpec((1, 1, d), lambda i: (i // per_b, 0, 0))] * 2
        args += [scale, shift]
    out_specs = [pl.BlockSpec((tm, d), lambda i: (i, 0))]
    out_shape = [jax.ShapeDtypeStruct((t, d), out_dtype)]
    if router is not None:
        w_r, b_r = router
        in_specs += [pl.BlockSpec((d, LANES), lambda i: (0, 0)), pl.BlockSpec((1, LANES), lambda i: (0, 0))]
        args += [w_r, b_r]
        out_specs += [pl.BlockSpec((tm, LANES), lambda i: (i, 0))] * 2
        out_shape += [jax.ShapeDtypeStruct((t, LANES), I32), jax.ShapeDtypeStruct((t, LANES), F32)]
    blocks = 3 * _nbytes((tm, d), F32) + _nbytes((d, LANES), F32)
    outs = pl.pallas_call(
        functools.partial(_norm_kernel, modulated=scale is not None, router=router is not None, eps=eps),
        grid=(t // tm,), in_specs=in_specs, out_specs=out_specs, out_shape=out_shape,
        compiler_params=_cparams(("arbitrary",), blocks), name=name,
    )(*args)
    return outs if router is not None else outs[0]


def _mm_kernel(*refs, n_a, a_of_w, n_extra, epi, has_eids, normed):
    refs = list(refs)
    if has_eids:
        refs = refs[1:]
    n_w = len(a_of_w)
    a_refs = refs[:n_a]
    w_refs = refs[n_a:n_a + n_w]
    pos = n_a + n_w
    if normed:
        g_ref, scale_ref, shift_ref = refs[pos:pos + 3]
        pos += 3
        h_ref = refs.pop()

        @pl.when(pl.program_id(1) == 0)
        def _():
            x = a_refs[0][...]
            y = x * lax.rsqrt(jnp.mean(x * x, axis=-1, keepdims=True) + NORM_EPS)
            y = y * g_ref[...]
            h_ref[...] = (y * (1.0 + scale_ref[0]) + shift_ref[0]).astype(BF16)

        a_vals = [h_ref[...]]
    else:
        a_vals = [a[...].astype(BF16) for a in a_refs]
    e_refs = refs[pos:pos + n_extra]
    o_refs = refs[pos + n_extra:]
    accs = []
    for ai, w_ref in zip(a_of_w, w_refs):
        w = w_ref[0] if len(w_ref.shape) == 3 else w_ref[...]
        accs.append(jnp.dot(a_vals[ai], w, preferred_element_type=F32))
    outs = epi(accs, [e[...] for e in e_refs])
    for o_ref, o in zip(o_refs, outs):
        o_ref[...] = o.astype(o_ref.dtype)


def _matmul(a_list, w_list, a_of_w, extras, epi, out_dtypes, n_cols, *, tm, tn, eids=None, norm=None, name):
    m = a_list[0].shape[0]
    in_specs, blocks, scratch = [], 0, []
    for a in a_list:
        in_specs.append(pl.BlockSpec((tm, a.shape[1]), lambda i, j, *_: (i, 0)))
        blocks += _nbytes((tm, a.shape[1]), a.dtype)
    for w in w_list:
        if w.ndim == 2:
            in_specs.append(pl.BlockSpec((w.shape[0], tn), lambda i, j, *_: (0, j)))
        else:
            in_specs.append(pl.BlockSpec((1, w.shape[1], tn), lambda i, j, e: (e[i], 0, j)))
        blocks += _nbytes((w.shape[-2], tn), w.dtype)
    norm_args = []
    if norm is not None:
        g, scale, shift, seq = norm
        k = a_list[0].shape[1]
        assert len(a_list) == 1 and seq % tm == 0
        per_b = seq // tm
        in_specs += [pl.BlockSpec((1, k), lambda i, j, *_: (0, 0))]
        in_specs += [pl.BlockSpec((1, 1, k), lambda i, j, *_: (i // per_b, 0, 0))] * 2
        norm_args = [g.reshape(1, k), scale, shift]
        scratch = [pltpu.VMEM((tm, k), BF16)]
        blocks += _nbytes((tm, k), BF16)
    for arr, bs, im in extras:
        in_specs.append(pl.BlockSpec(bs, im))
        blocks += _nbytes(bs, arr.dtype)
    out_specs = [pl.BlockSpec((tm, tn), lambda i, j, *_: (i, j)) for _ in out_dtypes]
    out_shape = [jax.ShapeDtypeStruct((m, n_cols), dt) for dt in out_dtypes]
    blocks += sum(_nbytes((tm, tn), dt) for dt in out_dtypes) + len(w_list) * _nbytes((tm, tn), F32)
    kern = functools.partial(_mm_kernel, n_a=len(a_list), a_of_w=tuple(a_of_w), n_extra=len(extras),
                             epi=epi, has_eids=eids is not None, normed=norm is not None)
    grid_spec = pltpu.PrefetchScalarGridSpec(
        num_scalar_prefetch=0 if eids is None else 1, grid=(m // tm, n_cols // tn),
        in_specs=in_specs, out_specs=out_specs, scratch_shapes=scratch)
    args = (([] if eids is None else [eids]) + list(a_list) + list(w_list) + norm_args
            + [e[0] for e in extras])
    outs = pl.pallas_call(kern, grid_spec=grid_spec, out_shape=out_shape,
                          compiler_params=_cparams(("arbitrary", "arbitrary"), blocks), name=name)(*args)
    return outs


def _nt_dot(a, b):
    return lax.dot_general(a, b, (((1,), (1,)), ((), ())), preferred_element_type=F32)


def _lane_blocks(x):
    return [x[:, c * LANES:(c + 1) * LANES] for c in range(x.shape[1] // LANES)]


def _softmax_step(lg, off, vt, carry):
    m, l, acc = carry
    rows, width = lg.shape
    block_max = functools.reduce(jnp.maximum, _lane_blocks(lg))
    row_max = jnp.broadcast_to(jnp.max(block_max, axis=-1, keepdims=True), (rows, LANES))
    m_new = jnp.maximum(m, row_max + off)
    alpha = jnp.exp2(m - m_new)
    p = jnp.exp2(lg - jnp.tile(m_new - off, (1, width // LANES)))
    l = alpha * l + functools.reduce(jnp.add, _lane_blocks(p))
    acc = (jnp.tile(alpha, (1, acc.shape[1] // LANES)) * acc
           + jnp.dot(p.astype(BF16), vt, preferred_element_type=F32))
    return m_new, l, acc


def _softmax_init(rows, width):
    return (jnp.full((rows, LANES), NEG, F32), jnp.zeros((rows, LANES), F32), jnp.zeros((rows, width), F32))


def _softmax_finish(carry):
    _, l, acc = carry
    return acc / jnp.sum(l, axis=-1, keepdims=True)


def _rel_pos(rows, cols):
    return (lax.broadcasted_iota(I32, (rows, cols), 1) - lax.broadcasted_iota(I32, (rows, cols), 0)).astype(F32)


def _tile_loops(n_full, n_kv, past_step, edge_step, carry):
    n_wide = n_full // WIDE_TILES
    carry = lax.fori_loop(0, n_wide, lambda i, c: past_step(i * WIDE_TILES, WIDE_TILES, c), carry)
    done = n_wide * WIDE_TILES
    n_pair = (n_full - done) // 2
    carry = lax.fori_loop(0, n_pair, lambda i, c: past_step(done + 2 * i, 2, c), carry)
    carry = lax.fori_loop(done + 2 * n_pair, n_full, lambda t, c: past_step(t, 1, c), carry)
    return lax.fori_loop(n_full, n_kv, edge_step, carry)


def _num_kv_tiles(q0, tq, tk, l_valid):
    kmax = jnp.minimum(l_valid, (((q0 + tq - 1) >> CHUNK_SHIFT) + 1) * CHUNK)
    return (kmax + tk - 1) // tk


def _dsa_kernel(qa_ref, iq_ref, iw_ref, k_ref, v_ref, ik_ref, o_ref, keys_ref, bias_ref, jlim_ref, relpos_ref, *,
                tq, tk, n_past, l_valid, topk, index_bits):
    q0 = n_past + pl.program_id(1) * tq
    n_kv = _num_kv_tiles(q0, tq, tk, l_valid)
    row = lax.broadcasted_iota(I32, (tq, tk), 0)
    col = lax.broadcasted_iota(I32, (tq, tk), 1)
    qpos = q0 + row
    lane = lax.broadcasted_iota(I32, (tq, LANES), 1)

    iw = iw_ref[0] * (I_Q ** -0.5)
    iq = iq_ref[0].astype(F32)
    iq_heads = []
    for h in range(IDX_HEADS):
        pair = iq[:, (h // 2) * LANES:(h // 2 + 1) * LANES]
        keep = (lane < IDX_DIM) if h % 2 == 0 else (lane >= IDX_DIM)
        iq_heads.append(jnp.where(keep, pair, 0.0).astype(BF16))
    iq_all = jnp.concatenate(iq_heads, axis=0)

    def to_key(x):
        bits = lax.bitcast_convert_type(x, I32)
        return bits ^ ((bits >> 31) & 0x7FFFFFFF)

    def score_tiles(t, n, carry):
        top1, top2 = carry
        start = pl.multiple_of(t * tk, tk)
        ikt = ik_ref[0, pl.ds(start, n * tk), :]
        rel = jnp.maximum(_nt_dot(iq_all, ikt), 0.0)
        score = jnp.zeros((tq, n * tk), F32)
        for h in range(IDX_HEADS):
            score = score + iw[:, h:h + 1] * rel[h * tq:(h + 1) * tq]
        for i in range(n):
            kpos = start + i * tk + col
            adm = ((kpos >> CHUNK_SHIFT) <= (qpos >> CHUNK_SHIFT)) & (kpos < l_valid)
            masked = jnp.where(adm, score[:, i * tk:(i + 1) * tk], -jnp.inf)
            keys_ref[t + i] = to_key(masked)
            for blk in _lane_blocks(masked):
                top2 = jnp.maximum(top2, jnp.minimum(top1, blk))
                top1 = jnp.maximum(top1, blk)
        return top1, top2

    n_pair = n_kv // 2
    lows = jnp.full((tq, LANES), -jnp.inf, F32)
    tops = lax.fori_loop(0, n_pair, lambda i, c: score_tiles(2 * i, 2, c), (lows, lows))
    top1, top2 = lax.fori_loop(2 * n_pair, n_kv, lambda t, c: score_tiles(t, 1, c), tops)

    def from_key(k):
        return lax.bitcast_convert_type(k ^ ((k >> 31) & 0x7FFFFFFF), F32)

    def reduce_tiles(tile_fn, combine, init):
        def body(t, acc):
            return combine(acc, functools.reduce(combine, _lane_blocks(tile_fn(keys_ref[t], t))))
        return lax.fori_loop(0, n_kv, body, init)

    def count(pred):
        per_lane = reduce_tiles(lambda k, t: jnp.where(pred(k, t), 1.0, 0.0), jnp.add, jnp.zeros((tq, LANES), F32))
        return jnp.sum(per_lane, axis=-1, keepdims=True)

    def largest_at_most(bound):
        per_lane = reduce_tiles(lambda k, t: jnp.where(k <= bound, k, MASKED_KEY), jnp.maximum,
                                jnp.full((tq, LANES), MASKED_KEY, I32))
        return to_key(jnp.max(from_key(per_lane), axis=-1, keepdims=True))

    def halve(lo, hi, cnt_lo):
        gap = hi - lo
        mid = lo + lax.shift_right_logical(gap, 1) + (gap & 1)
        cnt = count(lambda k, t: k >= mid)
        keep = cnt >= topk
        return jnp.where(keep, mid, lo), jnp.where(keep, hi, mid - 1), jnp.where(keep, cnt, cnt_lo)

    def unfinished(lo, hi, cnt_lo):
        return (lo != hi) & (cnt_lo != topk)

    def any_of(flags):
        return jnp.max(jnp.where(flags, 1.0, 0.0))

    def any_wide(lo, hi, cnt_lo):
        wide = lax.shift_right_logical(hi - lo, NARROW_BITS) != 0
        return any_of(unfinished(lo, hi, cnt_lo) & wide)

    def halve_wide(state):
        lo, hi, cnt_lo = halve(*state[:3])
        return lo, hi, cnt_lo, any_wide(lo, hi, cnt_lo)

    def step_down(state):
        lo, hi, cnt_lo = state[:3]
        todo = unfinished(lo, hi, cnt_lo)
        top = largest_at_most(hi)
        cnt = count(lambda k, t: k >= top)
        hit = cnt >= topk
        lo = jnp.where(todo & hit, top, lo)
        hi = jnp.where(todo, jnp.where(hit, top, top - 1), hi)
        cnt_lo = jnp.where(todo & hit, cnt, cnt_lo)
        lo, hi, cnt_lo = halve(lo, hi, cnt_lo)
        return lo, hi, cnt_lo, any_of(unfinished(lo, hi, cnt_lo))

    lo0 = to_key(jnp.min(top2, axis=-1, keepdims=True))
    hi0 = to_key(jnp.max(top1, axis=-1, keepdims=True))
    unknown = jnp.full((tq, 1), -1.0, F32)
    going = lambda state: state[3] > 0.0
    lo, hi, cnt_lo, _ = lax.while_loop(going, halve_wide, (lo0, hi0, unknown, any_wide(lo0, hi0, unknown)))
    thr = lax.while_loop(going, step_down, (lo, hi, cnt_lo, any_of(unfinished(lo, hi, cnt_lo))))[0]
    n_gt = count(lambda k, t: k > thr)
    n_ge = count(lambda k, t: k >= thr)
    want_ties = topk - n_gt
    need = ((n_ge - n_gt) > want_ties) & (thr > MASKED_KEY)
    jlim_ref[...] = jnp.full((tq, LANES), NO_LIMIT, I32)

    @pl.when(jnp.max(jnp.where(need, 1.0, 0.0)) > 0.0)
    def _():
        def index_bit(it, j_lim):
            cand = j_lim | lax.shift_left(jnp.int32(1), index_bits - 1 - it)
            below = count(lambda k, t: (k == thr) & ((t * tk + col) < cand))
            return jnp.where(below <= want_ties - 1.0, cand, j_lim)
        j_lim = lax.fori_loop(0, index_bits, index_bit, jnp.zeros((tq, 1), I32))
        jlim_ref[...] = jnp.broadcast_to(jnp.where(need, j_lim, NO_LIMIT), (tq, LANES))

    j_lim = jlim_ref[:, 0:1]

    def bias_tile(t, carry):
        k = keys_ref[t]
        kpos = t * tk + col
        sel = (k > thr) | ((k == thr) & (kpos <= j_lim))
        sel = sel & (k > MASKED_KEY)
        bias_ref[t] = jnp.where(sel, 0.0, NEG)
        return carry

    lax.fori_loop(0, n_kv, bias_tile, 0)

    qk_scale = HEAD_DIM ** -0.5 * LOG2E
    n_full = jnp.minimum(q0, l_valid) // tk

    @pl.when((pl.program_id(0) == 0) & (pl.program_id(1) == 0))
    def _():
        relpos_ref[...] = _rel_pos(tq, WIDE_TILES * tk)

    for j in range(A_KV_HEADS):
        q4 = jnp.concatenate([qa_ref[0, :, (j * A_GROUP + g) * HEAD_DIM:(j * A_GROUP + g + 1) * HEAD_DIM]
                              for g in range(A_GROUP)], axis=0)
        slopes2 = [2.0 ** -(j * A_GROUP + g + 1) * LOG2E for g in range(A_GROUP)]
        slope_rows = jnp.concatenate([jnp.full((tq, LANES), s2, F32) for s2 in slopes2], axis=0)

        def tile_operands(t, n, j=j, q4=q4):
            start = pl.multiple_of(t * tk, tk)
            kt = k_ref[0, pl.ds(start, n * tk), j * HEAD_DIM:(j + 1) * HEAD_DIM]
            vt = v_ref[0, pl.ds(start, n * tk), j * HEAD_DIM:(j + 1) * HEAD_DIM]
            bias = bias_ref[t] if n == 1 else jnp.concatenate([bias_ref[t + i] for i in range(n)], axis=1)
            return start, _nt_dot(q4, kt), vt, bias

        def past_step(t, n, carry, slopes2=slopes2, slope_rows=slope_rows):
            start, s, vt, bias = tile_operands(t, n)
            lg = jnp.concatenate([s[g * tq:(g + 1) * tq] * qk_scale + (relpos_ref[:, :n * tk] * slopes2[g] + bias)
                                  for g in range(A_GROUP)], axis=0)
            off = slope_rows * (start - q0).astype(F32)
            return _softmax_step(lg, off, vt, carry)

        def edge_step(t, carry, slopes2=slopes2):
            start, s, vt, bias = tile_operands(t, 1)
            dist = jnp.abs(qpos - (start + col)).astype(F32)
            lg = jnp.concatenate([s[g * tq:(g + 1) * tq] * qk_scale - slopes2[g] * dist + bias
                                  for g in range(A_GROUP)], axis=0)
            return _softmax_step(lg, 0.0, vt, carry)

        out = _softmax_finish(_tile_loops(n_full, n_kv, past_step, edge_step,
                                          _softmax_init(A_GROUP * tq, HEAD_DIM)))
        for g in range(A_GROUP):
            h = j * A_GROUP + g
            o_ref[0, :, h * HEAD_DIM:(h + 1) * HEAD_DIM] = out[g * tq:(g + 1) * tq].astype(o_ref.dtype)


def _dsa_attention(zb, zf, keys, off, *, n_past, l_valid, tq, tk, topk, name):
    b, s, _ = zb.shape
    karr, kcol, varr, vcol, ikarr, ikcol = keys
    lp = karr.shape[1]
    n_tiles = lp // tk
    assert topk <= min(tk, 2 * LANES) and WIDE_TILES % 2 == 0
    kern = functools.partial(_dsa_kernel, tq=tq, tk=tk, n_past=n_past, l_valid=l_valid, topk=topk,
                             index_bits=lp.bit_length())
    blocks = (_nbytes((tq, A_Q + I_Q), BF16) + _nbytes((tq, LANES), F32) + 2 * _nbytes((lp, A_KV), BF16)
              + _nbytes((lp, LANES), BF16) + _nbytes((tq, A_Q), BF16) + _nbytes((tq, lp), F32)
              + 8 * _nbytes((A_GROUP * tq, tk), F32))
    return pl.pallas_call(
        kern,
        grid=(b, s // tq),
        in_specs=[pl.BlockSpec((1, tq, A_Q), lambda bi, qi: (bi, qi, off['aq'] // A_Q)),
                  pl.BlockSpec((1, tq, I_Q), lambda bi, qi: (bi, qi, off['iq'] // I_Q)),
                  pl.BlockSpec((1, tq, LANES), lambda bi, qi: (bi, qi, off['iw'] // LANES)),
                  pl.BlockSpec((1, lp, A_KV), lambda bi, qi: (bi, 0, kcol)),
                  pl.BlockSpec((1, lp, A_KV), lambda bi, qi: (bi, 0, vcol)),
                  pl.BlockSpec((1, lp, LANES), lambda bi, qi: (bi, 0, ikcol))],
        out_specs=pl.BlockSpec((1, tq, A_Q), lambda bi, qi: (bi, qi, 0)),
        out_shape=jax.ShapeDtypeStruct((b, s, A_Q), BF16),
        scratch_shapes=[pltpu.VMEM((n_tiles, tq, tk), I32), pltpu.VMEM((n_tiles, tq, tk), F32),
                        pltpu.VMEM((tq, LANES), I32), pltpu.VMEM((tq, WIDE_TILES * tk), F32)],
        compiler_params=_cparams(("arbitrary", "arbitrary"), blocks),
        name=name,
    )(zb, zb, zf, karr, varr, ikarr)


def _diff_kernel(q_ref, k_ref, v_ref, lq_ref, g_ref, o_ref, relbias_ref, *, tq, tk, n_past, l_valid, lam_init):
    h = pl.program_id(1)
    q0 = n_past + pl.program_id(2) * tq
    n_kv = _num_kv_tiles(q0, tq, tk, l_valid)
    row = lax.broadcasted_iota(I32, (tq, tk), 0)
    col = lax.broadcasted_iota(I32, (tq, tk), 1)
    qpos = q0 + row
    slope2 = jnp.float32(1.0)
    for hh in range(B_HEADS):
        slope2 = jnp.where(h == hh, jnp.float32(2.0 ** (-8.0 * (hh + 1) / B_HEADS) * LOG2E), slope2)
    qk_scale = HEAD_DIM ** -0.5 * LOG2E
    n_full = jnp.minimum(q0, l_valid) // tk

    @pl.when(pl.program_id(2) == 0)
    def _():
        relbias_ref[...] = _rel_pos(tq, WIDE_TILES * tk) * slope2

    qc = [q_ref[0, :, c * HEAD_DIM:(c + 1) * HEAD_DIM] for c in range(2)]

    def tile_operands(t, n):
        start = pl.multiple_of(t * tk, tk)
        s = [_nt_dot(qc[c], k_ref[0, pl.ds(start, n * tk), c * HEAD_DIM:(c + 1) * HEAD_DIM]) for c in range(2)]
        return start, s, v_ref[0, pl.ds(start, n * tk), :]

    def past_step(t, n, carry):
        start, s, vt = tile_operands(t, n)
        lg = jnp.concatenate([s[c] * qk_scale + relbias_ref[:, :n * tk] for c in range(2)], axis=0)
        return _softmax_step(lg, slope2 * (start - q0).astype(F32), vt, carry)

    def edge_step(t, carry):
        start, s, vt = tile_operands(t, 1)
        kpos = start + col
        adm = ((kpos >> CHUNK_SHIFT) <= (qpos >> CHUNK_SHIFT)) & (kpos < l_valid)
        bias = jnp.where(adm, -slope2 * jnp.abs(qpos - kpos).astype(F32), NEG)
        lg = jnp.concatenate([s[c] * qk_scale + bias for c in range(2)], axis=0)
        return _softmax_step(lg, 0.0, vt, carry)

    out = _softmax_finish(_tile_loops(n_full, n_kv, past_step, edge_step,
                                      _softmax_init(2 * tq, 2 * HEAD_DIM)))
    lq = lq_ref[...]
    lam = (jnp.exp(jnp.sum(lq[0:1] * lq[1:2], axis=-1, keepdims=True))
           - jnp.exp(jnp.sum(lq[2:3] * lq[3:4], axis=-1, keepdims=True)) + lam_init)
    o = out[:tq] - lam * out[tq:]
    y = o * lax.rsqrt(jnp.mean(o * o, axis=-1, keepdims=True) + SUBLN_EPS)
    y = (y * g_ref[...]) * (1.0 - lam_init)
    o_ref[0] = y.astype(o_ref.dtype)


def _diff_attention(zb, keys, lam_qk_l, g_subln_l, off, *, n_past, l_valid, tq, tk, lam_init, name):
    b, s, _ = zb.shape
    karr, kcol0, varr, vcol0 = keys
    lp = karr.shape[1]
    w = 2 * HEAD_DIM
    kern = functools.partial(_diff_kernel, tq=tq, tk=tk, n_past=n_past, l_valid=l_valid, lam_init=lam_init)
    blocks = (2 * _nbytes((tq, w), BF16) + 2 * _nbytes((lp, w), BF16) + 8 * _nbytes((2 * tq, tk), F32))
    return pl.pallas_call(
        kern,
        grid=(b, B_HEADS, s // tq),
        in_specs=[pl.BlockSpec((1, tq, w), lambda bi, h, qi: (bi, qi, off['bq'] // w + h)),
                  pl.BlockSpec((1, lp, w), lambda bi, h, qi: (bi, 0, kcol0 + h)),
                  pl.BlockSpec((1, lp, w), lambda bi, h, qi: (bi, 0, vcol0 + h)),
                  pl.BlockSpec((4, HEAD_DIM), lambda bi, h, qi: (0, 0)),
                  pl.BlockSpec((1, w), lambda bi, h, qi: (0, 0))],
        out_specs=pl.BlockSpec((1, tq, w), lambda bi, h, qi: (bi, qi, h)),
        out_shape=jax.ShapeDtypeStruct((b, s, B_V), BF16),
        scratch_shapes=[pltpu.VMEM((tq, WIDE_TILES * tk), F32)],
        compiler_params=_cparams(("arbitrary", "arbitrary", "arbitrary"), blocks),
        name=name,
    )(zb, karr, varr, lam_qk_l, g_subln_l.reshape(1, w))


def _row_gather(src_ref, idx_ref, idx_base, idx_stride, dst_ref, sem, n_rows):
    def row_copy(r):
        return pltpu.make_async_copy(src_ref.at[pl.ds(idx_ref[idx_base + r * idx_stride], 1)],
                                     dst_ref.at[pl.ds(r, 1)], sem)

    def start():
        lax.fori_loop(0, n_rows, lambda r, c: (row_copy(r).start(), c)[1], 0, unroll=ISSUE_UNROLL)

    def wait():
        pltpu.make_async_copy(dst_ref.at[pl.ds(0, n_rows)], dst_ref.at[pl.ds(0, n_rows)], sem).wait()

    return start, wait


def _moe_up_kernel(eid_ref, tok_ref, h_ref, wg_ref, wu_ref, o_ref, rows_ref, rows16_ref, sems, *, blk, n_blocks):
    i, j = pl.program_id(0), pl.program_id(1)
    slot = i % 2

    def gather(block, buf):
        return _row_gather(h_ref, tok_ref, block * blk, 1, rows_ref.at[buf], sems.at[buf], blk)

    @pl.when((i == 0) & (j == 0))
    def _():
        gather(0, 0)[0]()

    @pl.when(j == 0)
    def _():
        gather(i, slot)[1]()

        @pl.when(i + 1 < n_blocks)
        def _():
            gather(i + 1, 1 - slot)[0]()

        rows16_ref[...] = rows_ref[slot].astype(BF16)

    a = rows16_ref[...]
    u = jnp.dot(a, wg_ref[0], preferred_element_type=F32)
    v = jnp.dot(a, wu_ref[0], preferred_element_type=F32)
    o_ref[...] = ((u * jax.nn.sigmoid(u)) * v).astype(o_ref.dtype)


def _moe_up(h2, tok_buf, block_expert, w_gate, w_up, *, blk, name):
    p = tok_buf.shape[0]
    d = h2.shape[1]
    fp = w_gate.shape[-1]
    tn = _tile(fp, 512)
    n_blocks = p // blk
    blocks = (_nbytes((blk, d), F32) + _nbytes((blk, d), BF16) // 2 + 2 * _nbytes((d, tn), BF16)
              + _nbytes((blk, tn), BF16) + _nbytes((blk, tn), F32))
    return pl.pallas_call(
        functools.partial(_moe_up_kernel, blk=blk, n_blocks=n_blocks),
        grid_spec=pltpu.PrefetchScalarGridSpec(
            num_scalar_prefetch=2, grid=(n_blocks, fp // tn),
            in_specs=[pl.BlockSpec(memory_space=pl.ANY),
                      pl.BlockSpec((1, d, tn), lambda i, j, e, tok: (e[i], 0, j)),
                      pl.BlockSpec((1, d, tn), lambda i, j, e, tok: (e[i], 0, j))],
            out_specs=pl.BlockSpec((blk, tn), lambda i, j, e, tok: (i, j)),
            scratch_shapes=[pltpu.VMEM((2, blk, d), F32), pltpu.VMEM((blk, d), BF16),
                            pltpu.SemaphoreType.DMA((2,))]),
        out_shape=jax.ShapeDtypeStruct((p, fp), BF16),
        compiler_params=_cparams(("arbitrary", "arbitrary"), blocks),
        name=name,
    )(block_expert, tok_buf, h2, w_gate, w_up)


def _combine_kernel(dest_ref, x_ref, y_ref, rg_ref, gate_ref, o_ref, picked_ref, sems, *, tm, n_steps):
    i = pl.program_id(0)
    slot = i % 2

    def fetch(step, buf):
        base = step * tm * TOP_K_EXPERTS
        return [_row_gather(y_ref, dest_ref, base + k, TOP_K_EXPERTS, picked_ref.at[buf, k], sems.at[buf, k], tm)
                for k in range(TOP_K_EXPERTS)]

    @pl.when(i == 0)
    def _():
        for start, _ in fetch(0, 0):
            start()

    for _, wait in fetch(i, slot):
        wait()

    @pl.when(i + 1 < n_steps)
    def _():
        for start, _ in fetch(i + 1, 1 - slot):
            start()

    rg = rg_ref[...]
    f = rg[:, 0:1] * picked_ref[slot, 0] + rg[:, 1:2] * picked_ref[slot, 1]
    o_ref[...] = x_ref[...] + gate_ref[0] * f


def _moe_combine(x2, yb, dest, rgate, gate_f, seq, *, name):
    t, d = x2.shape
    tm = _tile(seq, 256)
    per_b = seq // tm
    return pl.pallas_call(
        functools.partial(_combine_kernel, tm=tm, n_steps=t // tm),
        grid_spec=pltpu.PrefetchScalarGridSpec(
            num_scalar_prefetch=1, grid=(t // tm,),
            in_specs=[pl.BlockSpec((tm, d), lambda i, dst: (i, 0)),
                      pl.BlockSpec(memory_space=pl.ANY),
                      pl.BlockSpec((tm, LANES), lambda i, dst: (i, 0)),
                      pl.BlockSpec((1, 1, d), lambda i, dst: (i // per_b, 0, 0))],
            out_specs=pl.BlockSpec((tm, d), lambda i, dst: (i, 0)),
            scratch_shapes=[pltpu.VMEM((2, TOP_K_EXPERTS, tm, d), F32),
                            pltpu.SemaphoreType.DMA((2, TOP_K_EXPERTS))]),
        out_shape=jax.ShapeDtypeStruct((t, d), F32),
        compiler_params=_cparams(("arbitrary",), 4 * _nbytes((tm, d), F32)),
        name=name,
    )(dest, x2, yb, rgate, gate_f)


STATE_COLS = (('ak', A_KV), ('av', A_KV), ('ik', LANES), ('bk', B_QK), ('bv', B_V))


def _state_tails():
    return ((A_KV_HEADS, HEAD_DIM), (A_KV_HEADS, HEAD_DIM), (IDX_DIM,), (B_HEADS, 2, HEAD_DIM),
            (B_HEADS, 2 * HEAD_DIM))


def _state_kernel(*refs, depth):
    n = len(STATE_COLS)
    ins, outs = refs[:depth * n], refs[depth * n:]
    for l in range(depth):
        @pl.when(pl.program_id(0) == l)
        def _(l=l):
            ak, av, ik, bk, bv = ins[l * n:(l + 1) * n]
            for j in range(A_KV_HEADS):
                outs[0][0, 0, :, j, :] = ak[0, :, j * HEAD_DIM:(j + 1) * HEAD_DIM]
                outs[1][0, 0, :, j, :] = av[0, :, j * HEAD_DIM:(j + 1) * HEAD_DIM]
            outs[2][0, 0] = ik[0, :, :IDX_DIM]
            for h in range(B_HEADS):
                for c in range(2):
                    lo = (2 * h + c) * HEAD_DIM
                    outs[3][0, 0, :, h, c, :] = bk[0, :, lo:lo + HEAD_DIM]
                outs[4][0, 0, :, h, :] = bv[0, :, 2 * h * HEAD_DIM:2 * (h + 1) * HEAD_DIM]


def _state_rows(zf3_layers, off, *, name):
    depth = len(zf3_layers)
    b, s, _ = zf3_layers[0].shape
    ts = _tile(s, 256)
    in_specs, args = [], []
    for l in range(depth):
        for cname, width in STATE_COLS:
            blk = off[cname] // width
            in_specs.append(pl.BlockSpec(
                (1, ts, width),
                lambda li, bi, i, l=l, blk=blk: (jnp.where(li == l, bi, 0), jnp.where(li == l, i, 0), blk)))
            args.append(zf3_layers[l])
    tails = _state_tails()
    out_specs = [pl.BlockSpec((1, 1, ts) + t, lambda li, bi, i, nt=len(t): (li, bi, i) + (0,) * nt) for t in tails]
    out_shape = [jax.ShapeDtypeStruct((depth, b, s) + t, F32) for t in tails]
    blocks = (depth + 1) * sum(_nbytes((ts, w), F32) for _, w in STATE_COLS)
    return pl.pallas_call(
        functools.partial(_state_kernel, depth=depth),
        grid=(depth, b, s // ts), in_specs=in_specs, out_specs=out_specs, out_shape=out_shape,
        compiler_params=_cparams(("arbitrary", "arbitrary", "arbitrary"), blocks), name=name,
    )(*args)


def _pack_kernel(c_hbm, new_ref, o_ref, buf, sems, *, layer, tk, n_cached_tiles, tail):
    b, i = pl.program_id(0), pl.program_id(1)
    slot = i % 2
    piece = tail[-1]
    heads = list(itertools.product(*[range(t) for t in tail[:-1]]))

    def tile_copies(tile, s):
        rows = pl.ds(tile * tk, tk)
        return [pltpu.make_async_copy(c_hbm.at[(layer, b, rows) + idx + (slice(None),)], buf.at[s, n], sems.at[s, n])
                for n, idx in enumerate(heads)]

    @pl.when(i == 0)
    def _():
        for c in tile_copies(0, 0):
            c.start()

    @pl.when(i < n_cached_tiles)
    def _():
        for c in tile_copies(i, slot):
            c.wait()

        @pl.when(i + 1 < n_cached_tiles)
        def _():
            for c in tile_copies(i + 1, 1 - slot):
                c.start()

        for n in range(len(heads)):
            o_ref[0, :, n * piece:(n + 1) * piece] = buf[slot, n].astype(BF16)

    @pl.when(i >= n_cached_tiles)
    def _():
        o_ref[0] = new_ref[0]


def _pack_keys(cache, layer, new_rows, *, tk, name):
    _, b, p = cache.shape[:3]
    tail = cache.shape[3:]
    w = math.prod(tail)
    assert p % tk == 0 and new_rows.shape == (b, tk, w)
    n_cached_tiles = p // tk
    n_heads = w // tail[-1]
    return pl.pallas_call(
        functools.partial(_pack_kernel, layer=layer, tk=tk, n_cached_tiles=n_cached_tiles, tail=tail),
        grid=(b, n_cached_tiles + 1),
        in_specs=[pl.BlockSpec(memory_space=pl.ANY),
                  pl.BlockSpec((1, tk, w), lambda bi, i: (bi, 0, 0))],
        out_specs=pl.BlockSpec((1, tk, w), lambda bi, i: (bi, i, 0)),
        out_shape=jax.ShapeDtypeStruct((b, p + tk, w), BF16),
        scratch_shapes=[pltpu.VMEM((2, n_heads, tk, tail[-1]), F32), pltpu.SemaphoreType.DMA((2, n_heads))],
        compiler_params=_cparams(("arbitrary", "arbitrary"), _nbytes((tk, w), F32) + 2 * _nbytes((tk, w), BF16)),
        name=name,
    )(cache, new_rows)


def _silu_mul(accs, extras):
    u, v = accs
    return [(u * jax.nn.sigmoid(u)) * v]


def _moe_ffn(x2, h2, ridx, rgate, gate_f, seq, w_gate, w_up, w_down, *, tag):
    t, d = x2.shape
    fp = w_gate.shape[-1]
    a = t * TOP_K_EXPERTS
    blk = _tile(a, 512)
    n_blocks = -(-(a + N_EXPERTS * (blk - 1)) // blk)
    p = n_blocks * blk
    e_flat = ridx[:, :TOP_K_EXPERTS].reshape(-1)
    onehot = (e_flat[:, None] == jnp.arange(N_EXPERTS, dtype=I32)[None, :]).astype(I32)
    rank = jnp.sum((jnp.cumsum(onehot, axis=0) - onehot) * onehot, axis=1)
    counts = jnp.sum(onehot, axis=0)
    padded = ((counts + blk - 1) // blk) * blk
    pends = jnp.cumsum(padded)
    pstarts = pends - padded
    dest = (pstarts[e_flat] + rank).astype(I32)
    tok_buf = jnp.zeros((p,), I32).at[dest].set(jnp.arange(a, dtype=I32) // TOP_K_EXPERTS)
    block_expert = jnp.clip(jnp.searchsorted(pends, jnp.arange(n_blocks, dtype=I32) * blk, side='right'),
                            0, N_EXPERTS - 1).astype(I32)

    act = _moe_up(h2, tok_buf, block_expert, w_gate, w_up, blk=blk, name=f"moe_up_{tag}")
    (yb,) = _matmul([act], [w_down], [0], [], lambda accs, ex: accs, [F32], d, tm=blk, tn=_tile(d, 512),
                    eids=block_expert, name=f"moe_down_{tag}")
    return _moe_combine(x2, yb, dest, rgate, gate_f, seq, name=f"moe_combine_{tag}")


def _trunk(x, mods, past_all, wts, tag):
    b, s, d = x.shape
    t = b * s
    depth = len(wts['w_in'])
    off, nz = _z_layout(d)
    x2 = x.reshape(t, d)
    tn_d = _tile(d, 512)
    tm_b = _tile(s, 512)
    tm_f = _tile(t, 1024)
    per_b = s // tm_b
    fuse_norm = s % tm_f == 0
    zf3_layers = []
    for l in range(depth):
        shift_a, scale_a, gate_a, shift_f, scale_f, gate_f = [
            mods[l][:, i * d:(i + 1) * d].reshape(b, 1, d) for i in range(N_ADA)]
        batch_tile = lambda i, j, *_: (i // per_b, 0, j)

        if fuse_norm:
            h, norm_a = x2, (wts['g_attn'][l], scale_a, shift_a, s)
        else:
            h, norm_a = _norm(x2, wts['g_attn'][l], s, scale=scale_a, shift=shift_a, out_dtype=BF16,
                              name=f"norm_attn_{tag}{l}"), None
        zf, zb = _matmul([h], [wts['w_in'][l]], [0], [], lambda accs, ex: [accs[0], accs[0]], [F32, BF16], nz,
                         tm=tm_f, tn=IN_PROJ_TN, norm=norm_a, name=f"in_proj_{tag}{l}")
        zf3, zb3 = zf.reshape(b, s, nz), zb.reshape(b, s, nz)
        cut = lambda z, name, width: z[:, :, off[name]:off[name] + width]
        zf3_layers.append(zf3)

        if past_all is None:
            n_past, l_valid = 0, s
            tk = _tile(s, 512)
            tq_a, tq_b = _tile(s, 128), _tile(s, 256)
            dsa_keys = (zb3, off['ak'] // A_KV, zb3, off['av'] // A_KV, zb3, off['ik'] // LANES)
            diff_keys = (zb3, off['bk'] // (2 * HEAD_DIM), zb3, off['bv'] // (2 * HEAD_DIM))
        else:
            n_past = past_all[0].shape[2]
            l_valid = n_past + s
            tk = 512
            tq_a = tq_b = s

            def new_rows(name, width):
                return jnp.concatenate([cut(zb3, name, width), jnp.zeros((b, tk - s, width), BF16)], axis=1)

            def with_past(ci, name, width):
                return _pack_keys(past_all[ci], l, new_rows(name, width), tk=tk, name=f"pack_{name}_{tag}{l}")

            ik_past = past_all[2][l].astype(BF16)
            ik_all = jnp.concatenate([jnp.concatenate([ik_past, ik_past], axis=-1), new_rows('ik', LANES)], axis=1)
            dsa_keys = (with_past(0, 'ak', A_KV), 0, with_past(1, 'av', A_KV), 0, ik_all, 0)
            diff_keys = (with_past(3, 'bk', B_QK), 0, with_past(4, 'bv', B_V), 0)
        topk = min(TOPK_MAX, l_valid // 4)
        oa = _dsa_attention(zb3, zf3, dsa_keys, off, n_past=n_past, l_valid=l_valid, tq=tq_a, tk=tk,
                            topk=topk, name=f"dsa_{tag}{l}")
        lam_init = 0.8 - 0.6 * math.exp(-0.3 * l)
        ob = _diff_attention(zb3, diff_keys, wts['lam_qk'][l], wts['g_subln'][l], off, n_past=n_past,
                             l_valid=l_valid, tq=tq_b, tk=tk, lam_init=lam_init, name=f"diff_{tag}{l}")

        def merge_epi(accs, ex):
            return [jax.nn.sigmoid(ex[0]) * accs[0] + jax.nn.sigmoid(ex[1]) * accs[1]]

        (merged,) = _matmul(
            [oa.reshape(t, A_Q), ob.reshape(t, B_V)], [wts['w_out_a'][l], wts['w_out_b'][l]], [0, 1],
            [(zf, (tm_f, tn_d), lambda i, j, *_: (i, off['ga'] // tn_d + j)),
             (zf, (tm_f, tn_d), lambda i, j, *_: (i, off['gb'] // tn_d + j))],
            merge_epi, [BF16], d, tm=tm_f, tn=tn_d, name=f"merge_{tag}{l}")

        def resid_epi(accs, ex):
            return [ex[0] + ex[1][0] * accs[0]]

        (x2,) = _matmul([merged], [wts['w_out'][l]], [0],
                        [(x2, (tm_b, tn_d), lambda i, j, *_: (i, j)), (gate_a, (1, 1, tn_d), batch_tile)],
                        resid_epi, [F32], d, tm=tm_b, tn=tn_d, name=f"out_proj_{tag}{l}")

        if l % 2 == 0:
            i = l // 2
            if fuse_norm:
                h, norm_f = x2, (wts['g_ffn'][l], scale_f, shift_f, s)
            else:
                h, norm_f = _norm(x2, wts['g_ffn'][l], s, scale=scale_f, shift=shift_f, out_dtype=BF16,
                                  name=f"norm_ffn_{tag}{l}"), None
            fp = wts['w_ff_gate'][i].shape[-1]
            (act,) = _matmul([h], [wts['w_ff_gate'][i], wts['w_ff_up'][i]], [0, 0], [], _silu_mul, [BF16], fp,
                             tm=tm_f, tn=_tile(fp, 512), norm=norm_f, name=f"ffn_up_{tag}{l}")
            (x2,) = _matmul([act], [wts['w_ff_down'][i]], [0],
                            [(x2, (tm_b, tn_d), lambda i, j, *_: (i, j)), (gate_f, (1, 1, tn_d), batch_tile)],
                            resid_epi, [F32], d, tm=tm_b, tn=tn_d, name=f"ffn_down_{tag}{l}")
        else:
            i = l // 2
            h2, ridx, rgate = _norm(x2, wts['g_ffn'][l], s, scale=scale_f, shift=shift_f,
                                    router=(wts['w_router'][i], wts['b_router'][i]), out_dtype=F32,
                                    name=f"norm_router_{tag}{l}")
            x2 = _moe_ffn(x2, h2, ridx, rgate, gate_f, s, wts['w_moe_gate'][i], wts['w_moe_up'][i],
                          wts['w_moe_down'][i], tag=f"{tag}{l}")
    y = _norm(x2, wts['g_final'], s, out_dtype=F32, name=f"norm_final_{tag}")
    state = tuple(_state_rows(zf3_layers, off, name=f"state_rows_{tag}"))
    return y.reshape(b, s, d), state


def _prep_weights(w_in, w_out_a, w_out_b, w_out, w_ff_gate, w_ff_up, w_ff_down, w_router, b_router,
                  w_moe_gate, w_moe_up, w_moe_down, d_model):
    f = w_ff_gate.shape[-1]
    fp = _round_up(f, 512)
    pad_cols = lambda w: jnp.pad(w, [(0, 0)] * (w.ndim - 1) + [(0, fp - f)]).astype(BF16)
    pad_rows = lambda w: jnp.pad(w, [(0, 0)] * (w.ndim - 2) + [(0, fp - f), (0, 0)]).astype(BF16)
    n_moe = w_router.shape[0]
    return dict(
        w_in=[_prep_w_in(w_in[l], d_model) for l in range(w_in.shape[0])],
        w_out_a=w_out_a.astype(BF16), w_out_b=w_out_b.astype(BF16), w_out=w_out.astype(BF16),
        w_ff_gate=pad_cols(w_ff_gate), w_ff_up=pad_cols(w_ff_up), w_ff_down=pad_rows(w_ff_down),
        w_router=[jnp.pad(w_router[i], ((0, 0), (0, LANES - N_EXPERTS))) for i in range(n_moe)],
        b_router=[jnp.pad(b_router[i], (0, LANES - N_EXPERTS)).reshape(1, LANES) for i in range(n_moe)],
        w_moe_gate=pad_cols(w_moe_gate), w_moe_up=pad_cols(w_moe_up), w_moe_down=pad_rows(w_moe_down),
    )


def kernel(x_prompt, x_sample, c_prompt, c_sample, cache_dsa_k, cache_dsa_v, cache_idx_k, cache_diff_k, cache_diff_v, w_ada, b_ada, g_attn, w_in, w_out_a, w_out_b, w_out, lam_qk, g_subln, g_ffn, w_ff_gate, w_ff_up, w_ff_down, w_router, b_router, w_moe_gate, w_moe_up, w_moe_down, g_final):
    d = x_prompt.shape[-1]
    wts = _prep_weights(w_in, w_out_a, w_out_b, w_out, w_ff_gate, w_ff_up, w_ff_down, w_router, b_router,
                        w_moe_gate, w_moe_up, w_moe_down, d)
    wts.update(g_attn=g_attn, g_ffn=g_ffn, g_final=g_final, lam_qk=lam_qk, g_subln=g_subln)
    nb_p = c_prompt.shape[0]
    mods = _ada_mod(jnp.concatenate([c_prompt, c_sample], axis=0), w_ada, b_ada)
    y_p, st_p = _trunk(x_prompt, mods[:, :nb_p], None, wts, "p")
    past_all = (cache_dsa_k, cache_dsa_v, cache_idx_k, cache_diff_k, cache_diff_v)
    y_s, st_s = _trunk(x_sample, mods[:, nb_p:], past_all, wts, "s")
    return (y_p, y_s) + st_p + st_s
```

```python
import functools
import itertools
import math

import jax
import jax.numpy as jnp
from jax import lax
from jax.experimental import pallas as pl
from jax.experimental.pallas import tpu as pltpu

F32, BF16, I32 = jnp.float32, jnp.bfloat16, jnp.int32

CHUNK = 64
CHUNK_SHIFT = CHUNK.bit_length() - 1
HEAD_DIM = 128
A_HEADS = 8
A_KV_HEADS = 2
A_GROUP = A_HEADS // A_KV_HEADS
IDX_HEADS = 8
IDX_DIM = 64
TOPK_MAX = 256
B_HEADS = 4
N_EXPERTS = 8
TOP_K_EXPERTS = 2
N_ADA = 6
NORM_EPS = 1e-6
SUBLN_EPS = 1e-5
A_Q = A_HEADS * HEAD_DIM
A_KV = A_KV_HEADS * HEAD_DIM
I_Q = IDX_HEADS * IDX_DIM
B_QK = B_HEADS * 2 * HEAD_DIM
B_V = B_HEADS * 2 * HEAD_DIM

LANES = 128
VMEM_BYTES_V7X = 64 * 2 ** 20
NEG = -1e30
NO_LIMIT = 2 ** 30
LOG2E = math.log2(math.e)
IN_PROJ_TN = 512
MASKED_KEY = 0x807FFFFF - 2 ** 32
WIDE_TILES = 4
ISSUE_UNROLL = 8
NARROW_BITS = 12


def _cparams(dims, block_bytes):
    limit = min(max(2 * int(block_bytes) + (8 << 20), 32 << 20), VMEM_BYTES_V7X - (6 << 20))
    return pltpu.CompilerParams(dimension_semantics=dims, vmem_limit_bytes=limit)


def _nbytes(shape, dtype):
    return math.prod(shape) * jnp.dtype(dtype).itemsize


def _tile(n, pref):
    if n <= pref:
        return n
    t = pref
    while n % t:
        t //= 2
    return t


def _round_up(n, m):
    return -(-n // m) * m


def _z_layout(d_model):
    off, o = {}, 0
    for name, n in (('aq', A_Q), ('ak', A_KV), ('av', A_KV), ('iq', I_Q), ('bq', B_QK), ('bk', B_QK),
                    ('bv', B_V), ('ga', d_model), ('gb', d_model), ('ik', LANES), ('iw', LANES)):
        off[name] = o
        o += n
    return off, _round_up(o, IN_PROJ_TN)


def _prep_w_in(w, d_model):
    sizes = (A_Q, A_KV, A_KV, I_Q, IDX_DIM, IDX_HEADS, B_QK, B_QK, B_V, d_model, d_model)
    names = ('aq', 'ak', 'av', 'iq', 'ik', 'iw', 'bq', 'bk', 'bv', 'ga', 'gb')
    parts, o = {}, 0
    for n, s in zip(names, sizes):
        parts[n] = w[:, o:o + s]
        o += s
    cols = [parts[n] for n in ('aq', 'ak', 'av', 'iq', 'bq', 'bk', 'bv', 'ga', 'gb')]
    cols += [parts['ik'], parts['ik'], parts['iw']]
    used = sum(c.shape[1] for c in cols)
    cols.append(jnp.zeros((w.shape[0], _z_layout(d_model)[1] - used), w.dtype))
    return jnp.concatenate(cols, axis=1).astype(BF16)


def _ada_kernel(c_ref, w_ref, b_ref, o_ref):
    c = c_ref[...]
    s = c * jax.nn.sigmoid(c)
    o_ref[0] = jnp.dot(s.astype(BF16), w_ref[0].astype(BF16), preferred_element_type=F32) + b_ref[0]


def _ada_mod(c_all, w_ada, b_ada):
    depth, d, n = w_ada.shape
    nb = c_all.shape[0]
    tn = _tile(n, 1024)
    blocks = _nbytes((nb, d), F32) + _nbytes((d, tn), F32) * 2 + _nbytes((nb, tn), F32)
    return pl.pallas_call(
        _ada_kernel,
        grid=(depth, n // tn),
        in_specs=[pl.BlockSpec((nb, d), lambda l, j: (0, 0)),
                  pl.BlockSpec((1, d, tn), lambda l, j: (l, 0, j)),
                  pl.BlockSpec((1, 1, tn), lambda l, j: (l, 0, j))],
        out_specs=pl.BlockSpec((1, nb, tn), lambda l, j: (l, 0, j)),
        out_shape=jax.ShapeDtypeStruct((depth, nb, n), F32),
        compiler_params=_cparams(("arbitrary", "arbitrary"), blocks),
        name="ada_mod",
    )(c_all, w_ada, b_ada.reshape(depth, 1, n))


def _norm_kernel(*refs, modulated, router, eps):
    refs = list(refs)
    x_ref, g_ref = refs[:2]
    pos = 2
    x = x_ref[...]
    y = x * lax.rsqrt(jnp.mean(x * x, axis=-1, keepdims=True) + eps)
    y = y * g_ref[...]
    if modulated:
        scale_ref, shift_ref = refs[pos:pos + 2]
        pos += 2
        y = y * (1.0 + scale_ref[0]) + shift_ref[0]
    if router:
        wr_ref, br_ref = refs[pos:pos + 2]
        pos += 2
    o_ref = refs[pos]
    o_ref[...] = y.astype(o_ref.dtype)
    if router:
        idx_ref, gate_ref = refs[pos + 1:pos + 3]
        logits = jnp.dot(y, wr_ref[...], preferred_element_type=F32,
                         precision=lax.Precision.HIGHEST) + br_ref[...]
        lane = lax.broadcasted_iota(I32, logits.shape, 1)
        logits = jnp.where(lane < N_EXPERTS, logits, -jnp.inf)
        m1 = jnp.max(logits, axis=-1, keepdims=True)
        i1 = jnp.min(jnp.where(logits == m1, lane, LANES), axis=-1, keepdims=True)
        rest = jnp.where(lane == i1, -jnp.inf, logits)
        m2 = jnp.max(rest, axis=-1, keepdims=True)
        i2 = jnp.min(jnp.where(rest == m2, lane, LANES), axis=-1, keepdims=True)
        e = jnp.exp(m2 - m1)
        g1 = 1.0 / (1.0 + e)
        g2 = e / (1.0 + e)
        idx_ref[...] = jnp.where(lane == 0, i1, jnp.where(lane == 1, i2, 0))
        gate_ref[...] = jnp.where(lane == 0, g1, jnp.where(lane == 1, g2, 0.0))


def _norm(x2, g, seq, *, scale=None, shift=None, router=None, out_dtype, eps=NORM_EPS, name):
    t, d = x2.shape
    tm = _tile(seq, 256)
    per_b = seq // tm
    in_specs = [pl.BlockSpec((tm, d), lambda i: (i, 0)), pl.BlockSpec((1, d), lambda i: (0, 0))]
    args = [x2, g.reshape(1, d)]
    if scale is not None:
        in_specs += [pl.BlockSpec((1, 1, d), lambda i: (i // per_b, 0, 0))] * 2
        args += [scale, shift]
    out_specs = [pl.BlockSpec((tm, d), lambda i: (i, 0))]
    out_shape = [jax.ShapeDtypeStruct((t, d), out_dtype)]
    if router is not None:
        w_r, b_r = router
        in_specs += [pl.BlockSpec((d, LANES), lambda i: (0, 0)), pl.BlockSpec((1, LANES), lambda i: (0, 0))]
        args += [w_r, b_r]
        out_specs += [pl.BlockSpec((tm, LANES), lambda i: (i, 0))] * 2
        out_shape += [jax.ShapeDtypeStruct((t, LANES), I32), jax.ShapeDtypeStruct((t, LANES), F32)]
    blocks = 3 * _nbytes((tm, d), F32) + _nbytes((d, LANES), F32)
    outs = pl.pallas_call(
        functools.partial(_norm_kernel, modulated=scale is not None, router=router is not None, eps=eps),
        grid=(t // tm,), in_specs=in_specs, out_specs=out_specs, out_shape=out_shape,
        compiler_params=_cparams(("arbitrary",), blocks), name=name,
    )(*args)
    return outs if router is not None else outs[0]


def _mm_kernel(*refs, n_a, a_of_w, n_extra, epi, has_eids, normed):
    refs = list(refs)
    if has_eids:
        refs = refs[1:]
    n_w = len(a_of_w)
    a_refs = refs[:n_a]
    w_refs = refs[n_a:n_a + n_w]
    pos = n_a + n_w
    if normed:
        g_ref, scale_ref, shift_ref = refs[pos:pos + 3]
        pos += 3
        h_ref = refs.pop()

        @pl.when(pl.program_id(1) == 0)
        def _():
            x = a_refs[0][...]
            y = x * lax.rsqrt(jnp.mean(x * x, axis=-1, keepdims=True) + NORM_EPS)
            y = y * g_ref[...]
            h_ref[...] = (y * (1.0 + scale_ref[0]) + shift_ref[0]).astype(BF16)

        a_vals = [h_ref[...]]
    else:
        a_vals = [a[...].astype(BF16) for a in a_refs]
    e_refs = refs[pos:pos + n_extra]
    o_refs = refs[pos + n_extra:]
    accs = []
    for ai, w_ref in zip(a_of_w, w_refs):
        w = w_ref[0] if len(w_ref.shape) == 3 else w_ref[...]
        accs.append(jnp.dot(a_vals[ai], w, preferred_element_type=F32))
    outs = epi(accs, [e[...] for e in e_refs])
    for o_ref, o in zip(o_refs, outs):
        o_ref[...] = o.astype(o_ref.dtype)


def _matmul(a_list, w_list, a_of_w, extras, epi, out_dtypes, n_cols, *, tm, tn, eids=None, norm=None, name):
    m = a_list[0].shape[0]
    in_specs, blocks, scratch = [], 0, []
    for a in a_list:
        in_specs.append(pl.BlockSpec((tm, a.shape[1]), lambda i, j, *_: (i, 0)))
        blocks += _nbytes((tm, a.shape[1]), a.dtype)
    for w in w_list:
        if w.ndim == 2:
            in_specs.append(pl.BlockSpec((w.shape[0], tn), lambda i, j, *_: (0, j)))
        else:
            in_specs.append(pl.BlockSpec((1, w.shape[1], tn), lambda i, j, e: (e[i], 0, j)))
        blocks += _nbytes((w.shape[-2], tn), w.dtype)
    norm_args = []
    if norm is not None:
        g, scale, shift, seq = norm
        k = a_list[0].shape[1]
        assert len(a_list) == 1 and seq % tm == 0
        per_b = seq // tm
        in_specs += [pl.BlockSpec((1, k), lambda i, j, *_: (0, 0))]
        in_specs += [pl.BlockSpec((1, 1, k), lambda i, j, *_: (i // per_b, 0, 0))] * 2
        norm_args = [g.reshape(1, k), scale, shift]
        scratch = [pltpu.VMEM((tm, k), BF16)]
        blocks += _nbytes((tm, k), BF16)
    for arr, bs, im in extras:
        in_specs.append(pl.BlockSpec(bs, im))
        blocks += _nbytes(bs, arr.dtype)
    out_specs = [pl.BlockSpec((tm, tn), lambda i, j, *_: (i, j)) for _ in out_dtypes]
    out_shape = [jax.ShapeDtypeStruct((m, n_cols), dt) for dt in out_dtypes]
    blocks += sum(_nbytes((tm, tn), dt) for dt in out_dtypes) + len(w_list) * _nbytes((tm, tn), F32)
    kern = functools.partial(_mm_kernel, n_a=len(a_list), a_of_w=tuple(a_of_w), n_extra=len(extras),
                             epi=epi, has_eids=eids is not None, normed=norm is not None)
    grid_spec = pltpu.PrefetchScalarGridSpec(
        num_scalar_prefetch=0 if eids is None else 1, grid=(m // tm, n_cols // tn),
        in_specs=in_specs, out_specs=out_specs, scratch_shapes=scratch)
    args = (([] if eids is None else [eids]) + list(a_list) + list(w_list) + norm_args
            + [e[0] for e in extras])
    outs = pl.pallas_call(kern, grid_spec=grid_spec, out_shape=out_shape,
                          compiler_params=_cparams(("arbitrary", "arbitrary"), blocks), name=name)(*args)
    return outs


def _nt_dot(a, b):
    return lax.dot_general(a, b, (((1,), (1,)), ((), ())), preferred_element_type=F32)


def _lane_blocks(x):
    return [x[:, c * LANES:(c + 1) * LANES] for c in range(x.shape[1] // LANES)]


def _softmax_step(lg, off, vt, carry):
    m, l, acc = carry
    rows, width = lg.shape
    block_max = functools.reduce(jnp.maximum, _lane_blocks(lg))
    row_max = jnp.broadcast_to(jnp.max(block_max, axis=-1, keepdims=True), (rows, LANES))
    m_new = jnp.maximum(m, row_max + off)
    alpha = jnp.exp2(m - m_new)
    p = jnp.exp2(lg - jnp.tile(m_new - off, (1, width // LANES)))
    l = alpha * l + functools.reduce(jnp.add, _lane_blocks(p))
    acc = (jnp.tile(alpha, (1, acc.shape[1] // LANES)) * acc
           + jnp.dot(p.astype(BF16), vt, preferred_element_type=F32))
    return m_new, l, acc


def _softmax_init(rows, width):
    return (jnp.full((rows, LANES), NEG, F32), jnp.zeros((rows, LANES), F32), jnp.zeros((rows, width), F32))


def _softmax_finish(carry):
    _, l, acc = carry
    return acc / jnp.sum(l, axis=-1, keepdims=True)


def _rel_pos(rows, cols):
    return (lax.broadcasted_iota(I32, (rows, cols), 1) - lax.broadcasted_iota(I32, (rows, cols), 0)).astype(F32)


def _tile_loops(n_full, n_kv, past_step, edge_step, carry):
    n_wide = n_full // WIDE_TILES
    carry = lax.fori_loop(0, n_wide, lambda i, c: past_step(i * WIDE_TILES, WIDE_TILES, c), carry)
    done = n_wide * WIDE_TILES
    n_pair = (n_full - done) // 2
    carry = lax.fori_loop(0, n_pair, lambda i, c: past_step(done + 2 * i, 2, c), carry)
    carry = lax.fori_loop(done + 2 * n_pair, n_full, lambda t, c: past_step(t, 1, c), carry)
    return lax.fori_loop(n_full, n_kv, edge_step, carry)


def _num_kv_tiles(q0, tq, tk, l_valid):
    kmax = jnp.minimum(l_valid, (((q0 + tq - 1) >> CHUNK_SHIFT) + 1) * CHUNK)
    return (kmax + tk - 1) // tk


def _dsa_kernel(qa_ref, iq_ref, iw_ref, k_ref, v_ref, ik_ref, o_ref, keys_ref, bias_ref, jlim_ref, relpos_ref, *,
                tq, tk, n_past, l_valid, topk, index_bits):
    q0 = n_past + pl.program_id(1) * tq
    n_kv = _num_kv_tiles(q0, tq, tk, l_valid)
    row = lax.broadcasted_iota(I32, (tq, tk), 0)
    col = lax.broadcasted_iota(I32, (tq, tk), 1)
    qpos = q0 + row
    lane = lax.broadcasted_iota(I32, (tq, LANES), 1)

    iw = iw_ref[0] * (I_Q ** -0.5)
    iq = iq_ref[0].astype(F32)
    iq_heads = []
    for h in range(IDX_HEADS):
        pair = iq[:, (h // 2) * LANES:(h // 2 + 1) * LANES]
        keep = (lane < IDX_DIM) if h % 2 == 0 else (lane >= IDX_DIM)
        iq_heads.append(jnp.where(keep, pair, 0.0).astype(BF16))
    iq_all = jnp.concatenate(iq_heads, axis=0)

    def to_key(x):
        bits = lax.bitcast_convert_type(x, I32)
        return bits ^ ((bits >> 31) & 0x7FFFFFFF)

    def score_tiles(t, n, carry):
        top1, top2 = carry
        start = pl.multiple_of(t * tk, tk)
        ikt = ik_ref[0, pl.ds(start, n * tk), :]
        rel = jnp.maximum(_nt_dot(iq_all, ikt), 0.0)
        score = jnp.zeros((tq, n * tk), F32)
        for h in range(IDX_HEADS):
            score = score + iw[:, h:h + 1] * rel[h * tq:(h + 1) * tq]
        for i in range(n):
            kpos = start + i * tk + col
            adm = ((kpos >> CHUNK_SHIFT) <= (qpos >> CHUNK_SHIFT)) & (kpos < l_valid)
            masked = jnp.where(adm, score[:, i * tk:(i + 1) * tk], -jnp.inf)
            keys_ref[t + i] = to_key(masked)
            for blk in _lane_blocks(masked):
                top2 = jnp.maximum(top2, jnp.minimum(top1, blk))
                top1 = jnp.maximum(top1, blk)
        return top1, top2

    n_pair = n_kv // 2
    lows = jnp.full((tq, LANES), -jnp.inf, F32)
    tops = lax.fori_loop(0, n_pair, lambda i, c: score_tiles(2 * i, 2, c), (lows, lows))
    top1, top2 = lax.fori_loop(2 * n_pair, n_kv, lambda t, c: score_tiles(t, 1, c), tops)

    def from_key(k):
        return lax.bitcast_convert_type(k ^ ((k >> 31) & 0x7FFFFFFF), F32)

    def reduce_tiles(tile_fn, combine, init):
        def body(t, acc):
            return combine(acc, functools.reduce(combine, _lane_blocks(tile_fn(keys_ref[t], t))))
        return lax.fori_loop(0, n_kv, body, init)

    def count(pred):
        per_lane = reduce_tiles(lambda k, t: jnp.where(pred(k, t), 1.0, 0.0), jnp.add, jnp.zeros((tq, LANES), F32))
        return jnp.sum(per_lane, axis=-1, keepdims=True)

    def largest_at_most(bound):
        per_lane = reduce_tiles(lambda k, t: jnp.where(k <= bound, k, MASKED_KEY), jnp.maximum,
                                jnp.full((tq, LANES), MASKED_KEY, I32))
        return to_key(jnp.max(from_key(per_lane), axis=-1, keepdims=True))

    def halve(lo, hi, cnt_lo):
        gap = hi - lo
        mid = lo + lax.shift_right_logical(gap, 1) + (gap & 1)
        cnt = count(lambda k, t: k >= mid)
        keep = cnt >= topk
        return jnp.where(keep, mid, lo), jnp.where(keep, hi, mid - 1), jnp.where(keep, cnt, cnt_lo)

    def unfinished(lo, hi, cnt_lo):
        return (lo != hi) & (cnt_lo != topk)

    def any_of(flags):
        return jnp.max(jnp.where(flags, 1.0, 0.0))

    def any_wide(lo, hi, cnt_lo):
        wide = lax.shift_right_logical(hi - lo, NARROW_BITS) != 0
        return any_of(unfinished(lo, hi, cnt_lo) & wide)

    def halve_wide(state):
        lo, hi, cnt_lo = halve(*halve(*state[:3]))
        return lo, hi, cnt_lo, any_wide(lo, hi, cnt_lo)

    def step_down(state):
        lo, hi, cnt_lo = state[:3]
        todo = unfinished(lo, hi, cnt_lo)
        top = largest_at_most(hi)
        cnt = count(lambda k, t: k >= top)
        hit = cnt >= topk
        lo = jnp.where(todo & hit, top, lo)
        hi = jnp.where(todo, jnp.where(hit, top, top - 1), hi)
        cnt_lo = jnp.where(todo & hit, cnt, cnt_lo)
        lo, hi, cnt_lo = halve(lo, hi, cnt_lo)
        return lo, hi, cnt_lo, any_of(unfinished(lo, hi, cnt_lo))

    lo0 = to_key(jnp.min(top2, axis=-1, keepdims=True))
    hi0 = to_key(jnp.max(top1, axis=-1, keepdims=True))
    unknown = jnp.full((tq, 1), -1.0, F32)
    going = lambda state: state[3] > 0.0
    lo, hi, cnt_lo, _ = lax.while_loop(going, halve_wide, (lo0, hi0, unknown, any_wide(lo0, hi0, unknown)))
    thr = lax.while_loop(going, step_down, (lo, hi, cnt_lo, any_of(unfinished(lo, hi, cnt_lo))))[0]
    n_gt = count(lambda k, t: k > thr)
    n_ge = count(lambda k, t: k >= thr)
    want_ties = topk - n_gt
    need = ((n_ge - n_gt) > want_ties) & (thr > MASKED_KEY)
    jlim_ref[...] = jnp.full((tq, LANES), NO_LIMIT, I32)

    @pl.when(jnp.max(jnp.where(need, 1.0, 0.0)) > 0.0)
    def _():
        def index_bit(it, j_lim):
            cand = j_lim | lax.shift_left(jnp.int32(1), index_bits - 1 - it)
            below = count(lambda k, t: (k == thr) & ((t * tk + col) < cand))
            return jnp.where(below <= want_ties - 1.0, cand, j_lim)
        j_lim = lax.fori_loop(0, index_bits, index_bit, jnp.zeros((tq, 1), I32))
        jlim_ref[...] = jnp.broadcast_to(jnp.where(need, j_lim, NO_LIMIT), (tq, LANES))

    j_lim = jlim_ref[:, 0:1]

    def bias_tile(t, carry):
        k = keys_ref[t]
        kpos = t * tk + col
        sel = (k > thr) | ((k == thr) & (kpos <= j_lim))
        sel = sel & (k > MASKED_KEY)
        bias_ref[t] = jnp.where(sel, 0.0, NEG)
        return carry

    lax.fori_loop(0, n_kv, bias_tile, 0)

    qk_scale = HEAD_DIM ** -0.5 * LOG2E
    n_full = jnp.minimum(q0, l_valid) // tk

    @pl.when((pl.program_id(0) == 0) & (pl.program_id(1) == 0))
    def _():
        relpos_ref[...] = _rel_pos(tq, WIDE_TILES * tk)

    for j in range(A_KV_HEADS):
        q4 = jnp.concatenate([qa_ref[0, :, (j * A_GROUP + g) * HEAD_DIM:(j * A_GROUP + g + 1) * HEAD_DIM]
                              for g in range(A_GROUP)], axis=0)
        slopes2 = [2.0 ** -(j * A_GROUP + g + 1) * LOG2E for g in range(A_GROUP)]
        slope_rows = jnp.concatenate([jnp.full((tq, LANES), s2, F32) for s2 in slopes2], axis=0)

        def tile_operands(t, n, j=j, q4=q4):
            start = pl.multiple_of(t * tk, tk)
            kt = k_ref[0, pl.ds(start, n * tk), j * HEAD_DIM:(j + 1) * HEAD_DIM]
            vt = v_ref[0, pl.ds(start, n * tk), j * HEAD_DIM:(j + 1) * HEAD_DIM]
            bias = bias_ref[t] if n == 1 else jnp.concatenate([bias_ref[t + i] for i in range(n)], axis=1)
            return start, _nt_dot(q4, kt), vt, bias

        def past_step(t, n, carry, slopes2=slopes2, slope_rows=slope_rows):
            start, s, vt, bias = tile_operands(t, n)
            lg = jnp.concatenate([s[g * tq:(g + 1) * tq] * qk_scale + (relpos_ref[:, :n * tk] * slopes2[g] + bias)
                                  for g in range(A_GROUP)], axis=0)
            off = slope_rows * (start - q0).astype(F32)
            return _softmax_step(lg, off, vt, carry)

        def edge_step(t, carry, slopes2=slopes2):
            start, s, vt, bias = tile_operands(t, 1)
            dist = jnp.abs(qpos - (start + col)).astype(F32)
            lg = jnp.concatenate([s[g * tq:(g + 1) * tq] * qk_scale - slopes2[g] * dist + bias
                                  for g in range(A_GROUP)], axis=0)
            return _softmax_step(lg, 0.0, vt, carry)

        out = _softmax_finish(_tile_loops(n_full, n_kv, past_step, edge_step,
                                          _softmax_init(A_GROUP * tq, HEAD_DIM)))
        for g in range(A_GROUP):
            h = j * A_GROUP + g
            o_ref[0, :, h * HEAD_DIM:(h + 1) * HEAD_DIM] = out[g * tq:(g + 1) * tq].astype(o_ref.dtype)


def _dsa_attention(zb, zf, keys, off, *, n_past, l_valid, tq, tk, topk, name):
    b, s, _ = zb.shape
    karr, kcol, varr, vcol, ikarr, ikcol = keys
    lp = karr.shape[1]
    n_tiles = lp // tk
    assert topk <= min(tk, 2 * LANES) and WIDE_TILES % 2 == 0
    kern = functools.partial(_dsa_kernel, tq=tq, tk=tk, n_past=n_past, l_valid=l_valid, topk=topk,
                             index_bits=lp.bit_length())
    blocks = (_nbytes((tq, A_Q + I_Q), BF16) + _nbytes((tq, LANES), F32) + 2 * _nbytes((lp, A_KV), BF16)
              + _nbytes((lp, LANES), BF16) + _nbytes((tq, A_Q), BF16) + _nbytes((tq, lp), F32)
              + 8 * _nbytes((A_GROUP * tq, tk), F32))
    return pl.pallas_call(
        kern,
        grid=(b, s // tq),
        in_specs=[pl.BlockSpec((1, tq, A_Q), lambda bi, qi: (bi, qi, off['aq'] // A_Q)),
                  pl.BlockSpec((1, tq, I_Q), lambda bi, qi: (bi, qi, off['iq'] // I_Q)),
                  pl.BlockSpec((1, tq, LANES), lambda bi, qi: (bi, qi, off['iw'] // LANES)),
                  pl.BlockSpec((1, lp, A_KV), lambda bi, qi: (bi, 0, kcol)),
                  pl.BlockSpec((1, lp, A_KV), lambda bi, qi: (bi, 0, vcol)),
                  pl.BlockSpec((1, lp, LANES), lambda bi, qi: (bi, 0, ikcol))],
        out_specs=pl.BlockSpec((1, tq, A_Q), lambda bi, qi: (bi, qi, 0)),
        out_shape=jax.ShapeDtypeStruct((b, s, A_Q), BF16),
        scratch_shapes=[pltpu.VMEM((n_tiles, tq, tk), I32), pltpu.VMEM((n_tiles, tq, tk), F32),
                        pltpu.VMEM((tq, LANES), I32), pltpu.VMEM((tq, WIDE_TILES * tk), F32)],
        compiler_params=_cparams(("arbitrary", "arbitrary"), blocks),
        name=name,
    )(zb, zb, zf, karr, varr, ikarr)


def _diff_kernel(q_ref, k_ref, v_ref, lq_ref, g_ref, o_ref, relbias_ref, *, tq, tk, n_past, l_valid, lam_init):
    h = pl.program_id(1)
    q0 = n_past + pl.program_id(2) * tq
    n_kv = _num_kv_tiles(q0, tq, tk, l_valid)
    row = lax.broadcasted_iota(I32, (tq, tk), 0)
    col = lax.broadcasted_iota(I32, (tq, tk), 1)
    qpos = q0 + row
    slope2 = jnp.float32(1.0)
    for hh in range(B_HEADS):
        slope2 = jnp.where(h == hh, jnp.float32(2.0 ** (-8.0 * (hh + 1) / B_HEADS) * LOG2E), slope2)
    qk_scale = HEAD_DIM ** -0.5 * LOG2E
    n_full = jnp.minimum(q0, l_valid) // tk

    @pl.when(pl.program_id(2) == 0)
    def _():
        relbias_ref[...] = _rel_pos(tq, WIDE_TILES * tk) * slope2

    qc = [q_ref[0, :, c * HEAD_DIM:(c + 1) * HEAD_DIM] for c in range(2)]

    def tile_operands(t, n):
        start = pl.multiple_of(t * tk, tk)
        s = [_nt_dot(qc[c], k_ref[0, pl.ds(start, n * tk), c * HEAD_DIM:(c + 1) * HEAD_DIM]) for c in range(2)]
        return start, s, v_ref[0, pl.ds(start, n * tk), :]

    def past_step(t, n, carry):
        start, s, vt = tile_operands(t, n)
        lg = jnp.concatenate([s[c] * qk_scale + relbias_ref[:, :n * tk] for c in range(2)], axis=0)
        return _softmax_step(lg, slope2 * (start - q0).astype(F32), vt, carry)

    def edge_step(t, carry):
        start, s, vt = tile_operands(t, 1)
        kpos = start + col
        adm = ((kpos >> CHUNK_SHIFT) <= (qpos >> CHUNK_SHIFT)) & (kpos < l_valid)
        bias = jnp.where(adm, -slope2 * jnp.abs(qpos - kpos).astype(F32), NEG)
        lg = jnp.concatenate([s[c] * qk_scale + bias for c in range(2)], axis=0)
        return _softmax_step(lg, 0.0, vt, carry)

    out = _softmax_finish(_tile_loops(n_full, n_kv, past_step, edge_step,
                                      _softmax_init(2 * tq, 2 * HEAD_DIM)))
    lq = lq_ref[...]
    lam = (jnp.exp(jnp.sum(lq[0:1] * lq[1:2], axis=-1, keepdims=True))
           - jnp.exp(jnp.sum(lq[2:3] * lq[3:4], axis=-1, keepdims=True)) + lam_init)
    o = out[:tq] - lam * out[tq:]
    y = o * lax.rsqrt(jnp.mean(o * o, axis=-1, keepdims=True) + SUBLN_EPS)
    y = (y * g_ref[...]) * (1.0 - lam_init)
    o_ref[0] = y.astype(o_ref.dtype)


def _diff_attention(zb, keys, lam_qk_l, g_subln_l, off, *, n_past, l_valid, tq, tk, lam_init, name):
    b, s, _ = zb.shape
    karr, kcol0, varr, vcol0 = keys
    lp = karr.shape[1]
    w = 2 * HEAD_DIM
    kern = functools.partial(_diff_kernel, tq=tq, tk=tk, n_past=n_past, l_valid=l_valid, lam_init=lam_init)
    blocks = (2 * _nbytes((tq, w), BF16) + 2 * _nbytes((lp, w), BF16) + 8 * _nbytes((2 * tq, tk), F32))
    return pl.pallas_call(
        kern,
        grid=(b, B_HEADS, s // tq),
        in_specs=[pl.BlockSpec((1, tq, w), lambda bi, h, qi: (bi, qi, off['bq'] // w + h)),
                  pl.BlockSpec((1, lp, w), lambda bi, h, qi: (bi, 0, kcol0 + h)),
                  pl.BlockSpec((1, lp, w), lambda bi, h, qi: (bi, 0, vcol0 + h)),
                  pl.BlockSpec((4, HEAD_DIM), lambda bi, h, qi: (0, 0)),
                  pl.BlockSpec((1, w), lambda bi, h, qi: (0, 0))],
        out_specs=pl.BlockSpec((1, tq, w), lambda bi, h, qi: (bi, qi, h)),
        out_shape=jax.ShapeDtypeStruct((b, s, B_V), BF16),
        scratch_shapes=[pltpu.VMEM((tq, WIDE_TILES * tk), F32)],
        compiler_params=_cparams(("arbitrary", "arbitrary", "arbitrary"), blocks),
        name=name,
    )(zb, karr, varr, lam_qk_l, g_subln_l.reshape(1, w))


def _row_gather(src_ref, idx_ref, idx_base, idx_stride, dst_ref, sem, n_rows):
    def row_copy(r):
        return pltpu.make_async_copy(src_ref.at[pl.ds(idx_ref[idx_base + r * idx_stride], 1)],
                                     dst_ref.at[pl.ds(r, 1)], sem)

    def start():
        lax.fori_loop(0, n_rows, lambda r, c: (row_copy(r).start(), c)[1], 0, unroll=ISSUE_UNROLL)

    def wait():
        pltpu.make_async_copy(dst_ref.at[pl.ds(0, n_rows)], dst_ref.at[pl.ds(0, n_rows)], sem).wait()

    return start, wait


def _moe_up_kernel(eid_ref, tok_ref, h_ref, wg_ref, wu_ref, o_ref, rows_ref, rows16_ref, sems, *,
                   blk, n_blocks, n_col_steps, rows_per_step):
    i, j = pl.program_id(0), pl.program_id(1)
    slot = i % 2
    n_issued = n_col_steps * rows_per_step

    def row_copy(block, buf, r_dst, r_src):
        return pltpu.make_async_copy(h_ref.at[pl.ds(tok_ref[block * blk + r_src], 1)],
                                     rows_ref.at[buf, pl.ds(r_dst, 1)], sems.at[buf])

    def wait_all(buf):
        pltpu.make_async_copy(rows_ref.at[buf, pl.ds(0, n_issued)], rows_ref.at[buf, pl.ds(0, n_issued)],
                              sems.at[buf]).wait()

    @pl.when((i == 0) & (j == 0))
    def _():
        def first(r, c):
            row_copy(0, 0, r, jnp.minimum(r, blk - 1)).start()
            return c
        lax.fori_loop(0, n_issued, first, 0, unroll=ISSUE_UNROLL)

    @pl.when(j == 0)
    def _():
        wait_all(slot)
        rows16_ref[...] = rows_ref[slot, :blk].astype(BF16)

    nxt = jnp.minimum(i + 1, n_blocks - 1)
    for r in range(rows_per_step):
        r_dst = j * rows_per_step + r
        row_copy(nxt, 1 - slot, r_dst, jnp.minimum(r_dst, blk - 1)).start()

    a = rows16_ref[...]
    u = jnp.dot(a, wg_ref[0], preferred_element_type=F32)
    v = jnp.dot(a, wu_ref[0], preferred_element_type=F32)
    o_ref[...] = ((u * jax.nn.sigmoid(u)) * v).astype(o_ref.dtype)

    @pl.when((i == n_blocks - 1) & (j == n_col_steps - 1))
    def _():
        wait_all(1 - slot)


def _moe_up(h2, tok_buf, block_expert, w_gate, w_up, *, blk, name):
    p = tok_buf.shape[0]
    d = h2.shape[1]
    fp = w_gate.shape[-1]
    tn = _tile(fp, 512)
    n_blocks = p // blk
    n_col_steps = fp // tn
    rows_per_step = _round_up(-(-blk // n_col_steps), 8)
    n_rows_buf = n_col_steps * rows_per_step
    blocks = (_nbytes((n_rows_buf, d), F32) + _nbytes((blk, d), BF16) // 2 + 2 * _nbytes((d, tn), BF16)
              + _nbytes((blk, tn), BF16) + _nbytes((blk, tn), F32))
    return pl.pallas_call(
        functools.partial(_moe_up_kernel, blk=blk, n_blocks=n_blocks, n_col_steps=n_col_steps,
                          rows_per_step=rows_per_step),
        grid_spec=pltpu.PrefetchScalarGridSpec(
            num_scalar_prefetch=2, grid=(n_blocks, n_col_steps),
            in_specs=[pl.BlockSpec(memory_space=pl.ANY),
                      pl.BlockSpec((1, d, tn), lambda i, j, e, tok: (e[i], 0, j)),
                      pl.BlockSpec((1, d, tn), lambda i, j, e, tok: (e[i], 0, j))],
            out_specs=pl.BlockSpec((blk, tn), lambda i, j, e, tok: (i, j)),
            scratch_shapes=[pltpu.VMEM((2, n_rows_buf, d), F32), pltpu.VMEM((blk, d), BF16),
                            pltpu.SemaphoreType.DMA((2,))]),
        out_shape=jax.ShapeDtypeStruct((p, fp), BF16),
        compiler_params=_cparams(("arbitrary", "arbitrary"), blocks),
        name=name,
    )(block_expert, tok_buf, h2, w_gate, w_up)


def _combine_kernel(dest_ref, x_ref, y_ref, rg_ref, gate_ref, o_ref, picked_ref, sems, *, tm, n_steps):
    i = pl.program_id(0)
    slot = i % 2

    def fetch(step, buf):
        base = step * tm * TOP_K_EXPERTS
        return [_row_gather(y_ref, dest_ref, base + k, TOP_K_EXPERTS, picked_ref.at[buf, k], sems.at[buf, k], tm)
                for k in range(TOP_K_EXPERTS)]

    @pl.when(i == 0)
    def _():
        for start, _ in fetch(0, 0):
            start()

    for _, wait in fetch(i, slot):
        wait()

    @pl.when(i + 1 < n_steps)
    def _():
        for start, _ in fetch(i + 1, 1 - slot):
            start()

    rg = rg_ref[...]
    f = rg[:, 0:1] * picked_ref[slot, 0] + rg[:, 1:2] * picked_ref[slot, 1]
    o_ref[...] = x_ref[...] + gate_ref[0] * f


def _moe_combine(x2, yb, dest, rgate, gate_f, seq, *, name):
    t, d = x2.shape
    tm = _tile(seq, 256)
    per_b = seq // tm
    return pl.pallas_call(
        functools.partial(_combine_kernel, tm=tm, n_steps=t // tm),
        grid_spec=pltpu.PrefetchScalarGridSpec(
            num_scalar_prefetch=1, grid=(t // tm,),
            in_specs=[pl.BlockSpec((tm, d), lambda i, dst: (i, 0)),
                      pl.BlockSpec(memory_space=pl.ANY),
                      pl.BlockSpec((tm, LANES), lambda i, dst: (i, 0)),
                      pl.BlockSpec((1, 1, d), lambda i, dst: (i // per_b, 0, 0))],
            out_specs=pl.BlockSpec((tm, d), lambda i, dst: (i, 0)),
            scratch_shapes=[pltpu.VMEM((2, TOP_K_EXPERTS, tm, d), F32),
                            pltpu.SemaphoreType.DMA((2, TOP_K_EXPERTS))]),
        out_shape=jax.ShapeDtypeStruct((t, d), F32),
        compiler_params=_cparams(("arbitrary",), 4 * _nbytes((tm, d), F32)),
        name=name,
    )(dest, x2, yb, rgate, gate_f)


STATE_COLS = (('ak', A_KV), ('av', A_KV), ('ik', LANES), ('bk', B_QK), ('bv', B_V))


def _state_tails():
    return ((A_KV_HEADS, HEAD_DIM), (A_KV_HEADS, HEAD_DIM), (IDX_DIM,), (B_HEADS, 2, HEAD_DIM),
            (B_HEADS, 2 * HEAD_DIM))


def _state_kernel(*refs, depth):
    n = len(STATE_COLS)
    ins, outs = refs[:depth * n], refs[depth * n:]
    for l in range(depth):
        @pl.when(pl.program_id(0) == l)
        def _(l=l):
            ak, av, ik, bk, bv = ins[l * n:(l + 1) * n]
            for j in range(A_KV_HEADS):
                outs[0][0, 0, :, j, :] = ak[0, :, j * HEAD_DIM:(j + 1) * HEAD_DIM]
                outs[1][0, 0, :, j, :] = av[0, :, j * HEAD_DIM:(j + 1) * HEAD_DIM]
            outs[2][0, 0] = ik[0, :, :IDX_DIM]
            for h in range(B_HEADS):
                for c in range(2):
                    lo = (2 * h + c) * HEAD_DIM
                    outs[3][0, 0, :, h, c, :] = bk[0, :, lo:lo + HEAD_DIM]
                outs[4][0, 0, :, h, :] = bv[0, :, 2 * h * HEAD_DIM:2 * (h + 1) * HEAD_DIM]


def _state_rows(zf3_layers, off, *, name):
    depth = len(zf3_layers)
    b, s, _ = zf3_layers[0].shape
    ts = _tile(s, 256)
    in_specs, args = [], []
    for l in range(depth):
        for cname, width in STATE_COLS:
            blk = off[cname] // width
            in_specs.append(pl.BlockSpec(
                (1, ts, width),
                lambda li, bi, i, l=l, blk=blk: (jnp.where(li == l, bi, 0), jnp.where(li == l, i, 0), blk)))
            args.append(zf3_layers[l])
    tails = _state_tails()
    out_specs = [pl.BlockSpec((1, 1, ts) + t, lambda li, bi, i, nt=len(t): (li, bi, i) + (0,) * nt) for t in tails]
    out_shape = [jax.ShapeDtypeStruct((depth, b, s) + t, F32) for t in tails]
    blocks = (depth + 1) * sum(_nbytes((ts, w), F32) for _, w in STATE_COLS)
    return pl.pallas_call(
        functools.partial(_state_kernel, depth=depth),
        grid=(depth, b, s // ts), in_specs=in_specs, out_specs=out_specs, out_shape=out_shape,
        compiler_params=_cparams(("arbitrary", "arbitrary", "arbitrary"), blocks), name=name,
    )(*args)


def _pack_kernel(c_hbm, new_ref, o_ref, buf, sems, *, layer, tk, n_cached_tiles, tail):
    b, i = pl.program_id(0), pl.program_id(1)
    slot = i % 2
    piece = tail[-1]
    heads = list(itertools.product(*[range(t) for t in tail[:-1]]))

    def tile_copies(tile, s):
        rows = pl.ds(tile * tk, tk)
        return [pltpu.make_async_copy(c_hbm.at[(layer, b, rows) + idx + (slice(None),)], buf.at[s, n], sems.at[s, n])
                for n, idx in enumerate(heads)]

    @pl.when(i == 0)
    def _():
        for c in tile_copies(0, 0):
            c.start()

    @pl.when(i < n_cached_tiles)
    def _():
        for c in tile_copies(i, slot):
            c.wait()

        @pl.when(i + 1 < n_cached_tiles)
        def _():
            for c in tile_copies(i + 1, 1 - slot):
                c.start()

        for n in range(len(heads)):
            o_ref[0, :, n * piece:(n + 1) * piece] = buf[slot, n].astype(BF16)

    @pl.when(i >= n_cached_tiles)
    def _():
        o_ref[0] = new_ref[0]


def _pack_keys(cache, layer, new_rows, *, tk, name):
    _, b, p = cache.shape[:3]
    tail = cache.shape[3:]
    w = math.prod(tail)
    assert p % tk == 0 and new_rows.shape == (b, tk, w)
    n_cached_tiles = p // tk
    n_heads = w // tail[-1]
    return pl.pallas_call(
        functools.partial(_pack_kernel, layer=layer, tk=tk, n_cached_tiles=n_cached_tiles, tail=tail),
        grid=(b, n_cached_tiles + 1),
        in_specs=[pl.BlockSpec(memory_space=pl.ANY),
                  pl.BlockSpec((1, tk, w), lambda bi, i: (bi, 0, 0))],
        out_specs=pl.BlockSpec((1, tk, w), lambda bi, i: (bi, i, 0)),
        out_shape=jax.ShapeDtypeStruct((b, p + tk, w), BF16),
        scratch_shapes=[pltpu.VMEM((2, n_heads, tk, tail[-1]), F32), pltpu.SemaphoreType.DMA((2, n_heads))],
        compiler_params=_cparams(("arbitrary", "arbitrary"), _nbytes((tk, w), F32) + 2 * _nbytes((tk, w), BF16)),
        name=name,
    )(cache, new_rows)


def _silu_mul(accs, extras):
    u, v = accs
    return [(u * jax.nn.sigmoid(u)) * v]


def _moe_ffn(x2, h2, ridx, rgate, gate_f, seq, w_gate, w_up, w_down, *, tag):
    t, d = x2.shape
    fp = w_gate.shape[-1]
    a = t * TOP_K_EXPERTS
    blk = _tile(a, 512)
    n_blocks = -(-(a + N_EXPERTS * (blk - 1)) // blk)
    p = n_blocks * blk
    e_flat = ridx[:, :TOP_K_EXPERTS].reshape(-1)
    onehot = (e_flat[:, None] == jnp.arange(N_EXPERTS, dtype=I32)[None, :]).astype(I32)
    rank = jnp.sum((jnp.cumsum(onehot, axis=0) - onehot) * onehot, axis=1)
    counts = jnp.sum(onehot, axis=0)
    padded = ((counts + blk - 1) // blk) * blk
    pends = jnp.cumsum(padded)
    pstarts = pends - padded
    dest = (pstarts[e_flat] + rank).astype(I32)
    tok_buf = jnp.zeros((p,), I32).at[dest].set(jnp.arange(a, dtype=I32) // TOP_K_EXPERTS)
    block_expert = jnp.clip(jnp.searchsorted(pends, jnp.arange(n_blocks, dtype=I32) * blk, side='right'),
                            0, N_EXPERTS - 1).astype(I32)

    act = _moe_up(h2, tok_buf, block_expert, w_gate, w_up, blk=blk, name=f"moe_up_{tag}")
    (yb,) = _matmul([act], [w_down], [0], [], lambda accs, ex: accs, [F32], d, tm=blk, tn=_tile(d, 512),
                    eids=block_expert, name=f"moe_down_{tag}")
    return _moe_combine(x2, yb, dest, rgate, gate_f, seq, name=f"moe_combine_{tag}")


def _trunk(x, mods, past_all, wts, tag):
    b, s, d = x.shape
    t = b * s
    depth = len(wts['w_in'])
    off, nz = _z_layout(d)
    x2 = x.reshape(t, d)
    tn_d = _tile(d, 512)
    tm_b = _tile(s, 512)
    tm_f = _tile(t, 1024)
    per_b = s // tm_b
    fuse_norm = s % tm_f == 0
    zf3_layers = []
    for l in range(depth):
        shift_a, scale_a, gate_a, shift_f, scale_f, gate_f = [
            mods[l][:, i * d:(i + 1) * d].reshape(b, 1, d) for i in range(N_ADA)]
        batch_tile = lambda i, j, *_: (i // per_b, 0, j)

        if fuse_norm:
            h, norm_a = x2, (wts['g_attn'][l], scale_a, shift_a, s)
        else:
            h, norm_a = _norm(x2, wts['g_attn'][l], s, scale=scale_a, shift=shift_a, out_dtype=BF16,
                              name=f"norm_attn_{tag}{l}"), None
        zf, zb = _matmul([h], [wts['w_in'][l]], [0], [], lambda accs, ex: [accs[0], accs[0]], [F32, BF16], nz,
                         tm=tm_f, tn=IN_PROJ_TN, norm=norm_a, name=f"in_proj_{tag}{l}")
        zf3, zb3 = zf.reshape(b, s, nz), zb.reshape(b, s, nz)
        cut = lambda z, name, width: z[:, :, off[name]:off[name] + width]
        zf3_layers.append(zf3)

        if past_all is None:
            n_past, l_valid = 0, s
            tk = _tile(s, 512)
            tq_a, tq_b = _tile(s, 128), _tile(s, 256)
            dsa_keys = (zb3, off['ak'] // A_KV, zb3, off['av'] // A_KV, zb3, off['ik'] // LANES)
            diff_keys = (zb3, off['bk'] // (2 * HEAD_DIM), zb3, off['bv'] // (2 * HEAD_DIM))
        else:
            n_past = past_all[0].shape[2]
            l_valid = n_past + s
            tk = 512
            tq_a = tq_b = s

            def new_rows(name, width):
                return jnp.concatenate([cut(zb3, name, width), jnp.zeros((b, tk - s, width), BF16)], axis=1)

            def with_past(ci, name, width):
                return _pack_keys(past_all[ci], l, new_rows(name, width), tk=tk, name=f"pack_{name}_{tag}{l}")

            ik_past = past_all[2][l].astype(BF16)
            ik_all = jnp.concatenate([jnp.concatenate([ik_past, ik_past], axis=-1), new_rows('ik', LANES)], axis=1)
            dsa_keys = (with_past(0, 'ak', A_KV), 0, with_past(1, 'av', A_KV), 0, ik_all, 0)
            diff_keys = (with_past(3, 'bk', B_QK), 0, with_past(4, 'bv', B_V), 0)
        topk = min(TOPK_MAX, l_valid // 4)
        oa = _dsa_attention(zb3, zf3, dsa_keys, off, n_past=n_past, l_valid=l_valid, tq=tq_a, tk=tk,
                            topk=topk, name=f"dsa_{tag}{l}")
        lam_init = 0.8 - 0.6 * math.exp(-0.3 * l)
        ob = _diff_attention(zb3, diff_keys, wts['lam_qk'][l], wts['g_subln'][l], off, n_past=n_past,
                             l_valid=l_valid, tq=tq_b, tk=tk, lam_init=lam_init, name=f"diff_{tag}{l}")

        def merge_epi(accs, ex):
            return [jax.nn.sigmoid(ex[0]) * accs[0] + jax.nn.sigmoid(ex[1]) * accs[1]]

        (merged,) = _matmul(
            [oa.reshape(t, A_Q), ob.reshape(t, B_V)], [wts['w_out_a'][l], wts['w_out_b'][l]], [0, 1],
            [(zf, (tm_f, tn_d), lambda i, j, *_: (i, off['ga'] // tn_d + j)),
             (zf, (tm_f, tn_d), lambda i, j, *_: (i, off['gb'] // tn_d + j))],
            merge_epi, [BF16], d, tm=tm_f, tn=tn_d, name=f"merge_{tag}{l}")

        def resid_epi(accs, ex):
            return [ex[0] + ex[1][0] * accs[0]]

        (x2,) = _matmul([merged], [wts['w_out'][l]], [0],
                        [(x2, (tm_b, tn_d), lambda i, j, *_: (i, j)), (gate_a, (1, 1, tn_d), batch_tile)],
                        resid_epi, [F32], d, tm=tm_b, tn=tn_d, name=f"out_proj_{tag}{l}")

        if l % 2 == 0:
            i = l // 2
            if fuse_norm:
                h, norm_f = x2, (wts['g_ffn'][l], scale_f, shift_f, s)
            else:
                h, norm_f = _norm(x2, wts['g_ffn'][l], s, scale=scale_f, shift=shift_f, out_dtype=BF16,
                                  name=f"norm_ffn_{tag}{l}"), None
            fp = wts['w_ff_gate'][i].shape[-1]
            (act,) = _matmul([h], [wts['w_ff_gate'][i], wts['w_ff_up'][i]], [0, 0], [], _silu_mul, [BF16], fp,
                             tm=tm_f, tn=_tile(fp, 512), norm=norm_f, name=f"ffn_up_{tag}{l}")
            (x2,) = _matmul([act], [wts['w_ff_down'][i]], [0],
                            [(x2, (tm_b, tn_d), lambda i, j, *_: (i, j)), (gate_f, (1, 1, tn_d), batch_tile)],
                            resid_epi, [F32], d, tm=tm_b, tn=tn_d, name=f"ffn_down_{tag}{l}")
        else:
            i = l // 2
            h2, ridx, rgate = _norm(x2, wts['g_ffn'][l], s, scale=scale_f, shift=shift_f,
                                    router=(wts['w_router'][i], wts['b_router'][i]), out_dtype=F32,
                                    name=f"norm_router_{tag}{l}")
            x2 = _moe_ffn(x2, h2, ridx, rgate, gate_f, s, wts['w_moe_gate'][i], wts['w_moe_up'][i],
                          wts['w_moe_down'][i], tag=f"{tag}{l}")
    y = _norm(x2, wts['g_final'], s, out_dtype=F32, name=f"norm_final_{tag}")
    state = tuple(_state_rows(zf3_layers, off, name=f"state_rows_{tag}"))
    return y.reshape(b, s, d), state


def _prep_weights(w_in, w_out_a, w_out_b, w_out, w_ff_gate, w_ff_up, w_ff_down, w_router, b_router,
                  w_moe_gate, w_moe_up, w_moe_down, d_model):
    f = w_ff_gate.shape[-1]
    fp = _round_up(f, 512)
    pad_cols = lambda w: jnp.pad(w, [(0, 0)] * (w.ndim - 1) + [(0, fp - f)]).astype(BF16)
    pad_rows = lambda w: jnp.pad(w, [(0, 0)] * (w.ndim - 2) + [(0, fp - f), (0, 0)]).astype(BF16)
    n_moe = w_router.shape[0]
    return dict(
        w_in=[_prep_w_in(w_in[l], d_model) for l in range(w_in.shape[0])],
        w_out_a=w_out_a.astype(BF16), w_out_b=w_out_b.astype(BF16), w_out=w_out.astype(BF16),
        w_ff_gate=pad_cols(w_ff_gate), w_ff_up=pad_cols(w_ff_up), w_ff_down=pad_rows(w_ff_down),
        w_router=[jnp.pad(w_router[i], ((0, 0), (0, LANES - N_EXPERTS))) for i in range(n_moe)],
        b_router=[jnp.pad(b_router[i], (0, LANES - N_EXPERTS)).reshape(1, LANES) for i in range(n_moe)],
        w_moe_gate=pad_cols(w_moe_gate), w_moe_up=pad_cols(w_moe_up), w_moe_down=pad_rows(w_moe_down),
    )


def kernel(x_prompt, x_sample, c_prompt, c_sample, cache_dsa_k, cache_dsa_v, cache_idx_k, cache_diff_k, cache_diff_v, w_ada, b_ada, g_attn, w_in, w_out_a, w_out_b, w_out, lam_qk, g_subln, g_ffn, w_ff_gate, w_ff_up, w_ff_down, w_router, b_router, w_moe_gate, w_moe_up, w_moe_down, g_final):
    d = x_prompt.shape[-1]
    wts = _prep_weights(w_in, w_out_a, w_out_b, w_out, w_ff_gate, w_ff_up, w_ff_down, w_router, b_router,
                        w_moe_gate, w_moe_up, w_moe_down, d)
    wts.update(g_attn=g_attn, g_ffn=g_ffn, g_final=g_final, lam_qk=lam_qk, g_subln=g_subln)
    nb_p = c_prompt.shape[0]
    mods = _ada_mod(jnp.concatenate([c_prompt, c_sample], axis=0), w_ada, b_ada)
    y_p, st_p = _trunk(x_prompt, mods[:, :nb_p], None, wts, "p")
    past_all = (cache_dsa_k, cache_dsa_v, cache_idx_k, cache_diff_k, cache_diff_v)
    y_s, st_s = _trunk(x_sample, mods[:, nb_p:], past_all, wts, "s")
    return (y_p, y_s) + st_p + st_s
```

```python
import functools
import itertools
import math

import jax
import jax.numpy as jnp
from jax import lax
from jax.experimental import pallas as pl
from jax.experimental.pallas import tpu as pltpu

F32, BF16, I32 = jnp.float32, jnp.bfloat16, jnp.int32

CHUNK = 64
CHUNK_SHIFT = CHUNK.bit_length() - 1
HEAD_DIM = 128
A_HEADS = 8
A_KV_HEADS = 2
A_GROUP = A_HEADS // A_KV_HEADS
IDX_HEADS = 8
IDX_DIM = 64
TOPK_MAX = 256
B_HEADS = 4
N_EXPERTS = 8
TOP_K_EXPERTS = 2
N_ADA = 6
NORM_EPS = 1e-6
SUBLN_EPS = 1e-5
A_Q = A_HEADS * HEAD_DIM
A_KV = A_KV_HEADS * HEAD_DIM
I_Q = IDX_HEADS * IDX_DIM
B_QK = B_HEADS * 2 * HEAD_DIM
B_V = B_HEADS * 2 * HEAD_DIM

LANES = 128
VMEM_BYTES_V7X = 64 * 2 ** 20
NEG = -1e30
NO_LIMIT = 2 ** 30
LOG2E = math.log2(math.e)
IN_PROJ_TN = 512
MASKED_KEY = 0x807FFFFF - 2 ** 32
WIDE_TILES = 4
ISSUE_UNROLL = 8
NARROW_BITS = 12


def _cparams(dims, block_bytes):
    limit = min(max(2 * int(block_bytes) + (8 << 20), 32 << 20), VMEM_BYTES_V7X - (6 << 20))
    return pltpu.CompilerParams(dimension_semantics=dims, vmem_limit_bytes=limit)


def _nbytes(shape, dtype):
    return math.prod(shape) * jnp.dtype(dtype).itemsize


def _tile(n, pref):
    if n <= pref:
        return n
    t = pref
    while n % t:
        t //= 2
    return t


def _round_up(n, m):
    return -(-n // m) * m


def _z_layout(d_model):
    off, o = {}, 0
    for name, n in (('aq', A_Q), ('ak', A_KV), ('av', A_KV), ('iq', I_Q), ('bq', B_QK), ('bk', B_QK),
                    ('bv', B_V), ('ga', d_model), ('gb', d_model), ('ik', LANES), ('iw', LANES)):
        off[name] = o
        o += n
    return off, _round_up(o, IN_PROJ_TN)


def _prep_w_in(w, d_model):
    sizes = (A_Q, A_KV, A_KV, I_Q, IDX_DIM, IDX_HEADS, B_QK, B_QK, B_V, d_model, d_model)
    names = ('aq', 'ak', 'av', 'iq', 'ik', 'iw', 'bq', 'bk', 'bv', 'ga', 'gb')
    parts, o = {}, 0
    for n, s in zip(names, sizes):
        parts[n] = w[:, o:o + s]
        o += s
    cols = [parts[n] for n in ('aq', 'ak', 'av', 'iq', 'bq', 'bk', 'bv', 'ga', 'gb')]
    cols += [parts['ik'], parts['ik'], parts['iw']]
    used = sum(c.shape[1] for c in cols)
    cols.append(jnp.zeros((w.shape[0], _z_layout(d_model)[1] - used), w.dtype))
    return jnp.concatenate(cols, axis=1).astype(BF16)


def _ada_kernel(c_ref, w_ref, b_ref, o_ref):
    c = c_ref[...]
    s = c * jax.nn.sigmoid(c)
    o_ref[0] = jnp.dot(s.astype(BF16), w_ref[0].astype(BF16), preferred_element_type=F32) + b_ref[0]


def _ada_mod(c_all, w_ada, b_ada):
    depth, d, n = w_ada.shape
    nb = c_all.shape[0]
    tn = _tile(n, 1024)
    blocks = _nbytes((nb, d), F32) + _nbytes((d, tn), F32) * 2 + _nbytes((nb, tn), F32)
    return pl.pallas_call(
        _ada_kernel,
        grid=(depth, n // tn),
        in_specs=[pl.BlockSpec((nb, d), lambda l, j: (0, 0)),
                  pl.BlockSpec((1, d, tn), lambda l, j: (l, 0, j)),
                  pl.BlockSpec((1, 1, tn), lambda l, j: (l, 0, j))],
        out_specs=pl.BlockSpec((1, nb, tn), lambda l, j: (l, 0, j)),
        out_shape=jax.ShapeDtypeStruct((depth, nb, n), F32),
        compiler_params=_cparams(("arbitrary", "arbitrary"), blocks),
        name="ada_mod",
    )(c_all, w_ada, b_ada.reshape(depth, 1, n))


def _norm_kernel(*refs, modulated, router, eps):
    refs = list(refs)
    x_ref, g_ref = refs[:2]
    pos = 2
    x = x_ref[...]
    y = x * lax.rsqrt(jnp.mean(x * x, axis=-1, keepdims=True) + eps)
    y = y * g_ref[...]
    if modulated:
        scale_ref, shift_ref = refs[pos:pos + 2]
        pos += 2
        y = y * (1.0 + scale_ref[0]) + shift_ref[0]
    if router:
        wr_ref, br_ref = refs[pos:pos + 2]
        pos += 2
    o_ref = refs[pos]
    o_ref[...] = y.astype(o_ref.dtype)
    if router:
        idx_ref, gate_ref = refs[pos + 1:pos + 3]
        logits = jnp.dot(y, wr_ref[...], preferred_element_type=F32,
                         precision=lax.Precision.HIGHEST) + br_ref[...]
        lane = lax.broadcasted_iota(I32, logits.shape, 1)
        logits = jnp.where(lane < N_EXPERTS, logits, -jnp.inf)
        m1 = jnp.max(logits, axis=-1, keepdims=True)
        i1 = jnp.min(jnp.where(logits == m1, lane, LANES), axis=-1, keepdims=True)
        rest = jnp.where(lane == i1, -jnp.inf, logits)
        m2 = jnp.max(rest, axis=-1, keepdims=True)
        i2 = jnp.min(jnp.where(rest == m2, lane, LANES), axis=-1, keepdims=True)
        e = jnp.exp(m2 - m1)
        g1 = 1.0 / (1.0 + e)
        g2 = e / (1.0 + e)
        idx_ref[...] = jnp.where(lane == 0, i1, jnp.where(lane == 1, i2, 0))
        gate_ref[...] = jnp.where(lane == 0, g1, jnp.where(lane == 1, g2, 0.0))


def _norm(x2, g, seq, *, scale=None, shift=None, router=None, out_dtype, eps=NORM_EPS, name):
    t, d = x2.shape
    tm = _tile(seq, 256)
    per_b = seq // tm
    in_specs = [pl.BlockSpec((tm, d), lambda i: (i, 0)), pl.BlockSpec((1, d), lambda i: (0, 0))]
    args = [x2, g.reshape(1, d)]
    if scale is not None:
        in_specs += [pl.BlockSpec((1, 1, d), lambda i: (i // per_b, 0, 0))] * 2
        args += [scale, shift]
    out_specs = [pl.BlockSpec((tm, d), lambda i: (i, 0))]
    out_shape = [jax.ShapeDtypeStruct((t, d), out_dtype)]
    if router is not None:
        w_r, b_r = router
        in_specs += [pl.BlockSpec((d, LANES), lambda i: (0, 0)), pl.BlockSpec((1, LANES), lambda i: (0, 0))]
        args += [w_r, b_r]
        out_specs += [pl.BlockSpec((tm, LANES), lambda i: (i, 0))] * 2
        out_shape += [jax.ShapeDtypeStruct((t, LANES), I32), jax.ShapeDtypeStruct((t, LANES), F32)]
    blocks = 3 * _nbytes((tm, d), F32) + _nbytes((d, LANES), F32)
    outs = pl.pallas_call(
        functools.partial(_norm_kernel, modulated=scale is not None, router=router is not None, eps=eps),
        grid=(t // tm,), in_specs=in_specs, out_specs=out_specs, out_shape=out_shape,
        compiler_params=_cparams(("arbitrary",), blocks), name=name,
    )(*args)
    return outs if router is not None else outs[0]


def _mm_kernel(*refs, n_a, a_of_w, n_extra, epi, has_eids, normed):
    refs = list(refs)
    if has_eids:
        refs = refs[1:]
    n_w = len(a_of_w)
    a_refs = refs[:n_a]
    w_refs = refs[n_a:n_a + n_w]
    pos = n_a + n_w
    if normed:
        g_ref, scale_ref, shift_ref = refs[pos:pos + 3]
        pos += 3
        h_ref = refs.pop()

        @pl.when(pl.program_id(1) == 0)
        def _():
            x = a_refs[0][...]
            y = x * lax.rsqrt(jnp.mean(x * x, axis=-1, keepdims=True) + NORM_EPS)
            y = y * g_ref[...]
            h_ref[...] = (y * (1.0 + scale_ref[0]) + shift_ref[0]).astype(BF16)

        a_vals = [h_ref[...]]
    else:
        a_vals = [a[...].astype(BF16) for a in a_refs]
    e_refs = refs[pos:pos + n_extra]
    o_refs = refs[pos + n_extra:]
    accs = []
    for ai, w_ref in zip(a_of_w, w_refs):
        w = w_ref[0] if len(w_ref.shape) == 3 else w_ref[...]
        accs.append(jnp.dot(a_vals[ai], w, preferred_element_type=F32))
    outs = epi(accs, [e[...] for e in e_refs])
    for o_ref, o in zip(o_refs, outs):
        o_ref[...] = o.astype(o_ref.dtype)


def _matmul(a_list, w_list, a_of_w, extras, epi, out_dtypes, n_cols, *, tm, tn, eids=None, norm=None, name):
    m = a_list[0].shape[0]
    in_specs, blocks, scratch = [], 0, []
    for a in a_list:
        in_specs.append(pl.BlockSpec((tm, a.shape[1]), lambda i, j, *_: (i, 0)))
        blocks += _nbytes((tm, a.shape[1]), a.dtype)
    for w in w_list:
        if w.ndim == 2:
            in_specs.append(pl.BlockSpec((w.shape[0], tn), lambda i, j, *_: (0, j)))
        else:
            in_specs.append(pl.BlockSpec((1, w.shape[1], tn), lambda i, j, e: (e[i], 0, j)))
        blocks += _nbytes((w.shape[-2], tn), w.dtype)
    norm_args = []
    if norm is not None:
        g, scale, shift, seq = norm
        k = a_list[0].shape[1]
        assert len(a_list) == 1 and seq % tm == 0
        per_b = seq // tm
        in_specs += [pl.BlockSpec((1, k), lambda i, j, *_: (0, 0))]
        in_specs += [pl.BlockSpec((1, 1, k), lambda i, j, *_: (i // per_b, 0, 0))] * 2
        norm_args = [g.reshape(1, k), scale, shift]
        scratch = [pltpu.VMEM((tm, k), BF16)]
        blocks += _nbytes((tm, k), BF16)
    for arr, bs, im in extras:
        in_specs.append(pl.BlockSpec(bs, im))
        blocks += _nbytes(bs, arr.dtype)
    out_specs = [pl.BlockSpec((tm, tn), lambda i, j, *_: (i, j)) for _ in out_dtypes]
    out_shape = [jax.ShapeDtypeStruct((m, n_cols), dt) for dt in out_dtypes]
    blocks += sum(_nbytes((tm, tn), dt) for dt in out_dtypes) + len(w_list) * _nbytes((tm, tn), F32)
    kern = functools.partial(_mm_kernel, n_a=len(a_list), a_of_w=tuple(a_of_w), n_extra=len(extras),
                             epi=epi, has_eids=eids is not None, normed=norm is not None)
    grid_spec = pltpu.PrefetchScalarGridSpec(
        num_scalar_prefetch=0 if eids is None else 1, grid=(m // tm, n_cols // tn),
        in_specs=in_specs, out_specs=out_specs, scratch_shapes=scratch)
    args = (([] if eids is None else [eids]) + list(a_list) + list(w_list) + norm_args
            + [e[0] for e in extras])
    outs = pl.pallas_call(kern, grid_spec=grid_spec, out_shape=out_shape,
                          compiler_params=_cparams(("arbitrary", "arbitrary"), blocks), name=name)(*args)
    return outs


def _nt_dot(a, b):
    return lax.dot_general(a, b, (((1,), (1,)), ((), ())), preferred_element_type=F32)


def _lane_blocks(x):
    return [x[:, c * LANES:(c + 1) * LANES] for c in range(x.shape[1] // LANES)]


def _softmax_step(lg, off, vt, carry):
    m, l, acc = carry
    rows, width = lg.shape
    block_max = functools.reduce(jnp.maximum, _lane_blocks(lg))
    row_max = jnp.broadcast_to(jnp.max(block_max, axis=-1, keepdims=True), (rows, LANES))
    m_new = jnp.maximum(m, row_max + off)
    alpha = jnp.exp2(m - m_new)
    p = jnp.exp2(lg - jnp.tile(m_new - off, (1, width // LANES)))
    l = alpha * l + functools.reduce(jnp.add, _lane_blocks(p))
    acc = (jnp.tile(alpha, (1, acc.shape[1] // LANES)) * acc
           + jnp.dot(p.astype(BF16), vt, preferred_element_type=F32))
    return m_new, l, acc


def _softmax_init(rows, width):
    return (jnp.full((rows, LANES), NEG, F32), jnp.zeros((rows, LANES), F32), jnp.zeros((rows, width), F32))


def _softmax_finish(carry):
    _, l, acc = carry
    return acc / jnp.sum(l, axis=-1, keepdims=True)


def _rel_pos(rows, cols):
    return (lax.broadcasted_iota(I32, (rows, cols), 1) - lax.broadcasted_iota(I32, (rows, cols), 0)).astype(F32)


def _tile_loops(n_full, n_kv, past_step, edge_step, carry):
    n_wide = n_full // WIDE_TILES
    carry = lax.fori_loop(0, n_wide, lambda i, c: past_step(i * WIDE_TILES, WIDE_TILES, c), carry)
    done = n_wide * WIDE_TILES
    n_pair = (n_full - done) // 2
    carry = lax.fori_loop(0, n_pair, lambda i, c: past_step(done + 2 * i, 2, c), carry)
    carry = lax.fori_loop(done + 2 * n_pair, n_full, lambda t, c: past_step(t, 1, c), carry)
    return lax.fori_loop(n_full, n_kv, edge_step, carry)


def _num_kv_tiles(q0, tq, tk, l_valid):
    kmax = jnp.minimum(l_valid, (((q0 + tq - 1) >> CHUNK_SHIFT) + 1) * CHUNK)
    return (kmax + tk - 1) // tk


def _dsa_kernel(qa_ref, iq_ref, iw_ref, k_ref, v_ref, ik_ref, o_ref, keys_ref, bias_ref, relpos_ref, *,
                tq, tk, n_past, l_valid, topk, index_bits):
    q0 = n_past + pl.program_id(1) * tq
    n_kv = _num_kv_tiles(q0, tq, tk, l_valid)
    row = lax.broadcasted_iota(I32, (tq, tk), 0)
    col = lax.broadcasted_iota(I32, (tq, tk), 1)
    qpos = q0 + row
    lane = lax.broadcasted_iota(I32, (tq, LANES), 1)

    iw = iw_ref[0] * (I_Q ** -0.5)
    iq = iq_ref[0].astype(F32)
    iq_heads = []
    for h in range(IDX_HEADS):
        pair = iq[:, (h // 2) * LANES:(h // 2 + 1) * LANES]
        keep = (lane < IDX_DIM) if h % 2 == 0 else (lane >= IDX_DIM)
        iq_heads.append(jnp.where(keep, pair, 0.0).astype(BF16))
    iq_all = jnp.concatenate(iq_heads, axis=0)

    def to_key(x):
        bits = lax.bitcast_convert_type(x, I32)
        return bits ^ ((bits >> 31) & 0x7FFFFFFF)

    def score_tiles(t, n, carry):
        top1, top2 = carry
        start = pl.multiple_of(t * tk, tk)
        ikt = ik_ref[0, pl.ds(start, n * tk), :]
        rel = jnp.maximum(_nt_dot(iq_all, ikt), 0.0)
        score = jnp.zeros((tq, n * tk), F32)
        for h in range(IDX_HEADS):
            score = score + iw[:, h:h + 1] * rel[h * tq:(h + 1) * tq]
        for i in range(n):
            kpos = start + i * tk + col
            adm = ((kpos >> CHUNK_SHIFT) <= (qpos >> CHUNK_SHIFT)) & (kpos < l_valid)
            masked = jnp.where(adm, score[:, i * tk:(i + 1) * tk], -jnp.inf)
            keys_ref[t + i] = to_key(masked)
            for blk in _lane_blocks(masked):
                top2 = jnp.maximum(top2, jnp.minimum(top1, blk))
                top1 = jnp.maximum(top1, blk)
        return top1, top2

    n_pair = n_kv // 2
    lows = jnp.full((tq, LANES), -jnp.inf, F32)
    tops = lax.fori_loop(0, n_pair, lambda i, c: score_tiles(2 * i, 2, c), (lows, lows))
    top1, top2 = lax.fori_loop(2 * n_pair, n_kv, lambda t, c: score_tiles(t, 1, c), tops)

    def from_key(k):
        return lax.bitcast_convert_type(k ^ ((k >> 31) & 0x7FFFFFFF), F32)

    def reduce_tiles(tile_fn, combine, init):
        def body(t, acc):
            return combine(acc, functools.reduce(combine, _lane_blocks(tile_fn(keys_ref[t], t))))
        return lax.fori_loop(0, n_kv, body, init)

    def count(pred):
        per_lane = reduce_tiles(lambda k, t: jnp.where(pred(k, t), 1.0, 0.0), jnp.add, jnp.zeros((tq, LANES), F32))
        return jnp.sum(per_lane, axis=-1, keepdims=True)

    def largest_at_most(bound):
        per_lane = reduce_tiles(lambda k, t: jnp.where(k <= bound, k, MASKED_KEY), jnp.maximum,
                                jnp.full((tq, LANES), MASKED_KEY, I32))
        return to_key(jnp.max(from_key(per_lane), axis=-1, keepdims=True))

    def halve(lo, hi, cnt_lo):
        gap = hi - lo
        mid = lo + lax.shift_right_logical(gap, 1) + (gap & 1)
        cnt = count(lambda k, t: k >= mid)
        keep = cnt >= topk
        return jnp.where(keep, mid, lo), jnp.where(keep, hi, mid - 1), jnp.where(keep, cnt, cnt_lo)

    def unfinished(lo, hi, cnt_lo):
        return (lo != hi) & (cnt_lo != topk)

    def any_of(flags):
        return jnp.max(jnp.where(flags, 1.0, 0.0))

    def any_wide(lo, hi, cnt_lo):
        wide = lax.shift_right_logical(hi - lo, NARROW_BITS) != 0
        return any_of(unfinished(lo, hi, cnt_lo) & wide)

    def halve_wide(state):
        lo, hi, cnt_lo = halve(*halve(*state[:3]))
        return lo, hi, cnt_lo, any_wide(lo, hi, cnt_lo)

    def step_down(state):
        lo, hi, cnt_lo = state[:3]
        todo = unfinished(lo, hi, cnt_lo)
        top = largest_at_most(hi)
        cnt = count(lambda k, t: k >= top)
        hit = cnt >= topk
        lo = jnp.where(todo & hit, top, lo)
        hi = jnp.where(todo, jnp.where(hit, top, top - 1), hi)
        cnt_lo = jnp.where(todo & hit, cnt, cnt_lo)
        lo, hi, cnt_lo = halve(lo, hi, cnt_lo)
        return lo, hi, cnt_lo, any_of(unfinished(lo, hi, cnt_lo))

    lo0 = to_key(jnp.min(top2, axis=-1, keepdims=True))
    hi0 = to_key(jnp.max(top1, axis=-1, keepdims=True))
    unknown = jnp.full((tq, 1), -1.0, F32)
    going = lambda state: state[3] > 0.0
    lo, hi, cnt_lo, _ = lax.while_loop(going, halve_wide, (lo0, hi0, unknown, any_wide(lo0, hi0, unknown)))
    thr = lax.while_loop(going, step_down, (lo, hi, cnt_lo, any_of(unfinished(lo, hi, cnt_lo))))[0]
    n_gt = count(lambda k, t: k > thr)

    def at_or_above(k, t):
        keep = k >= thr
        bias_ref[t] = jnp.where(keep & (k > MASKED_KEY), 0.0, NEG)
        return keep

    n_ge = count(at_or_above)
    want_ties = topk - n_gt
    need = ((n_ge - n_gt) > want_ties) & (thr > MASKED_KEY)

    @pl.when(jnp.max(jnp.where(need, 1.0, 0.0)) > 0.0)
    def _():
        def index_bit(it, j_lim):
            cand = j_lim | lax.shift_left(jnp.int32(1), index_bits - 1 - it)
            below = count(lambda k, t: (k == thr) & ((t * tk + col) < cand))
            return jnp.where(below <= want_ties - 1.0, cand, j_lim)
        j_lim = lax.fori_loop(0, index_bits, index_bit, jnp.zeros((tq, 1), I32))
        j_lim = jnp.where(need, j_lim, NO_LIMIT)

        def bias_tile(t, carry):
            k = keys_ref[t]
            sel = (k > thr) | ((k == thr) & ((t * tk + col) <= j_lim))
            bias_ref[t] = jnp.where(sel & (k > MASKED_KEY), 0.0, NEG)
            return carry

        lax.fori_loop(0, n_kv, bias_tile, 0)

    qk_scale = HEAD_DIM ** -0.5 * LOG2E
    n_full = jnp.minimum(q0, l_valid) // tk

    @pl.when((pl.program_id(0) == 0) & (pl.program_id(1) == 0))
    def _():
        relpos_ref[...] = _rel_pos(tq, WIDE_TILES * tk)

    for j in range(A_KV_HEADS):
        q4 = jnp.concatenate([qa_ref[0, :, (j * A_GROUP + g) * HEAD_DIM:(j * A_GROUP + g + 1) * HEAD_DIM]
                              for g in range(A_GROUP)], axis=0)
        slopes2 = [2.0 ** -(j * A_GROUP + g + 1) * LOG2E for g in range(A_GROUP)]
        slope_rows = jnp.concatenate([jnp.full((tq, LANES), s2, F32) for s2 in slopes2], axis=0)

        def tile_operands(t, n, j=j, q4=q4):
            start = pl.multiple_of(t * tk, tk)
            kt = k_ref[0, pl.ds(start, n * tk), j * HEAD_DIM:(j + 1) * HEAD_DIM]
            vt = v_ref[0, pl.ds(start, n * tk), j * HEAD_DIM:(j + 1) * HEAD_DIM]
            bias = bias_ref[t] if n == 1 else jnp.concatenate([bias_ref[t + i] for i in range(n)], axis=1)
            return start, _nt_dot(q4, kt), vt, bias

        def past_step(t, n, carry, slopes2=slopes2, slope_rows=slope_rows):
            start, s, vt, bias = tile_operands(t, n)
            lg = jnp.concatenate([s[g * tq:(g + 1) * tq] * qk_scale + (relpos_ref[:, :n * tk] * slopes2[g] + bias)
                                  for g in range(A_GROUP)], axis=0)
            off = slope_rows * (start - q0).astype(F32)
            return _softmax_step(lg, off, vt, carry)

        def edge_step(t, carry, slopes2=slopes2):
            start, s, vt, bias = tile_operands(t, 1)
            dist = jnp.abs(qpos - (start + col)).astype(F32)
            lg = jnp.concatenate([s[g * tq:(g + 1) * tq] * qk_scale - slopes2[g] * dist + bias
                                  for g in range(A_GROUP)], axis=0)
            return _softmax_step(lg, 0.0, vt, carry)

        out = _softmax_finish(_tile_loops(n_full, n_kv, past_step, edge_step,
                                          _softmax_init(A_GROUP * tq, HEAD_DIM)))
        for g in range(A_GROUP):
            h = j * A_GROUP + g
            o_ref[0, :, h * HEAD_DIM:(h + 1) * HEAD_DIM] = out[g * tq:(g + 1) * tq].astype(o_ref.dtype)


def _dsa_attention(zb, zf, keys, off, *, n_past, l_valid, tq, tk, topk, name):
    b, s, _ = zb.shape
    karr, kcol, varr, vcol, ikarr, ikcol = keys
    lp = karr.shape[1]
    n_tiles = lp // tk
    assert topk <= min(tk, 2 * LANES) and WIDE_TILES % 2 == 0
    kern = functools.partial(_dsa_kernel, tq=tq, tk=tk, n_past=n_past, l_valid=l_valid, topk=topk,
                             index_bits=lp.bit_length())
    blocks = (_nbytes((tq, A_Q + I_Q), BF16) + _nbytes((tq, LANES), F32) + 2 * _nbytes((lp, A_KV), BF16)
              + _nbytes((lp, LANES), BF16) + _nbytes((tq, A_Q), BF16) + _nbytes((tq, lp), F32)
              + 8 * _nbytes((A_GROUP * tq, tk), F32))
    return pl.pallas_call(
        kern,
        grid=(b, s // tq),
        in_specs=[pl.BlockSpec((1, tq, A_Q), lambda bi, qi: (bi, qi, off['aq'] // A_Q)),
                  pl.BlockSpec((1, tq, I_Q), lambda bi, qi: (bi, qi, off['iq'] // I_Q)),
                  pl.BlockSpec((1, tq, LANES), lambda bi, qi: (bi, qi, off['iw'] // LANES)),
                  pl.BlockSpec((1, lp, A_KV), lambda bi, qi: (bi, 0, kcol)),
                  pl.BlockSpec((1, lp, A_KV), lambda bi, qi: (bi, 0, vcol)),
                  pl.BlockSpec((1, lp, LANES), lambda bi, qi: (bi, 0, ikcol))],
        out_specs=pl.BlockSpec((1, tq, A_Q), lambda bi, qi: (bi, qi, 0)),
        out_shape=jax.ShapeDtypeStruct((b, s, A_Q), BF16),
        scratch_shapes=[pltpu.VMEM((n_tiles, tq, tk), I32), pltpu.VMEM((n_tiles, tq, tk), F32),
                        pltpu.VMEM((tq, WIDE_TILES * tk), F32)],
        compiler_params=_cparams(("arbitrary", "arbitrary"), blocks),
        name=name,
    )(zb, zb, zf, karr, varr, ikarr)


def _diff_kernel(q_ref, k_ref, v_ref, lq_ref, g_ref, o_ref, relbias_ref, *, tq, tk, n_past, l_valid, lam_init):
    h = pl.program_id(1)
    q0 = n_past + pl.program_id(2) * tq
    n_kv = _num_kv_tiles(q0, tq, tk, l_valid)
    row = lax.broadcasted_iota(I32, (tq, tk), 0)
    col = lax.broadcasted_iota(I32, (tq, tk), 1)
    qpos = q0 + row
    slope2 = jnp.float32(1.0)
    for hh in range(B_HEADS):
        slope2 = jnp.where(h == hh, jnp.float32(2.0 ** (-8.0 * (hh + 1) / B_HEADS) * LOG2E), slope2)
    qk_scale = HEAD_DIM ** -0.5 * LOG2E
    n_full = jnp.minimum(q0, l_valid) // tk

    @pl.when(pl.program_id(2) == 0)
    def _():
        relbias_ref[...] = _rel_pos(tq, WIDE_TILES * tk) * slope2

    qc = [q_ref[0, :, c * HEAD_DIM:(c + 1) * HEAD_DIM] for c in range(2)]

    def tile_operands(t, n):
        start = pl.multiple_of(t * tk, tk)
        s = [_nt_dot(qc[c], k_ref[0, pl.ds(start, n * tk), c * HEAD_DIM:(c + 1) * HEAD_DIM]) for c in range(2)]
        return start, s, v_ref[0, pl.ds(start, n * tk), :]

    def past_step(t, n, carry):
        start, s, vt = tile_operands(t, n)
        lg = jnp.concatenate([s[c] * qk_scale + relbias_ref[:, :n * tk] for c in range(2)], axis=0)
        return _softmax_step(lg, slope2 * (start - q0).astype(F32), vt, carry)

    def edge_step(t, carry):
        start, s, vt = tile_operands(t, 1)
        kpos = start + col
        adm = ((kpos >> CHUNK_SHIFT) <= (qpos >> CHUNK_SHIFT)) & (kpos < l_valid)
        bias = jnp.where(adm, -slope2 * jnp.abs(qpos - kpos).astype(F32), NEG)
        lg = jnp.concatenate([s[c] * qk_scale + bias for c in range(2)], axis=0)
        return _softmax_step(lg, 0.0, vt, carry)

    out = _softmax_finish(_tile_loops(n_full, n_kv, past_step, edge_step,
                                      _softmax_init(2 * tq, 2 * HEAD_DIM)))
    lq = lq_ref[...]
    lam = (jnp.exp(jnp.sum(lq[0:1] * lq[1:2], axis=-1, keepdims=True))
           - jnp.exp(jnp.sum(lq[2:3] * lq[3:4], axis=-1, keepdims=True)) + lam_init)
    o = out[:tq] - lam * out[tq:]
    y = o * lax.rsqrt(jnp.mean(o * o, axis=-1, keepdims=True) + SUBLN_EPS)
    y = (y * g_ref[...]) * (1.0 - lam_init)
    o_ref[0] = y.astype(o_ref.dtype)


def _diff_attention(zb, keys, lam_qk_l, g_subln_l, off, *, n_past, l_valid, tq, tk, lam_init, name):
    b, s, _ = zb.shape
    karr, kcol0, varr, vcol0 = keys
    lp = karr.shape[1]
    w = 2 * HEAD_DIM
    kern = functools.partial(_diff_kernel, tq=tq, tk=tk, n_past=n_past, l_valid=l_valid, lam_init=lam_init)
    blocks = (2 * _nbytes((tq, w), BF16) + 2 * _nbytes((lp, w), BF16) + 8 * _nbytes((2 * tq, tk), F32))
    return pl.pallas_call(
        kern,
        grid=(b, B_HEADS, s // tq),
        in_specs=[pl.BlockSpec((1, tq, w), lambda bi, h, qi: (bi, qi, off['bq'] // w + h)),
                  pl.BlockSpec((1, lp, w), lambda bi, h, qi: (bi, 0, kcol0 + h)),
                  pl.BlockSpec((1, lp, w), lambda bi, h, qi: (bi, 0, vcol0 + h)),
                  pl.BlockSpec((4, HEAD_DIM), lambda bi, h, qi: (0, 0)),
                  pl.BlockSpec((1, w), lambda bi, h, qi: (0, 0))],
        out_specs=pl.BlockSpec((1, tq, w), lambda bi, h, qi: (bi, qi, h)),
        out_shape=jax.ShapeDtypeStruct((b, s, B_V), BF16),
        scratch_shapes=[pltpu.VMEM((tq, WIDE_TILES * tk), F32)],
        compiler_params=_cparams(("arbitrary", "arbitrary", "arbitrary"), blocks),
        name=name,
    )(zb, karr, varr, lam_qk_l, g_subln_l.reshape(1, w))


def _row_gather(src_ref, idx_ref, idx_base, idx_stride, dst_ref, sem, n_rows):
    def row_copy(r):
        return pltpu.make_async_copy(src_ref.at[pl.ds(idx_ref[idx_base + r * idx_stride], 1)],
                                     dst_ref.at[pl.ds(r, 1)], sem)

    def start():
        lax.fori_loop(0, n_rows, lambda r, c: (row_copy(r).start(), c)[1], 0, unroll=ISSUE_UNROLL)

    def wait():
        pltpu.make_async_copy(dst_ref.at[pl.ds(0, n_rows)], dst_ref.at[pl.ds(0, n_rows)], sem).wait()

    return start, wait


def _moe_up_kernel(eid_ref, tok_ref, h_ref, wg_ref, wu_ref, o_ref, rows_ref, rows16_ref, sems, *,
                   blk, n_blocks, n_col_steps, rows_per_step):
    i, j = pl.program_id(0), pl.program_id(1)
    slot = i % 2
    n_issued = n_col_steps * rows_per_step

    def row_copy(block, buf, r_dst, r_src):
        return pltpu.make_async_copy(h_ref.at[pl.ds(tok_ref[block * blk + r_src], 1)],
                                     rows_ref.at[buf, pl.ds(r_dst, 1)], sems.at[buf])

    def wait_all(buf):
        pltpu.make_async_copy(rows_ref.at[buf, pl.ds(0, n_issued)], rows_ref.at[buf, pl.ds(0, n_issued)],
                              sems.at[buf]).wait()

    @pl.when((i == 0) & (j == 0))
    def _():
        def first(r, c):
            row_copy(0, 0, r, jnp.minimum(r, blk - 1)).start()
            return c
        lax.fori_loop(0, n_issued, first, 0, unroll=ISSUE_UNROLL)

    @pl.when(j == 0)
    def _():
        wait_all(slot)
        rows16_ref[...] = rows_ref[slot, :blk].astype(BF16)

    nxt = jnp.minimum(i + 1, n_blocks - 1)
    for r in range(rows_per_step):
        r_dst = j * rows_per_step + r
        row_copy(nxt, 1 - slot, r_dst, jnp.minimum(r_dst, blk - 1)).start()

    a = rows16_ref[...]
    u = jnp.dot(a, wg_ref[0], preferred_element_type=F32)
    v = jnp.dot(a, wu_ref[0], preferred_element_type=F32)
    o_ref[...] = ((u * jax.nn.sigmoid(u)) * v).astype(o_ref.dtype)

    @pl.when((i == n_blocks - 1) & (j == n_col_steps - 1))
    def _():
        wait_all(1 - slot)


def _moe_up(h2, tok_buf, block_expert, w_gate, w_up, *, blk, name):
    p = tok_buf.shape[0]
    d = h2.shape[1]
    fp = w_gate.shape[-1]
    tn = _tile(fp, 512)
    n_blocks = p // blk
    n_col_steps = fp // tn
    rows_per_step = _round_up(-(-blk // n_col_steps), 8)
    n_rows_buf = n_col_steps * rows_per_step
    blocks = (_nbytes((n_rows_buf, d), F32) + _nbytes((blk, d), BF16) // 2 + 2 * _nbytes((d, tn), BF16)
              + _nbytes((blk, tn), BF16) + _nbytes((blk, tn), F32))
    return pl.pallas_call(
        functools.partial(_moe_up_kernel, blk=blk, n_blocks=n_blocks, n_col_steps=n_col_steps,
                          rows_per_step=rows_per_step),
        grid_spec=pltpu.PrefetchScalarGridSpec(
            num_scalar_prefetch=2, grid=(n_blocks, n_col_steps),
            in_specs=[pl.BlockSpec(memory_space=pl.ANY),
                      pl.BlockSpec((1, d, tn), lambda i, j, e, tok: (e[i], 0, j)),
                      pl.BlockSpec((1, d, tn), lambda i, j, e, tok: (e[i], 0, j))],
            out_specs=pl.BlockSpec((blk, tn), lambda i, j, e, tok: (i, j)),
            scratch_shapes=[pltpu.VMEM((2, n_rows_buf, d), F32), pltpu.VMEM((blk, d), BF16),
                            pltpu.SemaphoreType.DMA((2,))]),
        out_shape=jax.ShapeDtypeStruct((p, fp), BF16),
        compiler_params=_cparams(("arbitrary", "arbitrary"), blocks),
        name=name,
    )(block_expert, tok_buf, h2, w_gate, w_up)


def _combine_kernel(dest_ref, x_ref, y_ref, rg_ref, gate_ref, o_ref, picked_ref, sems, *, tm, n_steps):
    i = pl.program_id(0)
    slot = i % 2

    def fetch(step, buf):
        base = step * tm * TOP_K_EXPERTS
        return [_row_gather(y_ref, dest_ref, base + k, TOP_K_EXPERTS, picked_ref.at[buf, k], sems.at[buf, k], tm)
                for k in range(TOP_K_EXPERTS)]

    @pl.when(i == 0)
    def _():
        for start, _ in fetch(0, 0):
            start()

    for _, wait in fetch(i, slot):
        wait()

    @pl.when(i + 1 < n_steps)
    def _():
        for start, _ in fetch(i + 1, 1 - slot):
            start()

    rg = rg_ref[...]
    f = rg[:, 0:1] * picked_ref[slot, 0] + rg[:, 1:2] * picked_ref[slot, 1]
    o_ref[...] = x_ref[...] + gate_ref[0] * f


def _moe_combine(x2, yb, dest, rgate, gate_f, seq, *, name):
    t, d = x2.shape
    tm = _tile(seq, 256)
    per_b = seq // tm
    return pl.pallas_call(
        functools.partial(_combine_kernel, tm=tm, n_steps=t // tm),
        grid_spec=pltpu.PrefetchScalarGridSpec(
            num_scalar_prefetch=1, grid=(t // tm,),
            in_specs=[pl.BlockSpec((tm, d), lambda i, dst: (i, 0)),
                      pl.BlockSpec(memory_space=pl.ANY),
                      pl.BlockSpec((tm, LANES), lambda i, dst: (i, 0)),
                      pl.BlockSpec((1, 1, d), lambda i, dst: (i // per_b, 0, 0))],
            out_specs=pl.BlockSpec((tm, d), lambda i, dst: (i, 0)),
            scratch_shapes=[pltpu.VMEM((2, TOP_K_EXPERTS, tm, d), F32),
                            pltpu.SemaphoreType.DMA((2, TOP_K_EXPERTS))]),
        out_shape=jax.ShapeDtypeStruct((t, d), F32),
        compiler_params=_cparams(("arbitrary",), 4 * _nbytes((tm, d), F32)),
        name=name,
    )(dest, x2, yb, rgate, gate_f)


STATE_COLS = (('ak', A_KV), ('av', A_KV), ('ik', LANES), ('bk', B_QK), ('bv', B_V))


def _state_tails():
    return ((A_KV_HEADS, HEAD_DIM), (A_KV_HEADS, HEAD_DIM), (IDX_DIM,), (B_HEADS, 2, HEAD_DIM),
            (B_HEADS, 2 * HEAD_DIM))


def _state_kernel(*refs, depth):
    n = len(STATE_COLS)
    ins, outs = refs[:depth * n], refs[depth * n:]
    for l in range(depth):
        @pl.when(pl.program_id(0) == l)
        def _(l=l):
            ak, av, ik, bk, bv = ins[l * n:(l + 1) * n]
            for j in range(A_KV_HEADS):
                outs[0][0, 0, :, j, :] = ak[0, :, j * HEAD_DIM:(j + 1) * HEAD_DIM]
                outs[1][0, 0, :, j, :] = av[0, :, j * HEAD_DIM:(j + 1) * HEAD_DIM]
            outs[2][0, 0] = ik[0, :, :IDX_DIM]
            for h in range(B_HEADS):
                for c in range(2):
                    lo = (2 * h + c) * HEAD_DIM
                    outs[3][0, 0, :, h, c, :] = bk[0, :, lo:lo + HEAD_DIM]
                outs[4][0, 0, :, h, :] = bv[0, :, 2 * h * HEAD_DIM:2 * (h + 1) * HEAD_DIM]


def _state_rows(zf3_layers, off, *, name):
    depth = len(zf3_layers)
    b, s, _ = zf3_layers[0].shape
    ts = _tile(s, 256)
    in_specs, args = [], []
    for l in range(depth):
        for cname, width in STATE_COLS:
            blk = off[cname] // width
            in_specs.append(pl.BlockSpec(
                (1, ts, width),
                lambda li, bi, i, l=l, blk=blk: (jnp.where(li == l, bi, 0), jnp.where(li == l, i, 0), blk)))
            args.append(zf3_layers[l])
    tails = _state_tails()
    out_specs = [pl.BlockSpec((1, 1, ts) + t, lambda li, bi, i, nt=len(t): (li, bi, i) + (0,) * nt) for t in tails]
    out_shape = [jax.ShapeDtypeStruct((depth, b, s) + t, F32) for t in tails]
    blocks = (depth + 1) * sum(_nbytes((ts, w), F32) for _, w in STATE_COLS)
    return pl.pallas_call(
        functools.partial(_state_kernel, depth=depth),
        grid=(depth, b, s // ts), in_specs=in_specs, out_specs=out_specs, out_shape=out_shape,
        compiler_params=_cparams(("arbitrary", "arbitrary", "arbitrary"), blocks), name=name,
    )(*args)


def _pack_kernel(c_hbm, new_ref, o_ref, buf, sems, *, layer, tk, n_cached_tiles, tail):
    b, i = pl.program_id(0), pl.program_id(1)
    slot = i % 2
    piece = tail[-1]
    heads = list(itertools.product(*[range(t) for t in tail[:-1]]))

    def tile_copies(tile, s):
        rows = pl.ds(tile * tk, tk)
        return [pltpu.make_async_copy(c_hbm.at[(layer, b, rows) + idx + (slice(None),)], buf.at[s, n], sems.at[s, n])
                for n, idx in enumerate(heads)]

    @pl.when(i == 0)
    def _():
        for c in tile_copies(0, 0):
            c.start()

    @pl.when(i < n_cached_tiles)
    def _():
        for c in tile_copies(i, slot):
            c.wait()

        @pl.when(i + 1 < n_cached_tiles)
        def _():
            for c in tile_copies(i + 1, 1 - slot):
                c.start()

        for n in range(len(heads)):
            o_ref[0, :, n * piece:(n + 1) * piece] = buf[slot, n].astype(BF16)

    @pl.when(i >= n_cached_tiles)
    def _():
        o_ref[0] = new_ref[0]


def _pack_keys(cache, layer, new_rows, *, tk, name):
    _, b, p = cache.shape[:3]
    tail = cache.shape[3:]
    w = math.prod(tail)
    assert p % tk == 0 and new_rows.shape == (b, tk, w)
    n_cached_tiles = p // tk
    n_heads = w // tail[-1]
    return pl.pallas_call(
        functools.partial(_pack_kernel, layer=layer, tk=tk, n_cached_tiles=n_cached_tiles, tail=tail),
        grid=(b, n_cached_tiles + 1),
        in_specs=[pl.BlockSpec(memory_space=pl.ANY),
                  pl.BlockSpec((1, tk, w), lambda bi, i: (bi, 0, 0))],
        out_specs=pl.BlockSpec((1, tk, w), lambda bi, i: (bi, i, 0)),
        out_shape=jax.ShapeDtypeStruct((b, p + tk, w), BF16),
        scratch_shapes=[pltpu.VMEM((2, n_heads, tk, tail[-1]), F32), pltpu.SemaphoreType.DMA((2, n_heads))],
        compiler_params=_cparams(("arbitrary", "arbitrary"), _nbytes((tk, w), F32) + 2 * _nbytes((tk, w), BF16)),
        name=name,
    )(cache, new_rows)


def _silu_mul(accs, extras):
    u, v = accs
    return [(u * jax.nn.sigmoid(u)) * v]


def _moe_ffn(x2, h2, ridx, rgate, gate_f, seq, w_gate, w_up, w_down, *, tag):
    t, d = x2.shape
    fp = w_gate.shape[-1]
    a = t * TOP_K_EXPERTS
    blk = _tile(a, 512)
    n_blocks = -(-(a + N_EXPERTS * (blk - 1)) // blk)
    p = n_blocks * blk
    e_flat = ridx[:, :TOP_K_EXPERTS].reshape(-1)
    onehot = (e_flat[:, None] == jnp.arange(N_EXPERTS, dtype=I32)[None, :]).astype(I32)
    rank = jnp.sum((jnp.cumsum(onehot, axis=0) - onehot) * onehot, axis=1)
    counts = jnp.sum(onehot, axis=0)
    padded = ((counts + blk - 1) // blk) * blk
    pends = jnp.cumsum(padded)
    pstarts = pends - padded
    dest = (pstarts[e_flat] + rank).astype(I32)
    tok_buf = jnp.zeros((p,), I32).at[dest].set(jnp.arange(a, dtype=I32) // TOP_K_EXPERTS)
    block_expert = jnp.clip(jnp.searchsorted(pends, jnp.arange(n_blocks, dtype=I32) * blk, side='right'),
                            0, N_EXPERTS - 1).astype(I32)

    act = _moe_up(h2, tok_buf, block_expert, w_gate, w_up, blk=blk, name=f"moe_up_{tag}")
    (yb,) = _matmul([act], [w_down], [0], [], lambda accs, ex: accs, [F32], d, tm=blk, tn=_tile(d, 512),
                    eids=block_expert, name=f"moe_down_{tag}")
    return _moe_combine(x2, yb, dest, rgate, gate_f, seq, name=f"moe_combine_{tag}")


def _trunk(x, mods, past_all, wts, tag):
    b, s, d = x.shape
    t = b * s
    depth = len(wts['w_in'])
    off, nz = _z_layout(d)
    x2 = x.reshape(t, d)
    tn_d = _tile(d, 512)
    tm_b = _tile(s, 512)
    tm_f = _tile(t, 1024)
    per_b = s // tm_b
    fuse_norm = s % tm_f == 0
    zf3_layers = []
    for l in range(depth):
        shift_a, scale_a, gate_a, shift_f, scale_f, gate_f = [
            mods[l][:, i * d:(i + 1) * d].reshape(b, 1, d) for i in range(N_ADA)]
        batch_tile = lambda i, j, *_: (i // per_b, 0, j)

        if fuse_norm:
            h, norm_a = x2, (wts['g_attn'][l], scale_a, shift_a, s)
        else:
            h, norm_a = _norm(x2, wts['g_attn'][l], s, scale=scale_a, shift=shift_a, out_dtype=BF16,
                              name=f"norm_attn_{tag}{l}"), None
        zf, zb = _matmul([h], [wts['w_in'][l]], [0], [], lambda accs, ex: [accs[0], accs[0]], [F32, BF16], nz,
                         tm=tm_f, tn=IN_PROJ_TN, norm=norm_a, name=f"in_proj_{tag}{l}")
        zf3, zb3 = zf.reshape(b, s, nz), zb.reshape(b, s, nz)
        cut = lambda z, name, width: z[:, :, off[name]:off[name] + width]
        zf3_layers.append(zf3)

        if past_all is None:
            n_past, l_valid = 0, s
            tk = _tile(s, 512)
            tq_a, tq_b = _tile(s, 128), _tile(s, 256)
            dsa_keys = (zb3, off['ak'] // A_KV, zb3, off['av'] // A_KV, zb3, off['ik'] // LANES)
            diff_keys = (zb3, off['bk'] // (2 * HEAD_DIM), zb3, off['bv'] // (2 * HEAD_DIM))
        else:
            n_past = past_all[0].shape[2]
            l_valid = n_past + s
            tk = 512
            tq_a = tq_b = s

            def new_rows(name, width):
                return jnp.concatenate([cut(zb3, name, width), jnp.zeros((b, tk - s, width), BF16)], axis=1)

            def with_past(ci, name, width):
                return _pack_keys(past_all[ci], l, new_rows(name, width), tk=tk, name=f"pack_{name}_{tag}{l}")

            ik_past = past_all[2][l].astype(BF16)
            ik_all = jnp.concatenate([jnp.concatenate([ik_past, ik_past], axis=-1), new_rows('ik', LANES)], axis=1)
            dsa_keys = (with_past(0, 'ak', A_KV), 0, with_past(1, 'av', A_KV), 0, ik_all, 0)
            diff_keys = (with_past(3, 'bk', B_QK), 0, with_past(4, 'bv', B_V), 0)
        topk = min(TOPK_MAX, l_valid // 4)
        oa = _dsa_attention(zb3, zf3, dsa_keys, off, n_past=n_past, l_valid=l_valid, tq=tq_a, tk=tk,
                            topk=topk, name=f"dsa_{tag}{l}")
        lam_init = 0.8 - 0.6 * math.exp(-0.3 * l)
        ob = _diff_attention(zb3, diff_keys, wts['lam_qk'][l], wts['g_subln'][l], off, n_past=n_past,
                             l_valid=l_valid, tq=tq_b, tk=tk, lam_init=lam_init, name=f"diff_{tag}{l}")

        def merge_epi(accs, ex):
            return [jax.nn.sigmoid(ex[0]) * accs[0] + jax.nn.sigmoid(ex[1]) * accs[1]]

        (merged,) = _matmul(
            [oa.reshape(t, A_Q), ob.reshape(t, B_V)], [wts['w_out_a'][l], wts['w_out_b'][l]], [0, 1],
            [(zf, (tm_f, tn_d), lambda i, j, *_: (i, off['ga'] // tn_d + j)),
             (zf, (tm_f, tn_d), lambda i, j, *_: (i, off['gb'] // tn_d + j))],
            merge_epi, [BF16], d, tm=tm_f, tn=tn_d, name=f"merge_{tag}{l}")

        def resid_epi(accs, ex):
            return [ex[0] + ex[1][0] * accs[0]]

        (x2,) = _matmul([merged], [wts['w_out'][l]], [0],
                        [(x2, (tm_b, tn_d), lambda i, j, *_: (i, j)), (gate_a, (1, 1, tn_d), batch_tile)],
                        resid_epi, [F32], d, tm=tm_b, tn=tn_d, name=f"out_proj_{tag}{l}")

        if l % 2 == 0:
            i = l // 2
            if fuse_norm:
                h, norm_f = x2, (wts['g_ffn'][l], scale_f, shift_f, s)
            else:
                h, norm_f = _norm(x2, wts['g_ffn'][l], s, scale=scale_f, shift=shift_f, out_dtype=BF16,
                                  name=f"norm_ffn_{tag}{l}"), None
            fp = wts['w_ff_gate'][i].shape[-1]
            (act,) = _matmul([h], [wts['w_ff_gate'][i], wts['w_ff_up'][i]], [0, 0], [], _silu_mul, [BF16], fp,
                             tm=tm_f, tn=_tile(fp, 512), norm=norm_f, name=f"ffn_up_{tag}{l}")
            (x2,) = _matmul([act], [wts['w_ff_down'][i]], [0],
                            [(x2, (tm_b, tn_d), lambda i, j, *_: (i, j)), (gate_f, (1, 1, tn_d), batch_tile)],
                            resid_epi, [F32], d, tm=tm_b, tn=tn_d, name=f"ffn_down_{tag}{l}")
        else:
            i = l // 2
            h2, ridx, rgate = _norm(x2, wts['g_ffn'][l], s, scale=scale_f, shift=shift_f,
                                    router=(wts['w_router'][i], wts['b_router'][i]), out_dtype=F32,
                                    name=f"norm_router_{tag}{l}")
            x2 = _moe_ffn(x2, h2, ridx, rgate, gate_f, s, wts['w_moe_gate'][i], wts['w_moe_up'][i],
                          wts['w_moe_down'][i], tag=f"{tag}{l}")
    y = _norm(x2, wts['g_final'], s, out_dtype=F32, name=f"norm_final_{tag}")
    state = tuple(_state_rows(zf3_layers, off, name=f"state_rows_{tag}"))
    return y.reshape(b, s, d), state


def _prep_weights(w_in, w_out_a, w_out_b, w_out, w_ff_gate, w_ff_up, w_ff_down, w_router, b_router,
                  w_moe_gate, w_moe_up, w_moe_down, d_model):
    f = w_ff_gate.shape[-1]
    fp = _round_up(f, 512)
    pad_cols = lambda w: jnp.pad(w, [(0, 0)] * (w.ndim - 1) + [(0, fp - f)]).astype(BF16)
    pad_rows = lambda w: jnp.pad(w, [(0, 0)] * (w.ndim - 2) + [(0, fp - f), (0, 0)]).astype(BF16)
    n_moe = w_router.shape[0]
    return dict(
        w_in=[_prep_w_in(w_in[l], d_model) for l in range(w_in.shape[0])],
        w_out_a=w_out_a.astype(BF16), w_out_b=w_out_b.astype(BF16), w_out=w_out.astype(BF16),
        w_ff_gate=pad_cols(w_ff_gate), w_ff_up=pad_cols(w_ff_up), w_ff_down=pad_rows(w_ff_down),
        w_router=[jnp.pad(w_router[i], ((0, 0), (0, LANES - N_EXPERTS))) for i in range(n_moe)],
        b_router=[jnp.pad(b_router[i], (0, LANES - N_EXPERTS)).reshape(1, LANES) for i in range(n_moe)],
        w_moe_gate=pad_cols(w_moe_gate), w_moe_up=pad_cols(w_moe_up), w_moe_down=pad_rows(w_moe_down),
    )


def kernel(x_prompt, x_sample, c_prompt, c_sample, cache_dsa_k, cache_dsa_v, cache_idx_k, cache_diff_k, cache_diff_v, w_ada, b_ada, g_attn, w_in, w_out_a, w_out_b, w_out, lam_qk, g_subln, g_ffn, w_ff_gate, w_ff_up, w_ff_down, w_router, b_router, w_moe_gate, w_moe_up, w_moe_down, g_final):
    d = x_prompt.shape[-1]
    wts = _prep_weights(w_in, w_out_a, w_out_b, w_out, w_ff_gate, w_ff_up, w_ff_down, w_router, b_router,
                        w_moe_gate, w_moe_up, w_moe_down, d)
    wts.update(g_attn=g_attn, g_ffn=g_ffn, g_final=g_final, lam_qk=lam_qk, g_subln=g_subln)
    nb_p = c_prompt.shape[0]
    mods = _ada_mod(jnp.concatenate([c_prompt, c_sample], axis=0), w_ada, b_ada)
    y_p, st_p = _trunk(x_prompt, mods[:, :nb_p], None, wts, "p")
    past_all = (cache_dsa_k, cache_dsa_v, cache_idx_k, cache_diff_k, cache_diff_v)
    y_s, st_s = _trunk(x_sample, mods[:, nb_p:], past_all, wts, "s")
    return (y_p, y_s) + st_p + st_s
```

```python
import functools
import itertools
import math

import jax
import jax.numpy as jnp
from jax import lax
from jax.experimental import pallas as pl
from jax.experimental.pallas import tpu as pltpu

F32, BF16, I32 = jnp.float32, jnp.bfloat16, jnp.int32

CHUNK = 64
CHUNK_SHIFT = CHUNK.bit_length() - 1
HEAD_DIM = 128
A_HEADS = 8
A_KV_HEADS = 2
A_GROUP = A_HEADS // A_KV_HEADS
IDX_HEADS = 8
IDX_DIM = 64
TOPK_MAX = 256
B_HEADS = 4
N_EXPERTS = 8
TOP_K_EXPERTS = 2
N_ADA = 6
NORM_EPS = 1e-6
SUBLN_EPS = 1e-5
A_Q = A_HEADS * HEAD_DIM
A_KV = A_KV_HEADS * HEAD_DIM
I_Q = IDX_HEADS * IDX_DIM
B_QK = B_HEADS * 2 * HEAD_DIM
B_V = B_HEADS * 2 * HEAD_DIM

LANES = 128
VMEM_BYTES_V7X = 64 * 2 ** 20
NEG = -1e30
NO_LIMIT = 2 ** 30
LOG2E = math.log2(math.e)
IN_PROJ_TN = 512
MASKED_KEY = 0x807FFFFF - 2 ** 32
WIDE_TILES = 4
ISSUE_UNROLL = 8
NARROW_BITS = 12


def _cparams(dims, block_bytes):
    limit = min(max(2 * int(block_bytes) + (8 << 20), 32 << 20), VMEM_BYTES_V7X - (6 << 20))
    return pltpu.CompilerParams(dimension_semantics=dims, vmem_limit_bytes=limit)


def _nbytes(shape, dtype):
    return math.prod(shape) * jnp.dtype(dtype).itemsize


def _tile(n, pref):
    if n <= pref:
        return n
    t = pref
    while n % t:
        t //= 2
    return t


def _round_up(n, m):
    return -(-n // m) * m


def _z_layout(d_model):
    off, o = {}, 0
    for name, n in (('aq', A_Q), ('ak', A_KV), ('av', A_KV), ('iq', I_Q), ('bq', B_QK), ('bk', B_QK),
                    ('bv', B_V), ('ga', d_model), ('gb', d_model), ('ik', LANES), ('iw', LANES)):
        off[name] = o
        o += n
    return off, _round_up(o, IN_PROJ_TN)


def _prep_w_in(w, d_model):
    sizes = (A_Q, A_KV, A_KV, I_Q, IDX_DIM, IDX_HEADS, B_QK, B_QK, B_V, d_model, d_model)
    names = ('aq', 'ak', 'av', 'iq', 'ik', 'iw', 'bq', 'bk', 'bv', 'ga', 'gb')
    parts, o = {}, 0
    for n, s in zip(names, sizes):
        parts[n] = w[:, o:o + s]
        o += s
    cols = [parts[n] for n in ('aq', 'ak', 'av', 'iq', 'bq', 'bk', 'bv', 'ga', 'gb')]
    cols += [parts['ik'], parts['ik'], parts['iw']]
    used = sum(c.shape[1] for c in cols)
    cols.append(jnp.zeros((w.shape[0], _z_layout(d_model)[1] - used), w.dtype))
    return jnp.concatenate(cols, axis=1).astype(BF16)


def _ada_kernel(c_ref, w_ref, b_ref, o_ref):
    c = c_ref[...]
    s = c * jax.nn.sigmoid(c)
    o_ref[0] = jnp.dot(s.astype(BF16), w_ref[0].astype(BF16), preferred_element_type=F32) + b_ref[0]


def _ada_mod(c_all, w_ada, b_ada):
    depth, d, n = w_ada.shape
    nb = c_all.shape[0]
    tn = _tile(n, 1024)
    blocks = _nbytes((nb, d), F32) + _nbytes((d, tn), F32) * 2 + _nbytes((nb, tn), F32)
    return pl.pallas_call(
        _ada_kernel,
        grid=(depth, n // tn),
        in_specs=[pl.BlockSpec((nb, d), lambda l, j: (0, 0)),
                  pl.BlockSpec((1, d, tn), lambda l, j: (l, 0, j)),
                  pl.BlockSpec((1, 1, tn), lambda l, j: (l, 0, j))],
        out_specs=pl.BlockSpec((1, nb, tn), lambda l, j: (l, 0, j)),
        out_shape=jax.ShapeDtypeStruct((depth, nb, n), F32),
        compiler_params=_cparams(("arbitrary", "arbitrary"), blocks),
        name="ada_mod",
    )(c_all, w_ada, b_ada.reshape(depth, 1, n))


def _norm_kernel(*refs, modulated, router, eps):
    refs = list(refs)
    x_ref, g_ref = refs[:2]
    pos = 2
    x = x_ref[...]
    y = x * lax.rsqrt(jnp.mean(x * x, axis=-1, keepdims=True) + eps)
    y = y * g_ref[...]
    if modulated:
        scale_ref, shift_ref = refs[pos:pos + 2]
        pos += 2
        y = y * (1.0 + scale_ref[0]) + shift_ref[0]
    if router:
        wr_ref, br_ref = refs[pos:pos + 2]
        pos += 2
    o_ref = refs[pos]
    o_ref[...] = y.astype(o_ref.dtype)
    if router:
        idx_ref, gate_ref = refs[pos + 1:pos + 3]
        logits = jnp.dot(y, wr_ref[...], preferred_element_type=F32,
                         precision=lax.Precision.HIGHEST) + br_ref[...]
        lane = lax.broadcasted_iota(I32, logits.shape, 1)
        logits = jnp.where(lane < N_EXPERTS, logits, -jnp.inf)
        m1 = jnp.max(logits, axis=-1, keepdims=True)
        i1 = jnp.min(jnp.where(logits == m1, lane, LANES), axis=-1, keepdims=True)
        rest = jnp.where(lane == i1, -jnp.inf, logits)
        m2 = jnp.max(rest, axis=-1, keepdims=True)
        i2 = jnp.min(jnp.where(rest == m2, lane, LANES), axis=-1, keepdims=True)
        e = jnp.exp(m2 - m1)
        g1 = 1.0 / (1.0 + e)
        g2 = e / (1.0 + e)
        idx_ref[...] = jnp.where(lane == 0, i1, jnp.where(lane == 1, i2, 0))
        gate_ref[...] = jnp.where(lane == 0, g1, jnp.where(lane == 1, g2, 0.0))


def _norm(x2, g, seq, *, scale=None, shift=None, router=None, out_dtype, eps=NORM_EPS, name):
    t, d = x2.shape
    tm = _tile(seq, 256)
    per_b = seq // tm
    in_specs = [pl.BlockSpec((tm, d), lambda i: (i, 0)), pl.BlockSpec((1, d), lambda i: (0, 0))]
    args = [x2, g.reshape(1, d)]
    if scale is not None:
        in_specs += [pl.BlockSpec((1, 1, d), lambda i: (i // per_b, 0, 0))] * 2
        args += [scale, shift]
    out_specs = [pl.BlockSpec((tm, d), lambda i: (i, 0))]
    out_shape = [jax.ShapeDtypeStruct((t, d), out_dtype)]
    if router is not None:
        w_r, b_r = router
        in_specs += [pl.BlockSpec((d, LANES), lambda i: (0, 0)), pl.BlockSpec((1, LANES), lambda i: (0, 0))]
        args += [w_r, b_r]
        out_specs += [pl.BlockSpec((tm, LANES), lambda i: (i, 0))] * 2
        out_shape += [jax.ShapeDtypeStruct((t, LANES), I32), jax.ShapeDtypeStruct((t, LANES), F32)]
    blocks = 3 * _nbytes((tm, d), F32) + _nbytes((d, LANES), F32)
    outs = pl.pallas_call(
        functools.partial(_norm_kernel, modulated=scale is not None, router=router is not None, eps=eps),
        grid=(t // tm,), in_specs=in_specs, out_specs=out_specs, out_shape=out_shape,
        compiler_params=_cparams(("arbitrary",), blocks), name=name,
    )(*args)
    return outs if router is not None else outs[0]


def _mm_kernel(*refs, n_a, a_of_w, n_extra, epi, has_eids, normed):
    refs = list(refs)
    if has_eids:
        refs = refs[1:]
    n_w = len(a_of_w)
    a_refs = refs[:n_a]
    w_refs = refs[n_a:n_a + n_w]
    pos = n_a + n_w
    if normed:
        g_ref, scale_ref, shift_ref = refs[pos:pos + 3]
        pos += 3
        h_ref = refs.pop()

        @pl.when(pl.program_id(1) == 0)
        def _():
            x = a_refs[0][...]
            y = x * lax.rsqrt(jnp.mean(x * x, axis=-1, keepdims=True) + NORM_EPS)
            y = y * g_ref[...]
            h_ref[...] = (y * (1.0 + scale_ref[0]) + shift_ref[0]).astype(BF16)

        a_vals = [h_ref[...]]
    else:
        a_vals = [a[...].astype(BF16) for a in a_refs]
    e_refs = refs[pos:pos + n_extra]
    o_refs = refs[pos + n_extra:]
    accs = []
    for ai, w_ref in zip(a_of_w, w_refs):
        w = w_ref[0] if len(w_ref.shape) == 3 else w_ref[...]
        accs.append(jnp.dot(a_vals[ai], w, preferred_element_type=F32))
    outs = epi(accs, [e[...] for e in e_refs])
    for o_ref, o in zip(o_refs, outs):
        o_ref[...] = o.astype(o_ref.dtype)


def _matmul(a_list, w_list, a_of_w, extras, epi, out_dtypes, n_cols, *, tm, tn, eids=None, norm=None, name):
    m = a_list[0].shape[0]
    in_specs, blocks, scratch = [], 0, []
    for a in a_list:
        in_specs.append(pl.BlockSpec((tm, a.shape[1]), lambda i, j, *_: (i, 0)))
        blocks += _nbytes((tm, a.shape[1]), a.dtype)
    for w in w_list:
        if w.ndim == 2:
            in_specs.append(pl.BlockSpec((w.shape[0], tn), lambda i, j, *_: (0, j)))
        else:
            in_specs.append(pl.BlockSpec((1, w.shape[1], tn), lambda i, j, e: (e[i], 0, j)))
        blocks += _nbytes((w.shape[-2], tn), w.dtype)
    norm_args = []
    if norm is not None:
        g, scale, shift, seq = norm
        k = a_list[0].shape[1]
        assert len(a_list) == 1 and seq % tm == 0
        per_b = seq // tm
        in_specs += [pl.BlockSpec((1, k), lambda i, j, *_: (0, 0))]
        in_specs += [pl.BlockSpec((1, 1, k), lambda i, j, *_: (i // per_b, 0, 0))] * 2
        norm_args = [g.reshape(1, k), scale, shift]
        scratch = [pltpu.VMEM((tm, k), BF16)]
        blocks += _nbytes((tm, k), BF16)
    for arr, bs, im in extras:
        in_specs.append(pl.BlockSpec(bs, im))
        blocks += _nbytes(bs, arr.dtype)
    out_specs = [pl.BlockSpec((tm, tn), lambda i, j, *_: (i, j)) for _ in out_dtypes]
    out_shape = [jax.ShapeDtypeStruct((m, n_cols), dt) for dt in out_dtypes]
    blocks += sum(_nbytes((tm, tn), dt) for dt in out_dtypes) + len(w_list) * _nbytes((tm, tn), F32)
    kern = functools.partial(_mm_kernel, n_a=len(a_list), a_of_w=tuple(a_of_w), n_extra=len(extras),
                             epi=epi, has_eids=eids is not None, normed=norm is not None)
    grid_spec = pltpu.PrefetchScalarGridSpec(
        num_scalar_prefetch=0 if eids is None else 1, grid=(m // tm, n_cols // tn),
        in_specs=in_specs, out_specs=out_specs, scratch_shapes=scratch)
    args = (([] if eids is None else [eids]) + list(a_list) + list(w_list) + norm_args
            + [e[0] for e in extras])
    outs = pl.pallas_call(kern, grid_spec=grid_spec, out_shape=out_shape,
                          compiler_params=_cparams(("arbitrary", "arbitrary"), blocks), name=name)(*args)
    return outs


def _nt_dot(a, b):
    return lax.dot_general(a, b, (((1,), (1,)), ((), ())), preferred_element_type=F32)


def _lane_blocks(x):
    return [x[:, c * LANES:(c + 1) * LANES] for c in range(x.shape[1] // LANES)]


def _softmax_step(lg, off, vt, carry):
    m, l, acc = carry
    rows, width = lg.shape
    block_max = functools.reduce(jnp.maximum, _lane_blocks(lg))
    row_max = jnp.broadcast_to(jnp.max(block_max, axis=-1, keepdims=True), (rows, LANES))
    m_new = jnp.maximum(m, row_max + off)
    alpha = jnp.exp2(m - m_new)
    p = jnp.exp2(lg - jnp.tile(m_new - off, (1, width // LANES)))
    l = alpha * l + functools.reduce(jnp.add, _lane_blocks(p))
    acc = (jnp.tile(alpha, (1, acc.shape[1] // LANES)) * acc
           + jnp.dot(p.astype(BF16), vt, preferred_element_type=F32))
    return m_new, l, acc


def _softmax_init(rows, width):
    return (jnp.full((rows, LANES), NEG, F32), jnp.zeros((rows, LANES), F32), jnp.zeros((rows, width), F32))


def _softmax_finish(carry):
    _, l, acc = carry
    return acc / jnp.sum(l, axis=-1, keepdims=True)


def _rel_pos(rows, cols):
    return (lax.broadcasted_iota(I32, (rows, cols), 1) - lax.broadcasted_iota(I32, (rows, cols), 0)).astype(F32)


def _tile_loops(n_full, n_kv, past_step, edge_step, carry):
    n_wide = n_full // WIDE_TILES
    carry = lax.fori_loop(0, n_wide, lambda i, c: past_step(i * WIDE_TILES, WIDE_TILES, c), carry)
    done = n_wide * WIDE_TILES
    n_pair = (n_full - done) // 2
    carry = lax.fori_loop(0, n_pair, lambda i, c: past_step(done + 2 * i, 2, c), carry)
    carry = lax.fori_loop(done + 2 * n_pair, n_full, lambda t, c: past_step(t, 1, c), carry)
    return lax.fori_loop(n_full, n_kv, edge_step, carry)


def _num_kv_tiles(q0, tq, tk, l_valid):
    kmax = jnp.minimum(l_valid, (((q0 + tq - 1) >> CHUNK_SHIFT) + 1) * CHUNK)
    return (kmax + tk - 1) // tk


def _dsa_kernel(qa_ref, iq_ref, iw_ref, k_ref, v_ref, ik_ref, o_ref, keys_ref, bias_ref, relpos_ref, *,
                tq, tk, n_past, l_valid, topk, index_bits):
    q0 = n_past + pl.program_id(1) * tq
    n_kv = _num_kv_tiles(q0, tq, tk, l_valid)
    row = lax.broadcasted_iota(I32, (tq, tk), 0)
    col = lax.broadcasted_iota(I32, (tq, tk), 1)
    qpos = q0 + row
    lane = lax.broadcasted_iota(I32, (tq, LANES), 1)

    iw = iw_ref[0] * (I_Q ** -0.5)
    iq = iq_ref[0].astype(F32)
    iq_heads = []
    for h in range(IDX_HEADS):
        pair = iq[:, (h // 2) * LANES:(h // 2 + 1) * LANES]
        keep = (lane < IDX_DIM) if h % 2 == 0 else (lane >= IDX_DIM)
        iq_heads.append(jnp.where(keep, pair, 0.0).astype(BF16))
    iq_all = jnp.concatenate(iq_heads, axis=0)

    def to_key(x):
        bits = lax.bitcast_convert_type(x, I32)
        return bits ^ ((bits >> 31) & 0x7FFFFFFF)

    def score_tiles(t, n, carry):
        top1, top2 = carry
        start = pl.multiple_of(t * tk, tk)
        ikt = ik_ref[0, pl.ds(start, n * tk), :]
        rel = jnp.maximum(_nt_dot(iq_all, ikt), 0.0)
        score = jnp.zeros((tq, n * tk), F32)
        for h in range(IDX_HEADS):
            score = score + iw[:, h:h + 1] * rel[h * tq:(h + 1) * tq]
        for i in range(n):
            kpos = start + i * tk + col
            masked = jnp.where((kpos >> CHUNK_SHIFT) <= (qpos >> CHUNK_SHIFT), score[:, i * tk:(i + 1) * tk], -jnp.inf)
            if l_valid % tk:
                masked = jnp.where(kpos < l_valid, masked, -jnp.inf)
            keys_ref[t + i] = to_key(masked)
            for blk in _lane_blocks(masked):
                top2 = jnp.maximum(top2, jnp.minimum(top1, blk))
                top1 = jnp.maximum(top1, blk)
        return top1, top2

    n_pair = n_kv // 2
    lows = jnp.full((tq, LANES), -jnp.inf, F32)
    tops = lax.fori_loop(0, n_pair, lambda i, c: score_tiles(2 * i, 2, c), (lows, lows))
    top1, top2 = lax.fori_loop(2 * n_pair, n_kv, lambda t, c: score_tiles(t, 1, c), tops)

    def from_key(k):
        return lax.bitcast_convert_type(k ^ ((k >> 31) & 0x7FFFFFFF), F32)

    def reduce_tiles(tile_fn, combine, init):
        def body(t, acc):
            return combine(acc, functools.reduce(combine, _lane_blocks(tile_fn(keys_ref[t], t))))
        return lax.fori_loop(0, n_kv, body, init)

    def count(pred):
        per_lane = reduce_tiles(lambda k, t: jnp.where(pred(k, t), 1.0, 0.0), jnp.add, jnp.zeros((tq, LANES), F32))
        return jnp.sum(per_lane, axis=-1, keepdims=True)

    def largest_at_most(bound):
        per_lane = reduce_tiles(lambda k, t: jnp.where(k <= bound, k, MASKED_KEY), jnp.maximum,
                                jnp.full((tq, LANES), MASKED_KEY, I32))
        return to_key(jnp.max(from_key(per_lane), axis=-1, keepdims=True))

    def halve(lo, hi, cnt_lo):
        gap = hi - lo
        mid = lo + lax.shift_right_logical(gap, 1) + (gap & 1)
        cnt = count(lambda k, t: k >= mid)
        keep = cnt >= topk
        return jnp.where(keep, mid, lo), jnp.where(keep, hi, mid - 1), jnp.where(keep, cnt, cnt_lo)

    def unfinished(lo, hi, cnt_lo):
        return (lo != hi) & (cnt_lo != topk)

    def any_of(flags):
        return jnp.max(jnp.where(flags, 1.0, 0.0))

    def any_wide(lo, hi, cnt_lo):
        wide = lax.shift_right_logical(hi - lo, NARROW_BITS) != 0
        return any_of(unfinished(lo, hi, cnt_lo) & wide)

    def halve_wide(state):
        lo, hi, cnt_lo = halve(*halve(*state[:3]))
        return lo, hi, cnt_lo, any_wide(lo, hi, cnt_lo)

    def step_down(state):
        lo, hi, cnt_lo = state[:3]
        todo = unfinished(lo, hi, cnt_lo)
        top = largest_at_most(hi)
        cnt = count(lambda k, t: k >= top)
        hit = cnt >= topk
        lo = jnp.where(todo & hit, top, lo)
        hi = jnp.where(todo, jnp.where(hit, top, top - 1), hi)
        cnt_lo = jnp.where(todo & hit, cnt, cnt_lo)
        lo, hi, cnt_lo = halve(lo, hi, cnt_lo)
        return lo, hi, cnt_lo, any_of(unfinished(lo, hi, cnt_lo))

    lo0 = to_key(jnp.min(top2, axis=-1, keepdims=True))
    hi0 = to_key(jnp.max(top1, axis=-1, keepdims=True))
    unknown = jnp.full((tq, 1), -1.0, F32)
    going = lambda state: state[3] > 0.0
    lo, hi, cnt_lo, _ = lax.while_loop(going, halve_wide, (lo0, hi0, unknown, any_wide(lo0, hi0, unknown)))
    thr = lax.while_loop(going, step_down, (lo, hi, cnt_lo, any_of(unfinished(lo, hi, cnt_lo))))[0]
    n_gt = count(lambda k, t: k > thr)

    cut = jnp.maximum(thr, MASKED_KEY + 1)

    def at_or_above(k, t):
        bias_ref[t] = jnp.where(k >= cut, 0.0, NEG)
        return k >= thr

    n_ge = count(at_or_above)
    want_ties = topk - n_gt
    need = ((n_ge - n_gt) > want_ties) & (thr > MASKED_KEY)

    @pl.when(jnp.max(jnp.where(need, 1.0, 0.0)) > 0.0)
    def _():
        def index_bit(it, j_lim):
            cand = j_lim | lax.shift_left(jnp.int32(1), index_bits - 1 - it)
            below = count(lambda k, t: (k == thr) & ((t * tk + col) < cand))
            return jnp.where(below <= want_ties - 1.0, cand, j_lim)
        j_lim = lax.fori_loop(0, index_bits, index_bit, jnp.zeros((tq, 1), I32))
        j_lim = jnp.where(need, j_lim, NO_LIMIT)

        def bias_tile(t, carry):
            k = keys_ref[t]
            sel = (k > thr) | ((k == thr) & ((t * tk + col) <= j_lim))
            bias_ref[t] = jnp.where(sel & (k > MASKED_KEY), 0.0, NEG)
            return carry

        lax.fori_loop(0, n_kv, bias_tile, 0)

    qk_scale = HEAD_DIM ** -0.5 * LOG2E
    n_full = jnp.minimum(q0, l_valid) // tk

    @pl.when((pl.program_id(0) == 0) & (pl.program_id(1) == 0))
    def _():
        relpos_ref[...] = _rel_pos(tq, WIDE_TILES * tk)

    for j in range(A_KV_HEADS):
        q4 = jnp.concatenate([qa_ref[0, :, (j * A_GROUP + g) * HEAD_DIM:(j * A_GROUP + g + 1) * HEAD_DIM]
                              for g in range(A_GROUP)], axis=0)
        slopes2 = [2.0 ** -(j * A_GROUP + g + 1) * LOG2E for g in range(A_GROUP)]
        slope_rows = jnp.concatenate([jnp.full((tq, LANES), s2, F32) for s2 in slopes2], axis=0)

        def tile_operands(t, n, j=j, q4=q4):
            start = pl.multiple_of(t * tk, tk)
            kt = k_ref[0, pl.ds(start, n * tk), j * HEAD_DIM:(j + 1) * HEAD_DIM]
            vt = v_ref[0, pl.ds(start, n * tk), j * HEAD_DIM:(j + 1) * HEAD_DIM]
            bias = bias_ref[t] if n == 1 else jnp.concatenate([bias_ref[t + i] for i in range(n)], axis=1)
            return start, _nt_dot(q4, kt), vt, bias

        def past_step(t, n, carry, slopes2=slopes2, slope_rows=slope_rows):
            start, s, vt, bias = tile_operands(t, n)
            lg = jnp.concatenate([s[g * tq:(g + 1) * tq] * qk_scale + (relpos_ref[:, :n * tk] * slopes2[g] + bias)
                                  for g in range(A_GROUP)], axis=0)
            off = slope_rows * (start - q0).astype(F32)
            return _softmax_step(lg, off, vt, carry)

        def edge_step(t, carry, slopes2=slopes2):
            start, s, vt, bias = tile_operands(t, 1)
            dist = jnp.abs(qpos - (start + col)).astype(F32)
            lg = jnp.concatenate([s[g * tq:(g + 1) * tq] * qk_scale - slopes2[g] * dist + bias
                                  for g in range(A_GROUP)], axis=0)
            return _softmax_step(lg, 0.0, vt, carry)

        out = _softmax_finish(_tile_loops(n_full, n_kv, past_step, edge_step,
                                          _softmax_init(A_GROUP * tq, HEAD_DIM)))
        for g in range(A_GROUP):
            h = j * A_GROUP + g
            o_ref[0, :, h * HEAD_DIM:(h + 1) * HEAD_DIM] = out[g * tq:(g + 1) * tq].astype(o_ref.dtype)


def _dsa_attention(zb, zf, keys, off, *, n_past, l_valid, tq, tk, topk, name):
    b, s, _ = zb.shape
    karr, kcol, varr, vcol, ikarr, ikcol = keys
    lp = karr.shape[1]
    n_tiles = lp // tk
    assert topk <= min(tk, 2 * LANES) and WIDE_TILES % 2 == 0
    kern = functools.partial(_dsa_kernel, tq=tq, tk=tk, n_past=n_past, l_valid=l_valid, topk=topk,
                             index_bits=lp.bit_length())
    blocks = (_nbytes((tq, A_Q + I_Q), BF16) + _nbytes((tq, LANES), F32) + 2 * _nbytes((lp, A_KV), BF16)
              + _nbytes((lp, LANES), BF16) + _nbytes((tq, A_Q), BF16) + _nbytes((tq, lp), F32)
              + 8 * _nbytes((A_GROUP * tq, tk), F32))
    return pl.pallas_call(
        kern,
        grid=(b, s // tq),
        in_specs=[pl.BlockSpec((1, tq, A_Q), lambda bi, qi: (bi, qi, off['aq'] // A_Q)),
                  pl.BlockSpec((1, tq, I_Q), lambda bi, qi: (bi, qi, off['iq'] // I_Q)),
                  pl.BlockSpec((1, tq, LANES), lambda bi, qi: (bi, qi, off['iw'] // LANES)),
                  pl.BlockSpec((1, lp, A_KV), lambda bi, qi: (bi, 0, kcol)),
                  pl.BlockSpec((1, lp, A_KV), lambda bi, qi: (bi, 0, vcol)),
                  pl.BlockSpec((1, lp, LANES), lambda bi, qi: (bi, 0, ikcol))],
        out_specs=pl.BlockSpec((1, tq, A_Q), lambda bi, qi: (bi, qi, 0)),
        out_shape=jax.ShapeDtypeStruct((b, s, A_Q), BF16),
        scratch_shapes=[pltpu.VMEM((n_tiles, tq, tk), I32), pltpu.VMEM((n_tiles, tq, tk), F32),
                        pltpu.VMEM((tq, WIDE_TILES * tk), F32)],
        compiler_params=_cparams(("arbitrary", "arbitrary"), blocks),
        name=name,
    )(zb, zb, zf, karr, varr, ikarr)


def _diff_kernel(q_ref, k_ref, v_ref, lq_ref, g_ref, o_ref, relbias_ref, *, tq, tk, n_past, l_valid, lam_init):
    h = pl.program_id(1)
    q0 = n_past + pl.program_id(2) * tq
    n_kv = _num_kv_tiles(q0, tq, tk, l_valid)
    row = lax.broadcasted_iota(I32, (tq, tk), 0)
    col = lax.broadcasted_iota(I32, (tq, tk), 1)
    qpos = q0 + row
    slope2 = jnp.float32(1.0)
    for hh in range(B_HEADS):
        slope2 = jnp.where(h == hh, jnp.float32(2.0 ** (-8.0 * (hh + 1) / B_HEADS) * LOG2E), slope2)
    qk_scale = HEAD_DIM ** -0.5 * LOG2E
    n_full = jnp.minimum(q0, l_valid) // tk

    @pl.when(pl.program_id(2) == 0)
    def _():
        relbias_ref[...] = _rel_pos(tq, WIDE_TILES * tk) * slope2

    qc = [q_ref[0, :, c * HEAD_DIM:(c + 1) * HEAD_DIM] for c in range(2)]

    def tile_operands(t, n):
        start = pl.multiple_of(t * tk, tk)
        s = [_nt_dot(qc[c], k_ref[0, pl.ds(start, n * tk), c * HEAD_DIM:(c + 1) * HEAD_DIM]) for c in range(2)]
        return start, s, v_ref[0, pl.ds(start, n * tk), :]

    def past_step(t, n, carry):
        start, s, vt = tile_operands(t, n)
        lg = jnp.concatenate([s[c] * qk_scale + relbias_ref[:, :n * tk] for c in range(2)], axis=0)
        return _softmax_step(lg, slope2 * (start - q0).astype(F32), vt, carry)

    def edge_step(t, carry):
        start, s, vt = tile_operands(t, 1)
        kpos = start + col
        bias = jnp.where((kpos >> CHUNK_SHIFT) <= (qpos >> CHUNK_SHIFT),
                         -slope2 * jnp.abs(qpos - kpos).astype(F32), NEG)
        if l_valid % tk:
            bias = jnp.where(kpos < l_valid, bias, NEG)
        lg = jnp.concatenate([s[c] * qk_scale + bias for c in range(2)], axis=0)
        return _softmax_step(lg, 0.0, vt, carry)

    out = _softmax_finish(_tile_loops(n_full, n_kv, past_step, edge_step,
                                      _softmax_init(2 * tq, 2 * HEAD_DIM)))
    lq = lq_ref[...]
    lam = (jnp.exp(jnp.sum(lq[0:1] * lq[1:2], axis=-1, keepdims=True))
           - jnp.exp(jnp.sum(lq[2:3] * lq[3:4], axis=-1, keepdims=True)) + lam_init)
    o = out[:tq] - lam * out[tq:]
    y = o * lax.rsqrt(jnp.mean(o * o, axis=-1, keepdims=True) + SUBLN_EPS)
    y = (y * g_ref[...]) * (1.0 - lam_init)
    o_ref[0] = y.astype(o_ref.dtype)


def _diff_attention(zb, keys, lam_qk_l, g_subln_l, off, *, n_past, l_valid, tq, tk, lam_init, name):
    b, s, _ = zb.shape
    karr, kcol0, varr, vcol0 = keys
    lp = karr.shape[1]
    w = 2 * HEAD_DIM
    kern = functools.partial(_diff_kernel, tq=tq, tk=tk, n_past=n_past, l_valid=l_valid, lam_init=lam_init)
    blocks = (2 * _nbytes((tq, w), BF16) + 2 * _nbytes((lp, w), BF16) + 8 * _nbytes((2 * tq, tk), F32))
    return pl.pallas_call(
        kern,
        grid=(b, B_HEADS, s // tq),
        in_specs=[pl.BlockSpec((1, tq, w), lambda bi, h, qi: (bi, qi, off['bq'] // w + h)),
                  pl.BlockSpec((1, lp, w), lambda bi, h, qi: (bi, 0, kcol0 + h)),
                  pl.BlockSpec((1, lp, w), lambda bi, h, qi: (bi, 0, vcol0 + h)),
                  pl.BlockSpec((4, HEAD_DIM), lambda bi, h, qi: (0, 0)),
                  pl.BlockSpec((1, w), lambda bi, h, qi: (0, 0))],
        out_specs=pl.BlockSpec((1, tq, w), lambda bi, h, qi: (bi, qi, h)),
        out_shape=jax.ShapeDtypeStruct((b, s, B_V), BF16),
        scratch_shapes=[pltpu.VMEM((tq, WIDE_TILES * tk), F32)],
        compiler_params=_cparams(("arbitrary", "arbitrary", "arbitrary"), blocks),
        name=name,
    )(zb, karr, varr, lam_qk_l, g_subln_l.reshape(1, w))


def _row_gather(src_ref, idx_ref, idx_base, idx_stride, dst_ref, sem, n_rows):
    def row_copy(r):
        return pltpu.make_async_copy(src_ref.at[pl.ds(idx_ref[idx_base + r * idx_stride], 1)],
                                     dst_ref.at[pl.ds(r, 1)], sem)

    def start():
        lax.fori_loop(0, n_rows, lambda r, c: (row_copy(r).start(), c)[1], 0, unroll=ISSUE_UNROLL)

    def wait():
        pltpu.make_async_copy(dst_ref.at[pl.ds(0, n_rows)], dst_ref.at[pl.ds(0, n_rows)], sem).wait()

    return start, wait


def _moe_up_kernel(eid_ref, tok_ref, h_ref, wg_ref, wu_ref, o_ref, rows_ref, rows16_ref, sems, *,
                   blk, n_blocks, n_col_steps, rows_per_step):
    i, j = pl.program_id(0), pl.program_id(1)
    slot = i % 2
    n_issued = n_col_steps * rows_per_step

    def row_copy(block, buf, r_dst, r_src):
        return pltpu.make_async_copy(h_ref.at[pl.ds(tok_ref[block * blk + r_src], 1)],
                                     rows_ref.at[buf, pl.ds(r_dst, 1)], sems.at[buf])

    def wait_all(buf):
        pltpu.make_async_copy(rows_ref.at[buf, pl.ds(0, n_issued)], rows_ref.at[buf, pl.ds(0, n_issued)],
                              sems.at[buf]).wait()

    @pl.when((i == 0) & (j == 0))
    def _():
        def first(r, c):
            row_copy(0, 0, r, jnp.minimum(r, blk - 1)).start()
            return c
        lax.fori_loop(0, n_issued, first, 0, unroll=ISSUE_UNROLL)

    @pl.when(j == 0)
    def _():
        wait_all(slot)
        rows16_ref[...] = rows_ref[slot, :blk].astype(BF16)

    nxt = jnp.minimum(i + 1, n_blocks - 1)
    for r in range(rows_per_step):
        r_dst = j * rows_per_step + r
        row_copy(nxt, 1 - slot, r_dst, jnp.minimum(r_dst, blk - 1)).start()

    a = rows16_ref[...]
    u = jnp.dot(a, wg_ref[0], preferred_element_type=F32)
    v = jnp.dot(a, wu_ref[0], preferred_element_type=F32)
    o_ref[...] = ((u * jax.nn.sigmoid(u)) * v).astype(o_ref.dtype)

    @pl.when((i == n_blocks - 1) & (j == n_col_steps - 1))
    def _():
        wait_all(1 - slot)


def _moe_up(h2, tok_buf, block_expert, w_gate, w_up, *, blk, name):
    p = tok_buf.shape[0]
    d = h2.shape[1]
    fp = w_gate.shape[-1]
    tn = _tile(fp, 512)
    n_blocks = p // blk
    n_col_steps = fp // tn
    rows_per_step = _round_up(-(-blk // n_col_steps), 8)
    n_rows_buf = n_col_steps * rows_per_step
    blocks = (_nbytes((n_rows_buf, d), F32) + _nbytes((blk, d), BF16) // 2 + 2 * _nbytes((d, tn), BF16)
              + _nbytes((blk, tn), BF16) + _nbytes((blk, tn), F32))
    return pl.pallas_call(
        functools.partial(_moe_up_kernel, blk=blk, n_blocks=n_blocks, n_col_steps=n_col_steps,
                          rows_per_step=rows_per_step),
        grid_spec=pltpu.PrefetchScalarGridSpec(
            num_scalar_prefetch=2, grid=(n_blocks, n_col_steps),
            in_specs=[pl.BlockSpec(memory_space=pl.ANY),
                      pl.BlockSpec((1, d, tn), lambda i, j, e, tok: (e[i], 0, j)),
                      pl.BlockSpec((1, d, tn), lambda i, j, e, tok: (e[i], 0, j))],
            out_specs=pl.BlockSpec((blk, tn), lambda i, j, e, tok: (i, j)),
            scratch_shapes=[pltpu.VMEM((2, n_rows_buf, d), F32), pltpu.VMEM((blk, d), BF16),
                            pltpu.SemaphoreType.DMA((2,))]),
        out_shape=jax.ShapeDtypeStruct((p, fp), BF16),
        compiler_params=_cparams(("arbitrary", "arbitrary"), blocks),
        name=name,
    )(block_expert, tok_buf, h2, w_gate, w_up)


def _combine_kernel(dest_ref, x_ref, y_ref, rg_ref, gate_ref, o_ref, picked_ref, sems, *, tm, n_steps):
    i = pl.program_id(0)
    slot = i % 2

    def fetch(step, buf):
        base = step * tm * TOP_K_EXPERTS
        return [_row_gather(y_ref, dest_ref, base + k, TOP_K_EXPERTS, picked_ref.at[buf, k], sems.at[buf, k], tm)
                for k in range(TOP_K_EXPERTS)]

    @pl.when(i == 0)
    def _():
        for start, _ in fetch(0, 0):
            start()

    for _, wait in fetch(i, slot):
        wait()

    @pl.when(i + 1 < n_steps)
    def _():
        for start, _ in fetch(i + 1, 1 - slot):
            start()

    rg = rg_ref[...]
    f = rg[:, 0:1] * picked_ref[slot, 0] + rg[:, 1:2] * picked_ref[slot, 1]
    o_ref[...] = x_ref[...] + gate_ref[0] * f


def _moe_combine(x2, yb, dest, rgate, gate_f, seq, *, name):
    t, d = x2.shape
    tm = _tile(seq, 256)
    per_b = seq // tm
    return pl.pallas_call(
        functools.partial(_combine_kernel, tm=tm, n_steps=t // tm),
        grid_spec=pltpu.PrefetchScalarGridSpec(
            num_scalar_prefetch=1, grid=(t // tm,),
            in_specs=[pl.BlockSpec((tm, d), lambda i, dst: (i, 0)),
                      pl.BlockSpec(memory_space=pl.ANY),
                      pl.BlockSpec((tm, LANES), lambda i, dst: (i, 0)),
                      pl.BlockSpec((1, 1, d), lambda i, dst: (i // per_b, 0, 0))],
            out_specs=pl.BlockSpec((tm, d), lambda i, dst: (i, 0)),
            scratch_shapes=[pltpu.VMEM((2, TOP_K_EXPERTS, tm, d), F32),
                            pltpu.SemaphoreType.DMA((2, TOP_K_EXPERTS))]),
        out_shape=jax.ShapeDtypeStruct((t, d), F32),
        compiler_params=_cparams(("arbitrary",), 4 * _nbytes((tm, d), F32)),
        name=name,
    )(dest, x2, yb, rgate, gate_f)


STATE_COLS = (('ak', A_KV), ('av', A_KV), ('ik', LANES), ('bk', B_QK), ('bv', B_V))


def _state_tails():
    return ((A_KV_HEADS, HEAD_DIM), (A_KV_HEADS, HEAD_DIM), (IDX_DIM,), (B_HEADS, 2, HEAD_DIM),
            (B_HEADS, 2 * HEAD_DIM))


def _state_kernel(*refs, depth):
    n = len(STATE_COLS)
    ins, outs = refs[:depth * n], refs[depth * n:]
    for l in range(depth):
        @pl.when(pl.program_id(0) == l)
        def _(l=l):
            ak, av, ik, bk, bv = ins[l * n:(l + 1) * n]
            for j in range(A_KV_HEADS):
                outs[0][0, 0, :, j, :] = ak[0, :, j * HEAD_DIM:(j + 1) * HEAD_DIM]
                outs[1][0, 0, :, j, :] = av[0, :, j * HEAD_DIM:(j + 1) * HEAD_DIM]
            outs[2][0, 0] = ik[0, :, :IDX_DIM]
            for h in range(B_HEADS):
                for c in range(2):
                    lo = (2 * h + c) * HEAD_DIM
                    outs[3][0, 0, :, h, c, :] = bk[0, :, lo:lo + HEAD_DIM]
                outs[4][0, 0, :, h, :] = bv[0, :, 2 * h * HEAD_DIM:2 * (h + 1) * HEAD_DIM]


def _state_rows(zf3_layers, off, *, name):
    depth = len(zf3_layers)
    b, s, _ = zf3_layers[0].shape
    ts = _tile(s, 256)
    in_specs, args = [], []
    for l in range(depth):
        for cname, width in STATE_COLS:
            blk = off[cname] // width
            in_specs.append(pl.BlockSpec(
                (1, ts, width),
                lambda li, bi, i, l=l, blk=blk: (jnp.where(li == l, bi, 0), jnp.where(li == l, i, 0), blk)))
            args.append(zf3_layers[l])
    tails = _state_tails()
    out_specs = [pl.BlockSpec((1, 1, ts) + t, lambda li, bi, i, nt=len(t): (li, bi, i) + (0,) * nt) for t in tails]
    out_shape = [jax.ShapeDtypeStruct((depth, b, s) + t, F32) for t in tails]
    blocks = (depth + 1) * sum(_nbytes((ts, w), F32) for _, w in STATE_COLS)
    return pl.pallas_call(
        functools.partial(_state_kernel, depth=depth),
        grid=(depth, b, s // ts), in_specs=in_specs, out_specs=out_specs, out_shape=out_shape,
        compiler_params=_cparams(("arbitrary", "arbitrary", "arbitrary"), blocks), name=name,
    )(*args)


def _pack_kernel(c_hbm, new_ref, o_ref, buf, sems, *, layer, tk, n_cached_tiles, tail):
    b, i = pl.program_id(0), pl.program_id(1)
    slot = i % 2
    piece = tail[-1]
    heads = list(itertools.product(*[range(t) for t in tail[:-1]]))

    def tile_copies(tile, s):
        rows = pl.ds(tile * tk, tk)
        return [pltpu.make_async_copy(c_hbm.at[(layer, b, rows) + idx + (slice(None),)], buf.at[s, n], sems.at[s, n])
                for n, idx in enumerate(heads)]

    @pl.when(i == 0)
    def _():
        for c in tile_copies(0, 0):
            c.start()

    @pl.when(i < n_cached_tiles)
    def _():
        for c in tile_copies(i, slot):
            c.wait()

        @pl.when(i + 1 < n_cached_tiles)
        def _():
            for c in tile_copies(i + 1, 1 - slot):
                c.start()

        for n in range(len(heads)):
            o_ref[0, :, n * piece:(n + 1) * piece] = buf[slot, n].astype(BF16)

    @pl.when(i >= n_cached_tiles)
    def _():
        o_ref[0] = new_ref[0]


def _pack_keys(cache, layer, new_rows, *, tk, name):
    _, b, p = cache.shape[:3]
    tail = cache.shape[3:]
    w = math.prod(tail)
    assert p % tk == 0 and new_rows.shape == (b, tk, w)
    n_cached_tiles = p // tk
    n_heads = w // tail[-1]
    return pl.pallas_call(
        functools.partial(_pack_kernel, layer=layer, tk=tk, n_cached_tiles=n_cached_tiles, tail=tail),
        grid=(b, n_cached_tiles + 1),
        in_specs=[pl.BlockSpec(memory_space=pl.ANY),
                  pl.BlockSpec((1, tk, w), lambda bi, i: (bi, 0, 0))],
        out_specs=pl.BlockSpec((1, tk, w), lambda bi, i: (bi, i, 0)),
        out_shape=jax.ShapeDtypeStruct((b, p + tk, w), BF16),
        scratch_shapes=[pltpu.VMEM((2, n_heads, tk, tail[-1]), F32), pltpu.SemaphoreType.DMA((2, n_heads))],
        compiler_params=_cparams(("arbitrary", "arbitrary"), _nbytes((tk, w), F32) + 2 * _nbytes((tk, w), BF16)),
        name=name,
    )(cache, new_rows)


def _silu_mul(accs, extras):
    u, v = accs
    return [(u * jax.nn.sigmoid(u)) * v]


def _moe_ffn(x2, h2, ridx, rgate, gate_f, seq, w_gate, w_up, w_down, *, tag):
    t, d = x2.shape
    fp = w_gate.shape[-1]
    a = t * TOP_K_EXPERTS
    blk = _tile(a, 512)
    n_blocks = -(-(a + N_EXPERTS * (blk - 1)) // blk)
    p = n_blocks * blk
    e_flat = ridx[:, :TOP_K_EXPERTS].reshape(-1)
    onehot = (e_flat[:, None] == jnp.arange(N_EXPERTS, dtype=I32)[None, :]).astype(I32)
    rank = jnp.sum((jnp.cumsum(onehot, axis=0) - onehot) * onehot, axis=1)
    counts = jnp.sum(onehot, axis=0)
    padded = ((counts + blk - 1) // blk) * blk
    pends = jnp.cumsum(padded)
    pstarts = pends - padded
    dest = (pstarts[e_flat] + rank).astype(I32)
    tok_buf = jnp.zeros((p,), I32).at[dest].set(jnp.arange(a, dtype=I32) // TOP_K_EXPERTS)
    block_expert = jnp.clip(jnp.searchsorted(pends, jnp.arange(n_blocks, dtype=I32) * blk, side='right'),
                            0, N_EXPERTS - 1).astype(I32)

    act = _moe_up(h2, tok_buf, block_expert, w_gate, w_up, blk=blk, name=f"moe_up_{tag}")
    (yb,) = _matmul([act], [w_down], [0], [], lambda accs, ex: accs, [F32], d, tm=blk, tn=_tile(d, 512),
                    eids=block_expert, name=f"moe_down_{tag}")
    return _moe_combine(x2, yb, dest, rgate, gate_f, seq, name=f"moe_combine_{tag}")


def _trunk(x, mods, past_all, wts, tag):
    b, s, d = x.shape
    t = b * s
    depth = len(wts['w_in'])
    off, nz = _z_layout(d)
    x2 = x.reshape(t, d)
    tn_d = _tile(d, 512)
    tm_b = _tile(s, 512)
    tm_f = _tile(t, 1024)
    per_b = s // tm_b
    fuse_norm = s % tm_f == 0
    zf3_layers = []
    for l in range(depth):
        shift_a, scale_a, gate_a, shift_f, scale_f, gate_f = [
            mods[l][:, i * d:(i + 1) * d].reshape(b, 1, d) for i in range(N_ADA)]
        batch_tile = lambda i, j, *_: (i // per_b, 0, j)

        if fuse_norm:
            h, norm_a = x2, (wts['g_attn'][l], scale_a, shift_a, s)
        else:
            h, norm_a = _norm(x2, wts['g_attn'][l], s, scale=scale_a, shift=shift_a, out_dtype=BF16,
                              name=f"norm_attn_{tag}{l}"), None
        zf, zb = _matmul([h], [wts['w_in'][l]], [0], [], lambda accs, ex: [accs[0], accs[0]], [F32, BF16], nz,
                         tm=tm_f, tn=IN_PROJ_TN, norm=norm_a, name=f"in_proj_{tag}{l}")
        zf3, zb3 = zf.reshape(b, s, nz), zb.reshape(b, s, nz)
        cut = lambda z, name, width: z[:, :, off[name]:off[name] + width]
        zf3_layers.append(zf3)

        if past_all is None:
            n_past, l_valid = 0, s
            tk = _tile(s, 512)
            tq_a, tq_b = _tile(s, 128), _tile(s, 256)
            dsa_keys = (zb3, off['ak'] // A_KV, zb3, off['av'] // A_KV, zb3, off['ik'] // LANES)
            diff_keys = (zb3, off['bk'] // (2 * HEAD_DIM), zb3, off['bv'] // (2 * HEAD_DIM))
        else:
            n_past = past_all[0].shape[2]
            l_valid = n_past + s
            tk = 512
            tq_a = tq_b = s

            def new_rows(name, width):
                return jnp.concatenate([cut(zb3, name, width), jnp.zeros((b, tk - s, width), BF16)], axis=1)

            def with_past(ci, name, width):
                return _pack_keys(past_all[ci], l, new_rows(name, width), tk=tk, name=f"pack_{name}_{tag}{l}")

            ik_past = past_all[2][l].astype(BF16)
            ik_all = jnp.concatenate([jnp.concatenate([ik_past, ik_past], axis=-1), new_rows('ik', LANES)], axis=1)
            dsa_keys = (with_past(0, 'ak', A_KV), 0, with_past(1, 'av', A_KV), 0, ik_all, 0)
            diff_keys = (with_past(3, 'bk', B_QK), 0, with_past(4, 'bv', B_V), 0)
        topk = min(TOPK_MAX, l_valid // 4)
        oa = _dsa_attention(zb3, zf3, dsa_keys, off, n_past=n_past, l_valid=l_valid, tq=tq_a, tk=tk,
                            topk=topk, name=f"dsa_{tag}{l}")
        lam_init = 0.8 - 0.6 * math.exp(-0.3 * l)
        ob = _diff_attention(zb3, diff_keys, wts['lam_qk'][l], wts['g_subln'][l], off, n_past=n_past,
                             l_valid=l_valid, tq=tq_b, tk=tk, lam_init=lam_init, name=f"diff_{tag}{l}")

        def merge_epi(accs, ex):
            return [jax.nn.sigmoid(ex[0]) * accs[0] + jax.nn.sigmoid(ex[1]) * accs[1]]

        (merged,) = _matmul(
            [oa.reshape(t, A_Q), ob.reshape(t, B_V)], [wts['w_out_a'][l], wts['w_out_b'][l]], [0, 1],
            [(zf, (tm_f, tn_d), lambda i, j, *_: (i, off['ga'] // tn_d + j)),
             (zf, (tm_f, tn_d), lambda i, j, *_: (i, off['gb'] // tn_d + j))],
            merge_epi, [BF16], d, tm=tm_f, tn=tn_d, name=f"merge_{tag}{l}")

        def resid_epi(accs, ex):
            return [ex[0] + ex[1][0] * accs[0]]

        (x2,) = _matmul([merged], [wts['w_out'][l]], [0],
                        [(x2, (tm_b, tn_d), lambda i, j, *_: (i, j)), (gate_a, (1, 1, tn_d), batch_tile)],
                        resid_epi, [F32], d, tm=tm_b, tn=tn_d, name=f"out_proj_{tag}{l}")

        if l % 2 == 0:
            i = l // 2
            if fuse_norm:
                h, norm_f = x2, (wts['g_ffn'][l], scale_f, shift_f, s)
            else:
                h, norm_f = _norm(x2, wts['g_ffn'][l], s, scale=scale_f, shift=shift_f, out_dtype=BF16,
                                  name=f"norm_ffn_{tag}{l}"), None
            fp = wts['w_ff_gate'][i].shape[-1]
            (act,) = _matmul([h], [wts['w_ff_gate'][i], wts['w_ff_up'][i]], [0, 0], [], _silu_mul, [BF16], fp,
                             tm=tm_f, tn=_tile(fp, 512), norm=norm_f, name=f"ffn_up_{tag}{l}")
            (x2,) = _matmul([act], [wts['w_ff_down'][i]], [0],
                            [(x2, (tm_b, tn_d), lambda i, j, *_: (i, j)), (gate_f, (1, 1, tn_d), batch_tile)],
                            resid_epi, [F32], d, tm=tm_b, tn=tn_d, name=f"ffn_down_{tag}{l}")
        else:
            i = l // 2
            h2, ridx, rgate = _norm(x2, wts['g_ffn'][l], s, scale=scale_f, shift=shift_f,
                                    router=(wts['w_router'][i], wts['b_router'][i]), out_dtype=F32,
                                    name=f"norm_router_{tag}{l}")
            x2 = _moe_ffn(x2, h2, ridx, rgate, gate_f, s, wts['w_moe_gate'][i], wts['w_moe_up'][i],
                          wts['w_moe_down'][i], tag=f"{tag}{l}")
    y = _norm(x2, wts['g_final'], s, out_dtype=F32, name=f"norm_final_{tag}")
    state = tuple(_state_rows(zf3_layers, off, name=f"state_rows_{tag}"))
    return y.reshape(b, s, d), state


def _prep_weights(w_in, w_out_a, w_out_b, w_out, w_ff_gate, w_ff_up, w_ff_down, w_router, b_router,
                  w_moe_gate, w_moe_up, w_moe_down, d_model):
    f = w_ff_gate.shape[-1]
    fp = _round_up(f, 512)
    pad_cols = lambda w: jnp.pad(w, [(0, 0)] * (w.ndim - 1) + [(0, fp - f)]).astype(BF16)
    pad_rows = lambda w: jnp.pad(w, [(0, 0)] * (w.ndim - 2) + [(0, fp - f), (0, 0)]).astype(BF16)
    n_moe = w_router.shape[0]
    return dict(
        w_in=[_prep_w_in(w_in[l], d_model) for l in range(w_in.shape[0])],
        w_out_a=w_out_a.astype(BF16), w_out_b=w_out_b.astype(BF16), w_out=w_out.astype(BF16),
        w_ff_gate=pad_cols(w_ff_gate), w_ff_up=pad_cols(w_ff_up), w_ff_down=pad_rows(w_ff_down),
        w_router=[jnp.pad(w_router[i], ((0, 0), (0, LANES - N_EXPERTS))) for i in range(n_moe)],
        b_router=[jnp.pad(b_router[i], (0, LANES - N_EXPERTS)).reshape(1, LANES) for i in range(n_moe)],
        w_moe_gate=pad_cols(w_moe_gate), w_moe_up=pad_cols(w_moe_up), w_moe_down=pad_rows(w_moe_down),
    )


def kernel(x_prompt, x_sample, c_prompt, c_sample, cache_dsa_k, cache_dsa_v, cache_idx_k, cache_diff_k, cache_diff_v, w_ada, b_ada, g_attn, w_in, w_out_a, w_out_b, w_out, lam_qk, g_subln, g_ffn, w_ff_gate, w_ff_up, w_ff_down, w_router, b_router, w_moe_gate, w_moe_up, w_moe_down, g_final):
    d = x_prompt.shape[-1]
    wts = _prep_weights(w_in, w_out_a, w_out_b, w_out, w_ff_gate, w_ff_up, w_ff_down, w_router, b_router,
                        w_moe_gate, w_moe_up, w_moe_down, d)
    wts.update(g_attn=g_attn, g_ffn=g_ffn, g_final=g_final, lam_qk=lam_qk, g_subln=g_subln)
    nb_p = c_prompt.shape[0]
    mods = _ada_mod(jnp.concatenate([c_prompt, c_sample], axis=0), w_ada, b_ada)
    y_p, st_p = _trunk(x_prompt, mods[:, :nb_p], None, wts, "p")
    past_all = (cache_dsa_k, cache_dsa_v, cache_idx_k, cache_diff_k, cache_diff_v)
    y_s, st_s = _trunk(x_sample, mods[:, nb_p:], past_all, wts, "s")
    return (y_p, y_s) + st_p + st_s
```
